```python
import math
import jax
import jax.numpy as jnp
from jax import lax
import numpy as np

D_MODEL = 2048
BATCH = 2
SEQ = 4096
DEPTH = 4
DEC_BATCH = 4
DEC_SEQ = 4096
PAST_LEN = 128

PLE_DIM = 256
GRID_W = 64
RMS_EPS = 1e-6

A_HEADS = 8
A_DK = 128
A_DV = 128
A_CONV_W = 5
DN_CHUNK = 64

B_HEADS = 8
B_DH = 128
NA_ROWS = 8
NA_COLS = 16

C_HEADS = 8
C_DH = 128
ROPE_THETA = 500000.0
ROPE_DIMS = C_DH // 4
Q_BLOCK = 128

D_FF = 8192
FFN_CONV_W = 3

N_AB = (DEPTH + 1) // 2
N_C = DEPTH // 2
A_QK_W = A_HEADS * A_DK
A_V_W = A_HEADS * A_DV
A_QKV_W = 2 * A_QK_W + A_V_W
B_W = B_HEADS * B_DH
AB_IN = A_QKV_W + A_V_W + 4 * A_HEADS + 3 * B_W
AB_MIX = A_V_W + B_W
C_QK_W = C_HEADS * 2 * C_DH
C_V_W = C_HEADS * 2 * C_DH
C_IN = 2 * C_QK_W + C_V_W

kernel_name = 'hybrid_deltanet_natten_diffattn_encoder'


def rmsnorm(x, w):
    xf = x.astype(jnp.float32)
    y = xf * lax.rsqrt(jnp.mean(xf * xf, axis=-1, keepdims=True) + RMS_EPS)
    return (y * w.astype(jnp.float32)).astype(x.dtype)


def l2norm(x):
    return x * lax.rsqrt(jnp.sum(x * x, axis=-1, keepdims=True) + 1e-6)


def dwconv_centred(x, w):
    k, s = w.shape[0], x.shape[1]
    xp = jnp.pad(x, ((0, 0), (k // 2, k // 2), (0, 0)))
    return sum(xp[:, i:i + s] * w[i].astype(x.dtype) for i in range(k))


def gated_delta_chunked(q, k, v, g, beta):
    bn, h, s, dk = k.shape
    dv = v.shape[-1]
    c = DN_CHUNK
    n = s // c
    q, k, v = (t.reshape(bn, h, n, c, t.shape[-1]) for t in (q, k, v))
    g = jnp.cumsum(g.reshape(bn, h, n, c), axis=-1)
    beta = beta.reshape(bn, h, n, c)
    incl = jnp.tril(jnp.ones((c, c), dtype=bool))
    strict = jnp.tril(jnp.ones((c, c), dtype=bool), -1)
    decay = jnp.exp(jnp.where(incl, g[..., :, None] - g[..., None, :], -jnp.inf))
    kb = k * beta[..., None]
    lower = jnp.where(strict, jnp.einsum('bhnid,bhnjd->bhnij', kb, k) * decay, 0.0)
    rhs = jnp.concatenate([v * beta[..., None], kb * jnp.exp(g)[..., None]], axis=-1)
    sol = lax.linalg.triangular_solve(lower + jnp.eye(c, dtype=lower.dtype), rhs,
                                      left_side=True, lower=True)
    u, w = sol[..., :dv], sol[..., dv:]
    intra = jnp.einsum('bhnid,bhnjd->bhnij', q, k) * decay

    def step(state, xs):
        q_c, k_c, u_c, w_c, g_c, a_c = xs
        v_new = u_c - jnp.einsum('bhck,bhkv->bhcv', w_c, state)
        o = (jnp.einsum('bhck,bhkv->bhcv', q_c * jnp.exp(g_c)[..., None], state)
             + jnp.einsum('bhcm,bhmv->bhcv', a_c, v_new))
        g_last = g_c[..., -1]
        state = (state * jnp.exp(g_last)[..., None, None]
                 + jnp.einsum('bhck,bhcv->bhkv', k_c * jnp.exp(g_last[..., None] - g_c)[..., None], v_new))
        return state, o

    xs = tuple(jnp.moveaxis(t, 2, 0) for t in (q, k, u, w, g, intra))
    _, o = lax.scan(step, jnp.zeros((bn, h, dk, dv), jnp.float32), xs)
    return jnp.moveaxis(o, 0, 2).reshape(bn, h, s, dv)


def deltanet_mixer(qkv, z, gates, conv_w, a_log, dt_bias, out_norm):
    bn, s, _ = qkv.shape
    f32 = jnp.float32
    qkv = jax.nn.silu(dwconv_centred(qkv, conv_w)).astype(f32)
    q = l2norm(qkv[..., :A_QK_W].reshape(bn, s, A_HEADS, A_DK)) * (A_DK ** -0.5)
    k = l2norm(qkv[..., A_QK_W:2 * A_QK_W].reshape(bn, s, A_HEADS, A_DK))
    v = qkv[..., 2 * A_QK_W:].reshape(bn, s, A_HEADS, A_DV)
    gates = gates.astype(f32).reshape(bn, s, 2, 2, A_HEADS)
    g = -jnp.exp(a_log.astype(f32)) * jax.nn.softplus(gates[:, :, 0] + dt_bias.astype(f32))
    beta = jax.nn.sigmoid(gates[:, :, 1])
    qh, kh, vh = (jnp.moveaxis(t, 1, 2) for t in (q, k, v))
    gh = jnp.moveaxis(g, 1, -1)
    bh = jnp.moveaxis(beta, 1, -1)
    o_fwd = gated_delta_chunked(qh, kh, vh, gh[:, 0], bh[:, 0])
    o_bwd = jnp.flip(gated_delta_chunked(jnp.flip(qh, 2), jnp.flip(kh, 2), jnp.flip(vh, 2),
                                         jnp.flip(gh[:, 1], -1), jnp.flip(bh[:, 1], -1)), 2)
    o = jnp.moveaxis(o_fwd + o_bwd, 1, 2)
    o = rmsnorm(o, out_norm) * jax.nn.silu(z.astype(f32).reshape(bn, s, A_HEADS, A_DV))
    return o.reshape(bn, s, A_V_W).astype(z.dtype)


def neighbourhood_attention(q, k, v, rpb):
    bn, s, h, d = q.shape
    rows = s // GRID_W
    wr = min(NA_ROWS, rows)
    r = np.arange(rows)
    row_start = np.clip(r - wr // 2, 0, rows - wr)
    row_idx = row_start[:, None] + np.arange(wr)[None, :]
    c = np.arange(GRID_W)
    col_start = np.clip(c - NA_COLS // 2, 0, GRID_W - NA_COLS)
    col_in = (c[None, :] >= col_start[:, None]) & (c[None, :] < col_start[:, None] + NA_COLS)
    dr = row_idx - r[:, None] + NA_ROWS - 1
    dc = np.clip(c[None, :] - c[:, None] + NA_COLS - 1, 0, 2 * NA_COLS - 2)
    bias = rpb[:, dr[:, None, :, None], dc[None, :, None, :]]
    bias = jnp.where(col_in[None, None, :, None, :], bias, -jnp.inf)
    qg = q.reshape(bn, rows, GRID_W, h, d)
    kg = k.reshape(bn, rows, GRID_W, h, d)[:, row_idx]
    vg = v.reshape(bn, rows, GRID_W, h, d)[:, row_idx]
    sc = jnp.einsum('brchd,brikhd->bhrcik', qg, kg, preferred_element_type=jnp.float32) * (d ** -0.5)
    p = jax.nn.softmax(sc + bias[None].astype(jnp.float32), axis=(-2, -1))
    o = jnp.einsum('bhrcik,brikhd->brchd', p.astype(v.dtype), vg)
    return o.reshape(bn, s, h * d)


def partial_rope(x, cos, sin):
    half = ROPE_DIMS // 2
    xf = x[..., :ROPE_DIMS].astype(jnp.float32)
    x1, x2 = xf[..., :half], xf[..., half:]
    cb = cos[None, :, None, None, :]
    sb = sin[None, :, None, None, :]
    rot = jnp.concatenate([x1 * cb - x2 * sb, x2 * cb + x1 * sb], axis=-1)
    return jnp.concatenate([rot.astype(x.dtype), x[..., ROPE_DIMS:]], axis=-1)


def rope_tables(s):
    inv = ROPE_THETA ** (-jnp.arange(0, ROPE_DIMS, 2, dtype=jnp.float32) / ROPE_DIMS)
    ang = jnp.arange(s, dtype=jnp.float32)[:, None] * inv[None, :]
    return jnp.cos(ang), jnp.sin(ang)


def mixer_ab(hn, w_in, conv_w, a_log, dt_bias, out_norm, rpb, w_out):
    bn, s, _ = hn.shape
    proj = hn @ w_in
    o1 = A_QKV_W
    o2 = o1 + A_V_W
    o3 = o2 + 4 * A_HEADS
    o_a = deltanet_mixer(proj[..., :o1], proj[..., o1:o2], proj[..., o2:o3],
                         conv_w, a_log, dt_bias, out_norm)
    qkv_b = proj[..., o3:].reshape(bn, s, 3, B_HEADS, B_DH)
    o_b = neighbourhood_attention(qkv_b[:, :, 0], qkv_b[:, :, 1], qkv_b[:, :, 2], rpb)
    return jnp.concatenate([o_a, o_b], axis=-1) @ w_out


def mixer_c(hn, w_in, lam_params, subln_w, w_out, cos, sin, layer):
    bn, s, _ = hn.shape
    proj = hn @ w_in
    q = proj[..., :C_QK_W].reshape(bn, s, C_HEADS, 2, C_DH)
    k = proj[..., C_QK_W:2 * C_QK_W].reshape(bn, s, C_HEADS, 2, C_DH)
    v = proj[..., 2 * C_QK_W:].reshape(bn, s, C_HEADS, 2 * C_DH)
    q = partial_rope(q, cos, sin)
    k = partial_rope(k, cos, sin)
    lambda_init = 0.8 - 0.6 * math.exp(-0.3 * layer)
    lp = lam_params.astype(jnp.float32)
    lam = jnp.exp(jnp.sum(lp[0] * lp[1])) - jnp.exp(jnp.sum(lp[2] * lp[3])) + lambda_init
    scale = C_DH ** -0.5
    nblk = s // Q_BLOCK
    qb = jnp.moveaxis(q.reshape(bn, nblk, Q_BLOCK, C_HEADS, 2, C_DH), 1, 0)

    def block(q_blk):
        sc = jnp.einsum('bqhjd,bkhjd->bhjqk', q_blk, k, preferred_element_type=jnp.float32) * scale
        pm = jax.nn.softmax(sc, axis=-1)
        a = pm[:, :, 0] - lam * pm[:, :, 1]
        return jnp.einsum('bhqk,bkhe->bqhe', a.astype(v.dtype), v)

    o = lax.map(block, qb)
    o = jnp.moveaxis(o, 0, 1).reshape(bn, s, C_HEADS, 2 * C_DH)
    o = rmsnorm(o, subln_w) * (1.0 - lambda_init)
    return o.reshape(bn, s, C_V_W) @ w_out


def conv_ffn(x, w_in, conv_w, conv_b, w_out):
    hid = dwconv_centred(x @ w_in, conv_w) + conv_b.astype(x.dtype)
    gate, up = hid[..., :D_FF], hid[..., D_FF:]
    return (jax.nn.gelu(gate, approximate=True) * up) @ w_out


def trunk(x, p, ab_w_in, ab_conv_w, ab_a_log, ab_dt_bias, ab_out_norm, ab_rpb, ab_w_out,
          c_w_in, c_lambda, c_subln, c_w_out, norms, ffn_w_in, ffn_conv_w, ffn_conv_b,
          ffn_w_out, ple_w_proj, ple_w_gate):
    cos, sin = rope_tables(x.shape[1])
    h = x
    for layer in range(DEPTH):
        j = layer // 2
        hn = rmsnorm(h, norms[layer, 0])
        if layer % 2 == 0:
            mix = mixer_ab(hn, ab_w_in[j], ab_conv_w[j], ab_a_log[j], ab_dt_bias[j],
                           ab_out_norm[j], ab_rpb[j], ab_w_out[j])
        else:
            mix = mixer_c(hn, c_w_in[j], c_lambda[j], c_subln[j], c_w_out[j], cos, sin, layer)
        h = h + rmsnorm(mix, norms[layer, 1])
        f = conv_ffn(rmsnorm(h, norms[layer, 2]), ffn_w_in[layer], ffn_conv_w[layer],
                     ffn_conv_b[layer], ffn_w_out[layer])
        h = h + rmsnorm(f, norms[layer, 3])
        h = h + jax.nn.sigmoid(h @ ple_w_gate[layer]) * (p[layer] @ ple_w_proj[layer])
    return h


def setup_inputs(seed: int = 0) -> dict:
    key = jax.random.key(seed)
    ks = jax.random.split(key, 24)
    f32 = jnp.float32

    def nrm(k, shape, scale):
        return jax.random.normal(k, shape, f32) * scale

    dt = jnp.exp(jax.random.uniform(ks[7], (N_AB, 2, A_HEADS), f32, math.log(1e-3), math.log(1e-1)))
    return {
        'x_prompt': nrm(ks[0], (BATCH, SEQ, D_MODEL), 1.0),
        'x_sample': nrm(ks[1], (DEC_BATCH, DEC_SEQ, D_MODEL), 1.0),
        'p_prompt': nrm(ks[2], (DEPTH, BATCH, SEQ, PLE_DIM), 1.0),
        'p_sample': nrm(ks[3], (DEPTH, DEC_BATCH, DEC_SEQ, PLE_DIM), 1.0),
        'ab_w_in': nrm(ks[4], (N_AB, D_MODEL, AB_IN), D_MODEL ** -0.5),
        'ab_conv_w': nrm(ks[5], (N_AB, A_CONV_W, A_QKV_W), A_CONV_W ** -0.5),
        'ab_a_log': jnp.log(jax.random.uniform(ks[6], (N_AB, 2, A_HEADS), f32, 1.0, 16.0)),
        'ab_dt_bias': dt + jnp.log(-jnp.expm1(-dt)),
        'ab_out_norm': 1.0 + nrm(ks[8], (N_AB, A_DV), 0.05),
        'ab_rpb': nrm(ks[9], (N_AB, B_HEADS, 2 * NA_ROWS - 1, 2 * NA_COLS - 1), 0.1),
        'ab_w_out': nrm(ks[10], (N_AB, AB_MIX, D_MODEL), AB_MIX ** -0.5),
        'c_w_in': nrm(ks[11], (N_C, D_MODEL, C_IN), D_MODEL ** -0.5),
        'c_lambda': nrm(ks[12], (N_C, 4, C_DH), 0.1),
        'c_subln': 1.0 + nrm(ks[13], (N_C, 2 * C_DH), 0.05),
        'c_w_out': nrm(ks[14], (N_C, C_V_W, D_MODEL), C_V_W ** -0.5),
        'norms': 1.0 + nrm(ks[15], (DEPTH, 4, D_MODEL), 0.05),
        'ffn_w_in': nrm(ks[16], (DEPTH, D_MODEL, 2 * D_FF), D_MODEL ** -0.5),
        'ffn_conv_w': nrm(ks[17], (DEPTH, FFN_CONV_W, 2 * D_FF), FFN_CONV_W ** -0.5),
        'ffn_conv_b': nrm(ks[18], (DEPTH, 2 * D_FF), 0.01),
        'ffn_w_out': nrm(ks[19], (DEPTH, D_FF, D_MODEL), D_FF ** -0.5),
        'ple_w_proj': nrm(ks[20], (DEPTH, PLE_DIM, D_MODEL), PLE_DIM ** -0.5),
        'ple_w_gate': nrm(ks[21], (DEPTH, D_MODEL, D_MODEL), D_MODEL ** -0.5),
    }


def reference(x_prompt, x_sample, p_prompt, p_sample, ab_w_in, ab_conv_w, ab_a_log, ab_dt_bias,
              ab_out_norm, ab_rpb, ab_w_out, c_w_in, c_lambda, c_subln, c_w_out, norms,
              ffn_w_in, ffn_conv_w, ffn_conv_b, ffn_w_out, ple_w_proj, ple_w_gate):
    y_prompt = trunk(x_prompt, p_prompt, ab_w_in, ab_conv_w, ab_a_log, ab_dt_bias, ab_out_norm,
                     ab_rpb, ab_w_out, c_w_in, c_lambda, c_subln, c_w_out, norms, ffn_w_in,
                     ffn_conv_w, ffn_conv_b, ffn_w_out, ple_w_proj, ple_w_gate)
    y_sample = trunk(x_sample, p_sample, ab_w_in, ab_conv_w, ab_a_log, ab_dt_bias, ab_out_norm,
                     ab_rpb, ab_w_out, c_w_in, c_lambda, c_subln, c_w_out, norms, ffn_w_in,
                     ffn_conv_w, ffn_conv_b, ffn_w_out, ple_w_proj, ple_w_gate)
    return (y_prompt, y_sample)
```

```python
import functools
import math

import jax
import jax.numpy as jnp
import numpy as np
from jax import lax
from jax.experimental import pallas as pl
from jax.experimental.pallas import tpu as pltpu

F32 = jnp.float32
BF16 = jnp.bfloat16

D_MODEL = 2048
PLE_DIM = 256
GRID_W = 64
RMS_EPS = 1e-6
HEADS = 8
HEAD_DIM = 128
A_CONV_W = 5
NA_ROWS = 8
NA_COLS = 16
ROPE_THETA = 500000.0
ROPE_DIMS = HEAD_DIM // 4
D_FF = 8192
FFN_CONV_W = 3
A_QKV_W = 3 * HEADS * HEAD_DIM
A_V_W = HEADS * HEAD_DIM
B_W = HEADS * HEAD_DIM
C_QK_W = HEADS * 2 * HEAD_DIM

LANES = 128
SUBLANES_BF16 = 16
VMEM_LIMIT = 56 * 1024 * 1024
NEG_BIG = -1e30

ROW_TILE = 512
COL_TILE = 512
FF_TILE = 512
DN_BLOCK = 256
DN_BASE = 64
ATT_Q_TILE = 256
NA_Q_ROWS = 8
NA_K_ROWS = 16


def _params(*sem):
    return pltpu.CompilerParams(dimension_semantics=sem, vmem_limit_bytes=VMEM_LIMIT)


def _rms(x, w):
    return x * lax.rsqrt(jnp.mean(x * x, axis=-1, keepdims=True) + RMS_EPS) * w


def _dot(a, b):
    return jnp.dot(a, b, preferred_element_type=F32)


def _dot_nt(a, b):
    return lax.dot_general(a, b, (((1,), (1,)), ((), ())), preferred_element_type=F32)


def _dot_tn(a, b):
    return lax.dot_general(a, b, (((0,), (0,)), ((), ())), preferred_element_type=F32)


def _sigmoid(x):
    return 1.0 / (1.0 + jnp.exp(-x))


def _norm_matmul_kernel(*refs, rope_tiles):
    if rope_tiles:
        x_ref, nw_ref, w_ref, rc_ref, rs1_ref, rs2_ref, o_ref, xn_ref = refs
    else:
        x_ref, nw_ref, w_ref, o_ref, xn_ref = refs
    j = pl.program_id(1)

    @pl.when(j == 0)
    def _():
        xn_ref[...] = _rms(x_ref[...], nw_ref[...]).astype(BF16)

    y = _dot(xn_ref[...], w_ref[...])
    if not rope_tiles:
        o_ref[...] = y.astype(o_ref.dtype)
        return

    @pl.when(j < rope_tiles)
    def _():
        n = y.shape[1]
        reps = n // LANES
        c = jnp.concatenate([rc_ref[...]] * reps, axis=1)
        s1 = jnp.concatenate([rs1_ref[...]] * reps, axis=1)
        s2 = jnp.concatenate([rs2_ref[...]] * reps, axis=1)
        half = ROPE_DIMS // 2
        r = y * c + pltpu.roll(y, half, 1) * s1 + pltpu.roll(y, n - half, 1) * s2
        o_ref[...] = r.astype(o_ref.dtype)

    @pl.when(j >= rope_tiles)
    def _():
        o_ref[...] = y.astype(o_ref.dtype)


def norm_matmul(x, nw, w, out_dtype, rope=None, rope_tiles=0, seq_len=None, tm=ROW_TILE, tn=COL_TILE):
    t, k = x.shape
    n = w.shape[1]
    tn = min(tn, n)
    assert t % tm == 0 and n % tn == 0
    in_specs = [
        pl.BlockSpec((tm, k), lambda i, j: (i, 0)),
        pl.BlockSpec((1, k), lambda i, j: (0, 0)),
        pl.BlockSpec((k, tn), lambda i, j: (0, j)),
    ]
    args = [x, nw.reshape(1, k), w]
    if rope_tiles:
        per_seq = seq_len // tm
        for tab in rope:
            in_specs.append(pl.BlockSpec((tm, LANES), lambda i, j: (i % per_seq, 0)))
            args.append(tab)
    return pl.pallas_call(
        functools.partial(_norm_matmul_kernel, rope_tiles=rope_tiles),
        grid=(t // tm, n // tn),
        in_specs=in_specs,
        out_specs=pl.BlockSpec((tm, tn), lambda i, j: (i, j)),
        out_shape=jax.ShapeDtypeStruct((t, n), out_dtype),
        scratch_shapes=[pltpu.VMEM((tm, k), BF16)],
        compiler_params=_params("parallel", "arbitrary"),
        name="norm_matmul",
    )(*args)


def _proj_norm_res_kernel(*refs, n_in):
    a_refs = refs[:n_in]
    w_refs = refs[n_in:2 * n_in]
    nw_ref, h_ref, o_ref = refs[2 * n_in:]
    y = _dot(a_refs[0][...], w_refs[0][...])
    for a_ref, w_ref in zip(a_refs[1:], w_refs[1:]):
        y = y + _dot(a_ref[...], w_ref[...])
    o_ref[...] = h_ref[...] + _rms(y, nw_ref[...])


def proj_norm_res(acts, ws, nw, h, tm=ROW_TILE):
    t, n = h.shape
    n_in = len(acts)
    in_specs = [pl.BlockSpec((tm, a.shape[1]), lambda i: (i, 0)) for a in acts]
    in_specs += [pl.BlockSpec(w.shape, lambda i: (0, 0)) for w in ws]
    in_specs += [pl.BlockSpec((1, n), lambda i: (0, 0)), pl.BlockSpec((tm, n), lambda i: (i, 0))]
    return pl.pallas_call(
        functools.partial(_proj_norm_res_kernel, n_in=n_in),
        grid=(t // tm,),
        in_specs=in_specs,
        out_specs=pl.BlockSpec((tm, n), lambda i: (i, 0)),
        out_shape=jax.ShapeDtypeStruct((t, n), F32),
        compiler_params=_params("parallel"),
        name="proj_norm_res",
    )(*acts, *ws, nw.reshape(1, n), h)


def _ple_kernel(h_ref, p_ref, wg_ref, wp_ref, o_ref):
    h = h_ref[...]
    gate = _sigmoid(_dot(h.astype(BF16), wg_ref[...]))
    o_ref[...] = h + gate * _dot(p_ref[...].astype(BF16), wp_ref[...])


def ple(h, p, wg, wp, tm=ROW_TILE):
    t, n = h.shape
    return pl.pallas_call(
        _ple_kernel,
        grid=(t // tm,),
        in_specs=[
            pl.BlockSpec((tm, n), lambda i: (i, 0)),
            pl.BlockSpec((tm, p.shape[1]), lambda i: (i, 0)),
            pl.BlockSpec(wg.shape, lambda i: (0, 0)),
            pl.BlockSpec(wp.shape, lambda i: (0, 0)),
        ],
        out_specs=pl.BlockSpec((tm, n), lambda i: (i, 0)),
        out_shape=jax.ShapeDtypeStruct((t, n), F32),
        compiler_params=_params("parallel"),
        name="ple",
    )(h, p, wg, wp)


HALO = SUBLANES_BF16


def _ffn_kernel(x_ref, xp_ref, xx_ref, nw_ref, wg_ref, wu_ref, cwg_ref, cwu_ref, cbg_ref, cbu_ref,
                wo_ref, nw2_ref, o_ref, xn_ref, hg_ref, hu_ref, acc_ref, *, tm, per_seq):
    i = pl.program_id(0)
    j = pl.program_id(1)
    k = x_ref.shape[1]

    @pl.when(j == 0)
    def _():
        nw = nw_ref[...]
        pos = i % per_seq
        prev = jnp.where(pos == 0, 0.0, _rms(xp_ref[...], nw))
        nxt = jnp.where(pos == per_seq - 1, 0.0, _rms(xx_ref[...], nw))
        pad = jnp.zeros((HALO - 8, k), F32)
        xn_ref[0:HALO, :] = jnp.concatenate([pad, prev], axis=0).astype(BF16)
        xn_ref[HALO:HALO + tm, :] = _rms(x_ref[...], nw).astype(BF16)
        xn_ref[HALO + tm:, :] = jnp.concatenate([nxt, pad], axis=0).astype(BF16)
        acc_ref[...] = jnp.zeros_like(acc_ref)

    xe = xn_ref[...]
    hg_ref[...] = _dot(xe, wg_ref[...])
    hu_ref[...] = _dot(xe, wu_ref[...])

    def conv(h_ref, cw_ref, cb_ref):
        cw = cw_ref[...]
        return (h_ref[pl.ds(HALO - 1, tm), :] * cw[0:1] + h_ref[pl.ds(HALO, tm), :] * cw[1:2]
                + h_ref[pl.ds(HALO + 1, tm), :] * cw[2:3] + cb_ref[...])

    g = conv(hg_ref, cwg_ref, cbg_ref)
    u = conv(hu_ref, cwu_ref, cbu_ref)
    c0 = math.sqrt(2.0 / math.pi)
    gelu = 0.5 * g * (1.0 + jnp.tanh(c0 * (g + 0.044715 * (g * g * g))))
    acc_ref[...] += _dot((gelu * u).astype(BF16), wo_ref[...])

    @pl.when(j == pl.num_programs(1) - 1)
    def _():
        o_ref[...] = x_ref[...] + _rms(acc_ref[...], nw2_ref[...])


def conv_ffn(h, nw_in, w_in, conv_w, conv_b, w_out, nw_out, seq_len, tm=ROW_TILE, tf=FF_TILE):
    t, d = h.shape
    f = w_out.shape[0]
    nf = f // tf
    per_seq = seq_len // tm
    r8 = tm // 8
    last8 = t // 8 - 1
    cb = conv_b.reshape(1, 2 * f)
    return pl.pallas_call(
        functools.partial(_ffn_kernel, tm=tm, per_seq=per_seq),
        grid=(t // tm, nf),
        in_specs=[
            pl.BlockSpec((tm, d), lambda i, j: (i, 0)),
            pl.BlockSpec((8, d), lambda i, j: (jnp.maximum(i * r8 - 1, 0), 0)),
            pl.BlockSpec((8, d), lambda i, j: (jnp.minimum((i + 1) * r8, last8), 0)),
            pl.BlockSpec((1, d), lambda i, j: (0, 0)),
            pl.BlockSpec((d, tf), lambda i, j: (0, j)),
            pl.BlockSpec((d, tf), lambda i, j: (0, j + nf)),
            pl.BlockSpec((FFN_CONV_W, tf), lambda i, j: (0, j)),
            pl.BlockSpec((FFN_CONV_W, tf), lambda i, j: (0, j + nf)),
            pl.BlockSpec((1, tf), lambda i, j: (0, j)),
            pl.BlockSpec((1, tf), lambda i, j: (0, j + nf)),
            pl.BlockSpec((tf, d), lambda i, j: (j, 0)),
            pl.BlockSpec((1, d), lambda i, j: (0, 0)),
        ],
        out_specs=pl.BlockSpec((tm, d), lambda i, j: (i, 0)),
        out_shape=jax.ShapeDtypeStruct((t, d), F32),
        scratch_shapes=[
            pltpu.VMEM((tm + 2 * HALO, d), BF16),
            pltpu.VMEM((tm + 2 * HALO, tf), F32),
            pltpu.VMEM((tm + 2 * HALO, tf), F32),
            pltpu.VMEM((tm, d), F32),
        ],
        compiler_params=_params("parallel", "arbitrary"),
        name="conv_ffn",
    )(h, h, h, nw_in.reshape(1, d), w_in, w_in, conv_w, conv_w, cb, cb, w_out, nw_out.reshape(1, d))


def _diff_attn_kernel(lp_ref, sub_ref, q_ref, k_ref, v_ref, o_ref, *, lambda_init):
    lp = lp_ref[...]
    lam = (jnp.exp(jnp.sum(lp[0:1] * lp[1:2], axis=-1, keepdims=True))
           - jnp.exp(jnp.sum(lp[2:3] * lp[3:4], axis=-1, keepdims=True)) + lambda_init)
    scale = HEAD_DIM ** -0.5
    q = (q_ref[0].astype(F32) * scale).astype(BF16)

    def probs(lo):
        s = _dot_nt(q[:, lo:lo + HEAD_DIM], k_ref[0, :, lo:lo + HEAD_DIM])
        e = jnp.exp(s - jnp.max(s, axis=-1, keepdims=True))
        return e, 1.0 / jnp.sum(e, axis=-1, keepdims=True)

    e1, r1 = probs(0)
    e2, r2 = probs(HEAD_DIM)
    a = e1 * r1 - e2 * (lam * r2)
    o = _dot(a.astype(BF16), v_ref[0])
    o_ref[0] = (_rms(o, sub_ref[...]) * (1.0 - lambda_init)).astype(o_ref.dtype)


def diff_attention(proj, lam_params, subln, layer, tq=ATT_Q_TILE):
    b, s, _ = proj.shape
    w = 2 * HEAD_DIM
    lambda_init = 0.8 - 0.6 * math.exp(-0.3 * layer)
    return pl.pallas_call(
        functools.partial(_diff_attn_kernel, lambda_init=lambda_init),
        grid=(b, HEADS, s // tq),
        in_specs=[
            pl.BlockSpec((4, HEAD_DIM), lambda bi, h, qi: (0, 0)),
            pl.BlockSpec((1, w), lambda bi, h, qi: (0, 0)),
            pl.BlockSpec((1, tq, w), lambda bi, h, qi: (bi, qi, h)),
            pl.BlockSpec((1, s, w), lambda bi, h, qi: (bi, 0, HEADS + h)),
            pl.BlockSpec((1, s, w), lambda bi, h, qi: (bi, 0, 2 * HEADS + h)),
        ],
        out_specs=pl.BlockSpec((1, tq, w), lambda bi, h, qi: (bi, qi, h)),
        out_shape=jax.ShapeDtypeStruct((b, s, C_QK_W), BF16),
        compiler_params=_params("parallel", "parallel", "arbitrary"),
        name="diff_attention",
    )(lam_params, subln.reshape(1, w), proj, proj, proj)


def _na_bias_index():
    rows = GRID_W
    nq, nk = NA_Q_ROWS * GRID_W, NA_K_ROWS * GRID_W
    idx_r = np.zeros((3, nq, nk), np.int32)
    idx_c = np.zeros((3, nq, nk), np.int32)
    valid = np.zeros((3, nq, nk), bool)
    qi = np.arange(nq)
    ki = np.arange(nk)
    qc, kc = qi % GRID_W, ki % GRID_W
    col_start = np.clip(qc - NA_COLS // 2, 0, GRID_W - NA_COLS)
    col_ok = (kc[None, :] >= col_start[:, None]) & (kc[None, :] < col_start[:, None] + NA_COLS)
    dc = np.clip(kc[None, :] - qc[:, None] + NA_COLS - 1, 0, 2 * NA_COLS - 2)
    for case, rb in enumerate((0, 1, rows // NA_Q_ROWS - 1)):
        base = int(np.clip(rb * NA_Q_ROWS - (NA_K_ROWS - NA_Q_ROWS) // 2, 0, rows - NA_K_ROWS))
        r = rb * NA_Q_ROWS + qi // GRID_W
        kr = base + ki // GRID_W
        row_start = np.clip(r - NA_ROWS // 2, 0, rows - NA_ROWS)
        row_ok = (kr[None, :] >= row_start[:, None]) & (kr[None, :] < row_start[:, None] + NA_ROWS)
        dr = kr[None, :] - r[:, None] + NA_ROWS - 1
        valid[case] = row_ok & col_ok
        idx_r[case] = np.clip(dr, 0, 2 * NA_ROWS - 2)
        idx_c[case] = dc
    return idx_r, idx_c, valid


def _na_kernel(q_ref, k_ref, v_ref, b_ref, o_ref):
    rb = pl.program_id(2)
    nk = NA_K_ROWS * GRID_W
    lo = (NA_K_ROWS - NA_Q_ROWS) // 2 * GRID_W
    start = jnp.clip(rb * (NA_Q_ROWS * GRID_W) - lo, 0, k_ref.shape[1] - nk)
    start = pl.multiple_of(start, lo)
    q = (q_ref[0].astype(F32) * (HEAD_DIM ** -0.5)).astype(BF16)
    s = _dot_nt(q, k_ref[0, pl.ds(start, nk), :]) + b_ref[0, 0]
    e = jnp.exp(s - jnp.max(s, axis=-1, keepdims=True))
    p = e * (1.0 / jnp.sum(e, axis=-1, keepdims=True))
    o_ref[0] = _dot(p.astype(BF16), v_ref[0, pl.ds(start, nk), :]).astype(o_ref.dtype)


def neighbourhood_attention(proj, col0, bias):
    b, s, _ = proj.shape
    nq = NA_Q_ROWS * GRID_W
    nblk = s // nq
    c0 = col0 // HEAD_DIM

    def bias_map(bi, h, rb):
        return (h, jnp.where(rb == 0, 0, jnp.where(rb == nblk - 1, 2, 1)), 0, 0)

    return pl.pallas_call(
        _na_kernel,
        grid=(b, HEADS, nblk),
        in_specs=[
            pl.BlockSpec((1, nq, HEAD_DIM), lambda bi, h, rb: (bi, rb, c0 + h)),
            pl.BlockSpec((1, s, HEAD_DIM), lambda bi, h, rb: (bi, 0, c0 + HEADS + h)),
            pl.BlockSpec((1, s, HEAD_DIM), lambda bi, h, rb: (bi, 0, c0 + 2 * HEADS + h)),
            pl.BlockSpec((1, 1) + bias.shape[2:], bias_map),
        ],
        out_specs=pl.BlockSpec((1, nq, HEAD_DIM), lambda bi, h, rb: (bi, rb, h)),
        out_shape=jax.ShapeDtypeStruct((b, s, B_W), BF16),
        compiler_params=_params("parallel", "parallel", "arbitrary"),
        name="neighbourhood_attention",
    )(proj, proj, proj, bias)


def _split3(x):
    hi = x.astype(BF16)
    r1 = x - hi.astype(F32)
    mid = r1.astype(BF16)
    lo = (r1 - mid.astype(F32)).astype(BF16)
    return hi, mid, lo


def _dn_prep_kernel(x_ref, xp_ref, xx_ref, cw_ref, gt_ref, alog_ref, dtb_ref,
                    q_ref, k_ref, v_ref, gc_ref, beta_ref, xs_ref):
    i = pl.program_id(1)
    tb = x_ref.shape[1]
    pad = A_CONV_W // 2
    xs_ref[0:HALO, :] = jnp.where(i == 0, 0.0, xp_ref[0].astype(F32))
    xs_ref[HALO:HALO + tb, :] = x_ref[0].astype(F32)
    xs_ref[HALO + tb:, :] = jnp.where(i == pl.num_programs(1) - 1, 0.0, xx_ref[0].astype(F32))
    cw = cw_ref[...]
    y = xs_ref[pl.ds(HALO - pad, tb), :] * cw[0:1]
    for t in range(1, A_CONV_W):
        y = y + xs_ref[pl.ds(HALO - pad + t, tb), :] * cw[t:t + 1]
    y = y * _sigmoid(y)
    for h in range(HEADS):
        qh = y[:, h * HEAD_DIM:(h + 1) * HEAD_DIM]
        kh = y[:, (HEADS + h) * HEAD_DIM:(HEADS + h + 1) * HEAD_DIM]
        qn = qh * (lax.rsqrt(jnp.sum(qh * qh, axis=-1, keepdims=True) + 1e-6) * (HEAD_DIM ** -0.5))
        kn = kh * lax.rsqrt(jnp.sum(kh * kh, axis=-1, keepdims=True) + 1e-6)
        q_ref[0, h] = qn.astype(q_ref.dtype)
        k_ref[0, h] = kn.astype(k_ref.dtype)
        v_ref[0, h] = y[:, (2 * HEADS + h) * HEAD_DIM:(2 * HEADS + h + 1) * HEAD_DIM].astype(v_ref.dtype)

    nd = 2 * HEADS
    gates = gt_ref[0]
    z = gates[:, 0:nd] + dtb_ref[...]
    softplus = jnp.maximum(z, 0.0) + jnp.log(1.0 + jnp.exp(-jnp.abs(z)))
    g = -jnp.exp(alog_ref[...]) * softplus
    beta_ref[0] = _sigmoid(gates[:, nd:2 * nd])
    r = lax.broadcasted_iota(jnp.int32, (tb, tb), 0)
    c = lax.broadcasted_iota(jnp.int32, (tb, tb), 1)
    lower = jnp.where(r >= c, 1.0, 0.0).astype(BF16)
    upper = jnp.where(r <= c, 1.0, 0.0).astype(BF16)
    parts = _split3(g)
    fwd = _dot(lower, parts[0]) + _dot(lower, parts[1]) + _dot(lower, parts[2])
    bwd = _dot(upper, parts[0]) + _dot(upper, parts[1]) + _dot(upper, parts[2])
    col = lax.broadcasted_iota(jnp.int32, (tb, nd), 1)
    gc_ref[0] = jnp.where(col < HEADS, fwd, bwd)


def deltanet_prep(proj, gates, conv_w, a_log, dt_bias, tb=DN_BLOCK):
    b, s, _ = proj.shape
    nb = s // tb
    rh = tb // HALO
    nd = 2 * HEADS
    hs = jax.ShapeDtypeStruct((b, HEADS, s, HEAD_DIM), BF16)
    head_spec = pl.BlockSpec((1, HEADS, tb, HEAD_DIM), lambda bi, i: (bi, 0, i, 0))
    vec_spec = pl.BlockSpec((1, tb, nd), lambda bi, i: (bi, i, 0))
    return pl.pallas_call(
        _dn_prep_kernel,
        grid=(b, nb),
        in_specs=[
            pl.BlockSpec((1, tb, A_QKV_W), lambda bi, i: (bi, i, 0)),
            pl.BlockSpec((1, HALO, A_QKV_W), lambda bi, i: (bi, jnp.maximum(i * rh - 1, 0), 0)),
            pl.BlockSpec((1, HALO, A_QKV_W), lambda bi, i: (bi, jnp.minimum((i + 1) * rh, s // HALO - 1), 0)),
            pl.BlockSpec((A_CONV_W, A_QKV_W), lambda bi, i: (0, 0)),
            pl.BlockSpec((1, tb, LANES), lambda bi, i: (bi, i, 0)),
            pl.BlockSpec((1, nd), lambda bi, i: (0, 0)),
            pl.BlockSpec((1, nd), lambda bi, i: (0, 0)),
        ],
        out_specs=[head_spec, head_spec, head_spec, vec_spec, vec_spec],
        out_shape=[hs, hs, hs, jax.ShapeDtypeStruct((b, s, nd), F32), jax.ShapeDtypeStruct((b, s, nd), F32)],
        scratch_shapes=[pltpu.VMEM((tb + 2 * HALO, A_QKV_W), F32)],
        compiler_params=_params("parallel", "arbitrary"),
        name="deltanet_prep",
    )(proj, proj, proj, conv_w, gates, a_log.reshape(1, nd), dt_bias.reshape(1, nd))


def _split2(x):
    hi = x.astype(BF16)
    return hi, (x - hi.astype(F32)).astype(BF16)


def _dot3(a, b):
    return _dot(a[0], b[0]) + (_dot(a[0], b[1]) + _dot(a[1], b[0]))


def _unit_lower_inverse(m, same_block):
    n = m.shape[0]
    r = lax.broadcasted_iota(jnp.int32, (n, n), 0)
    c = lax.broadcasted_iota(jnp.int32, (n, n), 1)
    eye = jnp.where(r == c, 1.0, 0.0)
    md = jnp.where(same_block(DN_BASE), m, 0.0)
    pw = _split2(md)
    t = eye + md
    size = 2
    while size < DN_BASE:
        pw = _split2(_dot3(pw, pw))
        t = t + _dot3(pw, _split2(t))
        size *= 2
    blk = DN_BASE
    while blk < n:
        off = jnp.where(same_block(2 * blk), m - jnp.where(same_block(blk), m, 0.0), 0.0)
        ts = _split2(t)
        t = t + _dot3(ts, _split2(_dot3(_split2(off), ts)))
        blk *= 2
    return t


def _dn_direction(q, k, v, gcol, grow, beta, state_ref, forward):
    n = q.shape[0]
    r = lax.broadcasted_iota(jnp.int32, (n, n), 0)
    c = lax.broadcasted_iota(jnp.int32, (n, n), 1)
    d = (r - c) if forward else (c - r)

    def same_block(size):
        return (r // size) == (c // size)

    kf = k.astype(F32)
    kb = kf * beta
    decay = jnp.exp(jnp.where(d >= 0, gcol - grow, NEG_BIG))
    m = jnp.where(d > 0, -(_dot_nt(kb.astype(BF16), k) * decay), 0.0)
    intra = _dot_nt(q, k) * decay
    eg = jnp.exp(gcol)
    rhs = jnp.concatenate([v.astype(F32) * beta, kb * eg], axis=1)
    sol = _dot(_unit_lower_inverse(m, same_block).astype(BF16), rhs.astype(BF16))
    u, w = sol[:, :HEAD_DIM], sol[:, HEAD_DIM:]

    state = state_ref[...]
    sb = state.astype(BF16)
    v_new = u - _dot(w.astype(BF16), sb)
    o = _dot((q.astype(F32) * eg).astype(BF16), sb) + _dot(intra.astype(BF16), v_new.astype(BF16))
    g_last = grow[:, n - 1:n] if forward else grow[:, 0:1]
    k_dec = kf * jnp.exp(g_last - gcol)
    state_ref[...] = state * jnp.exp(g_last) + _dot_tn(k_dec.astype(BF16), v_new.astype(BF16))
    return o


def _dn_kernel(qf_ref, kf_ref, vf_ref, qb_ref, kb_ref, vb_ref, gcf_ref, gcb_ref, btf_ref, btb_ref,
               grf_ref, grb_ref, of_ref, ob_ref, state_ref):
    h = pl.program_id(1)

    @pl.when(pl.program_id(2) == 0)
    def _():
        state_ref[...] = jnp.zeros_like(state_ref)

    lane = lax.broadcasted_iota(jnp.int32, gcf_ref.shape[1:], 1)

    def pick(ref, idx):
        return jnp.sum(jnp.where(lane == idx, ref[0], 0.0), axis=-1, keepdims=True)

    of_ref[0, 0] = _dn_direction(qf_ref[0, 0], kf_ref[0, 0], vf_ref[0, 0], pick(gcf_ref, h),
                                 grf_ref[0, pl.ds(h, 1), :], pick(btf_ref, h), state_ref.at[0], True)
    ob_ref[0, 0] = _dn_direction(qb_ref[0, 0], kb_ref[0, 0], vb_ref[0, 0], pick(gcb_ref, HEADS + h),
                                 grb_ref[0, pl.ds(HEADS + h, 1), :], pick(btb_ref, HEADS + h),
                                 state_ref.at[1], False)


def deltanet_scan(q, k, v, gc, beta, gc_rows, tb=DN_BLOCK):
    b, _, s, _ = q.shape
    nb = s // tb
    nd = 2 * HEADS
    fwd = pl.BlockSpec((1, 1, tb, HEAD_DIM), lambda bi, h, c: (bi, h, c, 0))
    bwd = pl.BlockSpec((1, 1, tb, HEAD_DIM), lambda bi, h, c: (bi, h, nb - 1 - c, 0))
    vec_f = pl.BlockSpec((1, tb, nd), lambda bi, h, c: (bi, c, 0))
    vec_b = pl.BlockSpec((1, tb, nd), lambda bi, h, c: (bi, nb - 1 - c, 0))
    row_f = pl.BlockSpec((1, nd, tb), lambda bi, h, c: (bi, 0, c))
    row_b = pl.BlockSpec((1, nd, tb), lambda bi, h, c: (bi, 0, nb - 1 - c))
    os_ = jax.ShapeDtypeStruct((b, HEADS, s, HEAD_DIM), F32)
    return pl.pallas_call(
        _dn_kernel,
        grid=(b, HEADS, nb),
        in_specs=[fwd, fwd, fwd, bwd, bwd, bwd, vec_f, vec_b, vec_f, vec_b, row_f, row_b],
        out_specs=[fwd, bwd],
        out_shape=[os_, os_],
        scratch_shapes=[pltpu.VMEM((2, HEAD_DIM, HEAD_DIM), F32)],
        compiler_params=_params("parallel", "parallel", "arbitrary"),
        name="deltanet_scan",
    )(q, k, v, q, k, v, gc, gc, beta, beta, gc_rows, gc_rows)


def _dn_out_kernel(of_ref, ob_ref, z_ref, nw_ref, o_ref):
    nw = nw_ref[...]
    for h in range(HEADS):
        o = _rms(of_ref[0, h] + ob_ref[0, h], nw)
        z = z_ref[0, :, h * HEAD_DIM:(h + 1) * HEAD_DIM].astype(F32)
        o_ref[0, :, h * HEAD_DIM:(h + 1) * HEAD_DIM] = (o * (z * _sigmoid(z))).astype(o_ref.dtype)


def deltanet_out(o_f, o_b, proj, z_col0, out_norm, ts=ROW_TILE):
    b, _, s, _ = o_f.shape
    head_spec = pl.BlockSpec((1, HEADS, ts, HEAD_DIM), lambda bi, i: (bi, 0, i, 0))
    zb = z_col0 // A_V_W
    return pl.pallas_call(
        _dn_out_kernel,
        grid=(b, s // ts),
        in_specs=[head_spec, head_spec,
                  pl.BlockSpec((1, ts, A_V_W), lambda bi, i: (bi, i, zb)),
                  pl.BlockSpec((1, HEAD_DIM), lambda bi, i: (0, 0))],
        out_specs=pl.BlockSpec((1, ts, A_V_W), lambda bi, i: (bi, i, 0)),
        out_shape=jax.ShapeDtypeStruct((b, s, A_V_W), BF16),
        compiler_params=_params("parallel", "parallel"),
        name="deltanet_out",
    )(o_f, o_b, proj, out_norm.reshape(1, HEAD_DIM))


def _rope_tables(s):
    half = ROPE_DIMS // 2
    inv = ROPE_THETA ** (-jnp.arange(0, ROPE_DIMS, 2, dtype=F32) / ROPE_DIMS)
    ang = jnp.arange(s, dtype=F32)[:, None] * inv[None, :]
    cos, sin = jnp.cos(ang), jnp.sin(ang)
    rest = HEAD_DIM - ROPE_DIMS
    c = jnp.concatenate([cos, cos, jnp.ones((s, rest), F32)], axis=1)
    s1 = jnp.concatenate([jnp.zeros((s, half), F32), sin, jnp.zeros((s, rest), F32)], axis=1)
    s2 = jnp.concatenate([-sin, jnp.zeros((s, half + rest), F32)], axis=1)
    return c, s1, s2


def _trunk(x, p, ab_w_in, ab_conv_w, ab_a_log, ab_dt_bias, ab_out_norm, ab_rpb, ab_w_out,
           c_w_in, c_lambda, c_subln, c_w_out, norms, ffn_w_in, ffn_conv_w, ffn_conv_b,
           ffn_w_out, ple_w_proj, ple_w_gate):
    b, s, d = x.shape
    t = b * s
    depth = norms.shape[0]
    h = x.reshape(t, d)
    rope = _rope_tables(s)
    idx_r, idx_c, valid = _na_bias_index()
    o1 = A_QKV_W
    o2 = o1 + A_V_W
    o3 = o2 + 4 * HEADS
    for layer in range(depth):
        j = layer // 2
        if layer % 2 == 0:
            w_in = ab_w_in[j]
            w_main = jnp.concatenate([w_in[:, :o2], w_in[:, o3:]], axis=1).astype(BF16)
            w_gate = jnp.pad(w_in[:, o2:o3], ((0, 0), (0, LANES - 4 * HEADS))).astype(BF16)
            proj = norm_matmul(h, norms[layer, 0], w_main, BF16).reshape(b, s, -1)
            gates = norm_matmul(h, norms[layer, 0], w_gate, F32).reshape(b, s, LANES)
            qa, ka, va, gc, beta = deltanet_prep(proj, gates, ab_conv_w[j], ab_a_log[j], ab_dt_bias[j])
            o_f, o_b = deltanet_scan(qa, ka, va, gc, beta, jnp.transpose(gc, (0, 2, 1)))
            o_a = deltanet_out(o_f, o_b, proj, o1, ab_out_norm[j])
            bias = jnp.where(valid[None], ab_rpb[j][:, idx_r, idx_c], -jnp.inf)
            o_nb = neighbourhood_attention(proj, o2, bias)
            w_out = ab_w_out[j].astype(BF16)
            h = proj_norm_res([o_a.reshape(t, -1), o_nb.reshape(t, -1)], [w_out[:A_V_W], w_out[A_V_W:]],
                              norms[layer, 1], h)
        else:
            proj = norm_matmul(h, norms[layer, 0], c_w_in[j].astype(BF16), BF16, rope=rope,
                               rope_tiles=2 * C_QK_W // COL_TILE, seq_len=s).reshape(b, s, -1)
            o_c = diff_attention(proj, c_lambda[j], c_subln[j], layer)
            h = proj_norm_res([o_c.reshape(t, -1)], [c_w_out[j].astype(BF16)], norms[layer, 1], h)
        h = conv_ffn(h, norms[layer, 2], ffn_w_in[layer].astype(BF16), ffn_conv_w[layer], ffn_conv_b[layer],
                     ffn_w_out[layer].astype(BF16), norms[layer, 3], s)
        h = ple(h, p[layer].reshape(t, -1), ple_w_gate[layer].astype(BF16), ple_w_proj[layer].astype(BF16))
    return h.reshape(b, s, d)


def kernel(x_prompt, x_sample, p_prompt, p_sample, ab_w_in, ab_conv_w, ab_a_log, ab_dt_bias, ab_out_norm,
           ab_rpb, ab_w_out, c_w_in, c_lambda, c_subln, c_w_out, norms, ffn_w_in, ffn_conv_w, ffn_conv_b,
           ffn_w_out, ple_w_proj, ple_w_gate):
    nb = x_prompt.shape[0]
    x = jnp.concatenate([x_prompt, x_sample], axis=0)
    p = jnp.concatenate([p_prompt, p_sample], axis=1)
    y = _trunk(x, p, ab_w_in, ab_conv_w, ab_a_log, ab_dt_bias, ab_out_norm, ab_rpb, ab_w_out,
               c_w_in, c_lambda, c_subln, c_w_out, norms, ffn_w_in, ffn_conv_w, ffn_conv_b,
               ffn_w_out, ple_w_proj, ple_w_gate)
    return (y[:nb], y[nb:])
```

```python
import functools
import math

import jax
import jax.numpy as jnp
import numpy as np
from jax import lax
from jax.experimental import pallas as pl
from jax.experimental.pallas import tpu as pltpu

F32 = jnp.float32
BF16 = jnp.bfloat16

D_MODEL = 2048
PLE_DIM = 256
GRID_W = 64
RMS_EPS = 1e-6
HEADS = 8
HEAD_DIM = 128
A_CONV_W = 5
NA_ROWS = 8
NA_COLS = 16
ROPE_THETA = 500000.0
ROPE_DIMS = HEAD_DIM // 4
D_FF = 8192
FFN_CONV_W = 3
A_QKV_W = 3 * HEADS * HEAD_DIM
A_V_W = HEADS * HEAD_DIM
B_W = HEADS * HEAD_DIM
C_QK_W = HEADS * 2 * HEAD_DIM

LANES = 128
SUBLANES_BF16 = 16
VMEM_LIMIT = 56 * 1024 * 1024
NEG_BIG = -1e30

ROW_TILE = 512
COL_TILE = 512
FF_TILE = 512
DN_BLOCK = 256
DN_BASE = 64
DN_HEADS_PER_STEP = 2
ATT_Q_TILE = 256
NA_Q_ROWS = 8
NA_K_ROWS = 16


def _params(*sem):
    return pltpu.CompilerParams(dimension_semantics=sem, vmem_limit_bytes=VMEM_LIMIT)


def _rms(x, w):
    return x * lax.rsqrt(jnp.mean(x * x, axis=-1, keepdims=True) + RMS_EPS) * w


def _dot(a, b):
    return jnp.dot(a, b, preferred_element_type=F32)


def _dot_nt(a, b):
    return lax.dot_general(a, b, (((1,), (1,)), ((), ())), preferred_element_type=F32)


def _dot_tn(a, b):
    return lax.dot_general(a, b, (((0,), (0,)), ((), ())), preferred_element_type=F32)


def _sigmoid(x):
    return 1.0 / (1.0 + jnp.exp(-x))


def _norm_matmul_kernel(*refs, rope_tiles):
    if rope_tiles:
        x_ref, nw_ref, w_ref, rc_ref, rs1_ref, rs2_ref, o_ref, xn_ref = refs
    else:
        x_ref, nw_ref, w_ref, o_ref, xn_ref = refs
    j = pl.program_id(1)

    @pl.when(j == 0)
    def _():
        xn_ref[...] = _rms(x_ref[...], nw_ref[...]).astype(BF16)

    y = _dot(xn_ref[...], w_ref[...])
    if not rope_tiles:
        o_ref[...] = y.astype(o_ref.dtype)
        return

    @pl.when(j < rope_tiles)
    def _():
        n = y.shape[1]
        reps = n // LANES
        c = jnp.concatenate([rc_ref[...]] * reps, axis=1)
        s1 = jnp.concatenate([rs1_ref[...]] * reps, axis=1)
        s2 = jnp.concatenate([rs2_ref[...]] * reps, axis=1)
        half = ROPE_DIMS // 2
        r = y * c + pltpu.roll(y, half, 1) * s1 + pltpu.roll(y, n - half, 1) * s2
        o_ref[...] = r.astype(o_ref.dtype)

    @pl.when(j >= rope_tiles)
    def _():
        o_ref[...] = y.astype(o_ref.dtype)


def norm_matmul(x, nw, w, out_dtype, rope=None, rope_tiles=0, seq_len=None, tm=ROW_TILE, tn=COL_TILE):
    t, k = x.shape
    n = w.shape[1]
    tn = min(tn, n)
    assert t % tm == 0 and n % tn == 0
    in_specs = [
        pl.BlockSpec((tm, k), lambda i, j: (i, 0)),
        pl.BlockSpec((1, k), lambda i, j: (0, 0)),
        pl.BlockSpec((k, tn), lambda i, j: (0, j)),
    ]
    args = [x, nw.reshape(1, k), w]
    if rope_tiles:
        per_seq = seq_len // tm
        for tab in rope:
            in_specs.append(pl.BlockSpec((tm, LANES), lambda i, j: (i % per_seq, 0)))
            args.append(tab)
    return pl.pallas_call(
        functools.partial(_norm_matmul_kernel, rope_tiles=rope_tiles),
        grid=(t // tm, n // tn),
        in_specs=in_specs,
        out_specs=pl.BlockSpec((tm, tn), lambda i, j: (i, j)),
        out_shape=jax.ShapeDtypeStruct((t, n), out_dtype),
        scratch_shapes=[pltpu.VMEM((tm, k), BF16)],
        compiler_params=_params("parallel", "arbitrary"),
        name="norm_matmul",
    )(*args)


def _proj_norm_res_kernel(*refs, n_in):
    a_refs = refs[:n_in]
    w_refs = refs[n_in:2 * n_in]
    nw_ref, h_ref, o_ref = refs[2 * n_in:]
    y = _dot(a_refs[0][...], w_refs[0][...])
    for a_ref, w_ref in zip(a_refs[1:], w_refs[1:]):
        y = y + _dot(a_ref[...], w_ref[...])
    o_ref[...] = h_ref[...] + _rms(y, nw_ref[...])


def proj_norm_res(acts, ws, nw, h, tm=ROW_TILE):
    t, n = h.shape
    n_in = len(acts)
    in_specs = [pl.BlockSpec((tm, a.shape[1]), lambda i: (i, 0)) for a in acts]
    in_specs += [pl.BlockSpec(w.shape, lambda i: (0, 0)) for w in ws]
    in_specs += [pl.BlockSpec((1, n), lambda i: (0, 0)), pl.BlockSpec((tm, n), lambda i: (i, 0))]
    return pl.pallas_call(
        functools.partial(_proj_norm_res_kernel, n_in=n_in),
        grid=(t // tm,),
        in_specs=in_specs,
        out_specs=pl.BlockSpec((tm, n), lambda i: (i, 0)),
        out_shape=jax.ShapeDtypeStruct((t, n), F32),
        compiler_params=_params("parallel"),
        name="proj_norm_res",
    )(*acts, *ws, nw.reshape(1, n), h)


def _ple_kernel(h_ref, p_ref, wg_ref, wp_ref, o_ref):
    h = h_ref[...]
    gate = _sigmoid(_dot(h.astype(BF16), wg_ref[...]))
    o_ref[...] = h + gate * _dot(p_ref[...].astype(BF16), wp_ref[...])


def ple(h, p, wg, wp, tm=ROW_TILE):
    t, n = h.shape
    return pl.pallas_call(
        _ple_kernel,
        grid=(t // tm,),
        in_specs=[
            pl.BlockSpec((tm, n), lambda i: (i, 0)),
            pl.BlockSpec((tm, p.shape[1]), lambda i: (i, 0)),
            pl.BlockSpec(wg.shape, lambda i: (0, 0)),
            pl.BlockSpec(wp.shape, lambda i: (0, 0)),
        ],
        out_specs=pl.BlockSpec((tm, n), lambda i: (i, 0)),
        out_shape=jax.ShapeDtypeStruct((t, n), F32),
        compiler_params=_params("parallel"),
        name="ple",
    )(h, p, wg, wp)


HALO = SUBLANES_BF16


def _ffn_kernel(x_ref, xp_ref, xx_ref, nw_ref, wg_ref, wu_ref, cwg_ref, cwu_ref, cbg_ref, cbu_ref,
                wo_ref, nw2_ref, o_ref, xn_ref, hg_ref, hu_ref, acc_ref, *, tm, per_seq):
    i = pl.program_id(0)
    j = pl.program_id(1)
    k = x_ref.shape[1]

    @pl.when(j == 0)
    def _():
        nw = nw_ref[...]
        pos = i % per_seq
        prev = jnp.where(pos == 0, 0.0, _rms(xp_ref[...], nw))
        nxt = jnp.where(pos == per_seq - 1, 0.0, _rms(xx_ref[...], nw))
        pad = jnp.zeros((HALO - 8, k), F32)
        xn_ref[0:HALO, :] = jnp.concatenate([pad, prev], axis=0).astype(BF16)
        xn_ref[HALO:HALO + tm, :] = _rms(x_ref[...], nw).astype(BF16)
        xn_ref[HALO + tm:, :] = jnp.concatenate([nxt, pad], axis=0).astype(BF16)
        acc_ref[...] = jnp.zeros_like(acc_ref)

    xe = xn_ref[...]
    hg_ref[...] = _dot(xe, wg_ref[...])
    hu_ref[...] = _dot(xe, wu_ref[...])

    def conv(h_ref, cw_ref, cb_ref):
        cw = cw_ref[...]
        return (h_ref[pl.ds(HALO - 1, tm), :] * cw[0:1] + h_ref[pl.ds(HALO, tm), :] * cw[1:2]
                + h_ref[pl.ds(HALO + 1, tm), :] * cw[2:3] + cb_ref[...])

    g = conv(hg_ref, cwg_ref, cbg_ref)
    u = conv(hu_ref, cwu_ref, cbu_ref)
    c0 = math.sqrt(2.0 / math.pi)
    gelu = 0.5 * g * (1.0 + jnp.tanh(c0 * (g + 0.044715 * (g * g * g))))
    acc_ref[...] += _dot((gelu * u).astype(BF16), wo_ref[...])

    @pl.when(j == pl.num_programs(1) - 1)
    def _():
        o_ref[...] = x_ref[...] + _rms(acc_ref[...], nw2_ref[...])


def conv_ffn(h, nw_in, w_in, conv_w, conv_b, w_out, nw_out, seq_len, tm=ROW_TILE, tf=FF_TILE):
    t, d = h.shape
    f = w_out.shape[0]
    nf = f // tf
    per_seq = seq_len // tm
    r8 = tm // 8
    last8 = t // 8 - 1
    cb = conv_b.reshape(1, 2 * f)
    return pl.pallas_call(
        functools.partial(_ffn_kernel, tm=tm, per_seq=per_seq),
        grid=(t // tm, nf),
        in_specs=[
            pl.BlockSpec((tm, d), lambda i, j: (i, 0)),
            pl.BlockSpec((8, d), lambda i, j: (jnp.maximum(i * r8 - 1, 0), 0)),
            pl.BlockSpec((8, d), lambda i, j: (jnp.minimum((i + 1) * r8, last8), 0)),
            pl.BlockSpec((1, d), lambda i, j: (0, 0)),
            pl.BlockSpec((d, tf), lambda i, j: (0, j)),
            pl.BlockSpec((d, tf), lambda i, j: (0, j + nf)),
            pl.BlockSpec((FFN_CONV_W, tf), lambda i, j: (0, j)),
            pl.BlockSpec((FFN_CONV_W, tf), lambda i, j: (0, j + nf)),
            pl.BlockSpec((1, tf), lambda i, j: (0, j)),
            pl.BlockSpec((1, tf), lambda i, j: (0, j + nf)),
            pl.BlockSpec((tf, d), lambda i, j: (j, 0)),
            pl.BlockSpec((1, d), lambda i, j: (0, 0)),
        ],
        out_specs=pl.BlockSpec((tm, d), lambda i, j: (i, 0)),
        out_shape=jax.ShapeDtypeStruct((t, d), F32),
        scratch_shapes=[
            pltpu.VMEM((tm + 2 * HALO, d), BF16),
            pltpu.VMEM((tm + 2 * HALO, tf), F32),
            pltpu.VMEM((tm + 2 * HALO, tf), F32),
            pltpu.VMEM((tm, d), F32),
        ],
        compiler_params=_params("parallel", "arbitrary"),
        name="conv_ffn",
    )(h, h, h, nw_in.reshape(1, d), w_in, w_in, conv_w, conv_w, cb, cb, w_out, nw_out.reshape(1, d))


def _diff_attn_kernel(lp_ref, sub_ref, q_ref, k_ref, v_ref, o_ref, *, lambda_init):
    lp = lp_ref[...]
    lam = (jnp.exp(jnp.sum(lp[0:1] * lp[1:2], axis=-1, keepdims=True))
           - jnp.exp(jnp.sum(lp[2:3] * lp[3:4], axis=-1, keepdims=True)) + lambda_init)
    q = (q_ref[0].astype(F32) * (HEAD_DIM ** -0.5 * math.log2(math.e))).astype(BF16)

    def unnormalised(lo):
        s = _dot_nt(q[:, lo:lo + HEAD_DIM], k_ref[0, :, lo:lo + HEAD_DIM])
        e = jnp.exp2(s - jnp.max(s, axis=-1, keepdims=True))
        return e, jnp.sum(e, axis=-1, keepdims=True)

    e1, l1 = unnormalised(0)
    e2, l2 = unnormalised(HEAD_DIM)
    a = e1 - e2 * (lam * l1 / l2)
    o = _dot(a.astype(BF16), v_ref[0]) * (1.0 / l1)
    o_ref[0] = (_rms(o, sub_ref[...]) * (1.0 - lambda_init)).astype(o_ref.dtype)


def diff_attention(proj, lam_params, subln, layer, tq=ATT_Q_TILE):
    b, s, _ = proj.shape
    w = 2 * HEAD_DIM
    lambda_init = 0.8 - 0.6 * math.exp(-0.3 * layer)
    return pl.pallas_call(
        functools.partial(_diff_attn_kernel, lambda_init=lambda_init),
        grid=(b, HEADS, s // tq),
        in_specs=[
            pl.BlockSpec((4, HEAD_DIM), lambda bi, h, qi: (0, 0)),
            pl.BlockSpec((1, w), lambda bi, h, qi: (0, 0)),
            pl.BlockSpec((1, tq, w), lambda bi, h, qi: (bi, qi, h)),
            pl.BlockSpec((1, s, w), lambda bi, h, qi: (bi, 0, HEADS + h)),
            pl.BlockSpec((1, s, w), lambda bi, h, qi: (bi, 0, 2 * HEADS + h)),
        ],
        out_specs=pl.BlockSpec((1, tq, w), lambda bi, h, qi: (bi, qi, h)),
        out_shape=jax.ShapeDtypeStruct((b, s, C_QK_W), BF16),
        compiler_params=_params("parallel", "parallel", "arbitrary"),
        name="diff_attention",
    )(lam_params, subln.reshape(1, w), proj, proj, proj)


def _na_bias(rpb):
    rows = GRID_W
    sel_r = np.zeros((3, NA_Q_ROWS, NA_K_ROWS, 2 * NA_ROWS - 1), np.float32)
    for case, rb in enumerate((0, 1, rows // NA_Q_ROWS - 1)):
        base = int(np.clip(rb * NA_Q_ROWS - (NA_K_ROWS - NA_Q_ROWS) // 2, 0, rows - NA_K_ROWS))
        for ql in range(NA_Q_ROWS):
            r = rb * NA_Q_ROWS + ql
            row_start = int(np.clip(r - NA_ROWS // 2, 0, rows - NA_ROWS))
            for kl in range(NA_K_ROWS):
                kr = base + kl
                if row_start <= kr < row_start + NA_ROWS:
                    sel_r[case, ql, kl, kr - r + NA_ROWS - 1] = 1.0
    c = np.arange(GRID_W)
    col_start = np.clip(c - NA_COLS // 2, 0, GRID_W - NA_COLS)
    col_ok = (c[None, :] >= col_start[:, None]) & (c[None, :] < col_start[:, None] + NA_COLS)
    dc = np.clip(c[None, :] - c[:, None] + NA_COLS - 1, 0, 2 * NA_COLS - 2)
    sel_c = (np.arange(2 * NA_COLS - 1)[:, None, None] == dc[None]) & col_ok[None]
    valid = (sel_r.sum(-1) > 0)[:, :, None, :, None] & col_ok[None, None, :, None, :]
    t1 = jnp.einsum("hab,cqka->hcqkb", rpb, sel_r, precision=lax.Precision.HIGHEST)
    t2 = jnp.einsum("hcqkb,bxy->hcqxky", t1, sel_c.astype(np.float32), precision=lax.Precision.HIGHEST)
    bias = jnp.where(valid[None], t2 * math.log2(math.e), -jnp.inf)
    return bias.reshape(rpb.shape[0], 3, NA_Q_ROWS * GRID_W, NA_K_ROWS * GRID_W)


def _na_kernel(q_ref, k_ref, v_ref, b_ref, o_ref):
    rb = pl.program_id(2)
    nk = NA_K_ROWS * GRID_W
    lo = (NA_K_ROWS - NA_Q_ROWS) // 2 * GRID_W
    start = jnp.clip(rb * (NA_Q_ROWS * GRID_W) - lo, 0, k_ref.shape[1] - nk)
    start = pl.multiple_of(start, lo)
    q = (q_ref[0].astype(F32) * (HEAD_DIM ** -0.5 * math.log2(math.e))).astype(BF16)
    s = _dot_nt(q, k_ref[0, pl.ds(start, nk), :]) + b_ref[0, 0]
    e = jnp.exp2(s - jnp.max(s, axis=-1, keepdims=True))
    inv_l = 1.0 / jnp.sum(e, axis=-1, keepdims=True)
    o_ref[0] = (_dot(e.astype(BF16), v_ref[0, pl.ds(start, nk), :]) * inv_l).astype(o_ref.dtype)


def neighbourhood_attention(proj, col0, bias):
    b, s, _ = proj.shape
    nq = NA_Q_ROWS * GRID_W
    nblk = s // nq
    c0 = col0 // HEAD_DIM

    def bias_map(bi, h, rb):
        return (h, jnp.where(rb == 0, 0, jnp.where(rb == nblk - 1, 2, 1)), 0, 0)

    return pl.pallas_call(
        _na_kernel,
        grid=(b, HEADS, nblk),
        in_specs=[
            pl.BlockSpec((1, nq, HEAD_DIM), lambda bi, h, rb: (bi, rb, c0 + h)),
            pl.BlockSpec((1, s, HEAD_DIM), lambda bi, h, rb: (bi, 0, c0 + HEADS + h)),
            pl.BlockSpec((1, s, HEAD_DIM), lambda bi, h, rb: (bi, 0, c0 + 2 * HEADS + h)),
            pl.BlockSpec((1, 1) + bias.shape[2:], bias_map),
        ],
        out_specs=pl.BlockSpec((1, nq, HEAD_DIM), lambda bi, h, rb: (bi, rb, h)),
        out_shape=jax.ShapeDtypeStruct((b, s, B_W), BF16),
        compiler_params=_params("parallel", "parallel", "arbitrary"),
        name="neighbourhood_attention",
    )(proj, proj, proj, bias)


def _split3(x):
    hi = x.astype(BF16)
    r1 = x - hi.astype(F32)
    mid = r1.astype(BF16)
    lo = (r1 - mid.astype(F32)).astype(BF16)
    return hi, mid, lo


def _dn_prep_kernel(x_ref, xp_ref, xx_ref, cw_ref, gt_ref, alog_ref, dtb_ref,
                    q_ref, k_ref, v_ref, gc_ref, beta_ref, xs_ref):
    i = pl.program_id(1)
    tb = x_ref.shape[1]
    pad = A_CONV_W // 2
    xs_ref[0:HALO, :] = jnp.where(i == 0, 0.0, xp_ref[0].astype(F32))
    xs_ref[HALO:HALO + tb, :] = x_ref[0].astype(F32)
    xs_ref[HALO + tb:, :] = jnp.where(i == pl.num_programs(1) - 1, 0.0, xx_ref[0].astype(F32))
    cw = cw_ref[...]
    y = xs_ref[pl.ds(HALO - pad, tb), :] * cw[0:1]
    for t in range(1, A_CONV_W):
        y = y + xs_ref[pl.ds(HALO - pad + t, tb), :] * cw[t:t + 1]
    y = y * _sigmoid(y)
    for h in range(HEADS):
        qh = y[:, h * HEAD_DIM:(h + 1) * HEAD_DIM]
        kh = y[:, (HEADS + h) * HEAD_DIM:(HEADS + h + 1) * HEAD_DIM]
        qn = qh * (lax.rsqrt(jnp.sum(qh * qh, axis=-1, keepdims=True) + 1e-6) * (HEAD_DIM ** -0.5))
        kn = kh * lax.rsqrt(jnp.sum(kh * kh, axis=-1, keepdims=True) + 1e-6)
        q_ref[0, h] = qn.astype(q_ref.dtype)
        k_ref[0, h] = kn.astype(k_ref.dtype)
        v_ref[0, h] = y[:, (2 * HEADS + h) * HEAD_DIM:(2 * HEADS + h + 1) * HEAD_DIM].astype(v_ref.dtype)

    nd = 2 * HEADS
    gates = gt_ref[0]
    z = gates[:, 0:nd] + dtb_ref[...]
    softplus = jnp.maximum(z, 0.0) + jnp.log(1.0 + jnp.exp(-jnp.abs(z)))
    g = -jnp.exp(alog_ref[...]) * softplus
    beta_ref[0] = _sigmoid(gates[:, nd:2 * nd])
    r = lax.broadcasted_iota(jnp.int32, (tb, tb), 0)
    c = lax.broadcasted_iota(jnp.int32, (tb, tb), 1)
    lower = jnp.where(r >= c, 1.0, 0.0).astype(BF16)
    upper = jnp.where(r <= c, 1.0, 0.0).astype(BF16)
    parts = _split3(g)
    fwd = _dot(lower, parts[0]) + _dot(lower, parts[1]) + _dot(lower, parts[2])
    bwd = _dot(upper, parts[0]) + _dot(upper, parts[1]) + _dot(upper, parts[2])
    col = lax.broadcasted_iota(jnp.int32, (tb, nd), 1)
    gc_ref[0] = jnp.where(col < HEADS, fwd, bwd)


def deltanet_prep(proj, gates, conv_w, a_log, dt_bias, tb=DN_BLOCK):
    b, s, _ = proj.shape
    nb = s // tb
    rh = tb // HALO
    nd = 2 * HEADS
    hs = jax.ShapeDtypeStruct((b, HEADS, s, HEAD_DIM), BF16)
    head_spec = pl.BlockSpec((1, HEADS, tb, HEAD_DIM), lambda bi, i: (bi, 0, i, 0))
    vec_spec = pl.BlockSpec((1, tb, nd), lambda bi, i: (bi, i, 0))
    return pl.pallas_call(
        _dn_prep_kernel,
        grid=(b, nb),
        in_specs=[
            pl.BlockSpec((1, tb, A_QKV_W), lambda bi, i: (bi, i, 0)),
            pl.BlockSpec((1, HALO, A_QKV_W), lambda bi, i: (bi, jnp.maximum(i * rh - 1, 0), 0)),
            pl.BlockSpec((1, HALO, A_QKV_W), lambda bi, i: (bi, jnp.minimum((i + 1) * rh, s // HALO - 1), 0)),
            pl.BlockSpec((A_CONV_W, A_QKV_W), lambda bi, i: (0, 0)),
            pl.BlockSpec((1, tb, LANES), lambda bi, i: (bi, i, 0)),
            pl.BlockSpec((1, nd), lambda bi, i: (0, 0)),
            pl.BlockSpec((1, nd), lambda bi, i: (0, 0)),
        ],
        out_specs=[head_spec, head_spec, head_spec, vec_spec, vec_spec],
        out_shape=[hs, hs, hs, jax.ShapeDtypeStruct((b, s, nd), F32), jax.ShapeDtypeStruct((b, s, nd), F32)],
        scratch_shapes=[pltpu.VMEM((tb + 2 * HALO, A_QKV_W), F32)],
        compiler_params=_params("parallel", "arbitrary"),
        name="deltanet_prep",
    )(proj, proj, proj, conv_w, gates, a_log.reshape(1, nd), dt_bias.reshape(1, nd))


def _split2(x):
    hi = x.astype(BF16)
    return hi, (x - hi.astype(F32)).astype(BF16)


def _dot3(a, b):
    lhs = jnp.concatenate([a[0], a[0], a[1]], axis=1)
    rhs = jnp.concatenate([b[0], b[1], b[0]], axis=0)
    return _dot(lhs, rhs)


def _unit_triangular_solve(ms, rhss, same_block):
    n = ms[0].shape[0]
    r = lax.broadcasted_iota(jnp.int32, (n, n), 0)
    c = lax.broadcasted_iota(jnp.int32, (n, n), 1)
    eye = jnp.where(r == c, 1.0, 0.0)
    mds = [jnp.where(same_block(DN_BASE), m, 0.0) for m in ms]
    pws = [_split2(md) for md in mds]
    ts = [eye + md for md in mds]
    size = 2
    while size < DN_BASE:
        pws = [_split2(_dot3(pw, pw)) for pw in pws]
        ts = [t + _dot3(pw, _split2(t)) for pw, t in zip(pws, ts)]
        size *= 2
    t_bases = [t.astype(BF16) for t in ts]
    offs = {}
    blk = 2 * DN_BASE
    while blk <= n:
        offs[blk] = [jnp.where(same_block(blk), jnp.where(same_block(blk // 2), 0.0, m), 0.0).astype(BF16)
                     for m in ms]
        blk *= 2

    def apply(blk, ys):
        if blk == DN_BASE:
            return [_dot(tb, y.astype(BF16)) for tb, y in zip(t_bases, ys)]
        zs = apply(blk // 2, ys)
        corr = apply(blk // 2, [_dot(off, z.astype(BF16)) for off, z in zip(offs[blk], zs)])
        return [z + cr for z, cr in zip(zs, corr)]

    return apply(n, rhss)


def _dn_block(chains):
    n = chains[0][0].shape[0]
    r = lax.broadcasted_iota(jnp.int32, (n, n), 0)
    c = lax.broadcasted_iota(jnp.int32, (n, n), 1)

    def same_block(size):
        return (r // size) == (c // size)

    ms, rhss, pre = [], [], []
    for q, k, v, gcol, grow, beta, state_ref, forward in chains:
        d = (r - c) if forward else (c - r)
        kf = k.astype(F32)
        kb = kf * beta
        decay = jnp.exp(jnp.where(d >= 0, gcol - grow, NEG_BIG))
        ms.append(jnp.where(d > 0, -(_dot_nt(kb.astype(BF16), k) * decay), 0.0))
        intra = (_dot_nt(q, k) * decay).astype(BF16)
        eg = jnp.exp(gcol)
        rhss.append(jnp.concatenate([v.astype(F32) * beta, kb * eg], axis=1))
        g_last = grow[:, n - 1:n] if forward else grow[:, 0:1]
        pre.append((intra, (q.astype(F32) * eg).astype(BF16), (kf * jnp.exp(g_last - gcol)).astype(BF16),
                    jnp.exp(g_last)))
    sols = _unit_triangular_solve(ms, rhss, same_block)
    outs = []
    for chain, sol, (intra, q_dec, k_dec, blk_decay) in zip(chains, sols, pre):
        state_ref = chain[6]
        state = state_ref[...]
        sb = state.astype(BF16)
        v_new = sol[:, :HEAD_DIM] - _dot(sol[:, HEAD_DIM:].astype(BF16), sb)
        vb = v_new.astype(BF16)
        outs.append(_dot(q_dec, sb) + _dot(intra, vb))
        state_ref[...] = state * blk_decay + _dot_tn(k_dec, vb)
    return outs


def _dn_kernel(qf_ref, kf_ref, vf_ref, qb_ref, kb_ref, vb_ref, gcf_ref, gcb_ref, btf_ref, btb_ref,
               grf_ref, grb_ref, of_ref, ob_ref, state_ref):
    hb = qf_ref.shape[1]

    @pl.when(pl.program_id(2) == 0)
    def _():
        state_ref[...] = jnp.zeros_like(state_ref)

    lane = lax.broadcasted_iota(jnp.int32, gcf_ref.shape[1:], 1)

    def pick(ref, idx):
        return jnp.sum(jnp.where(lane == idx, ref[0], 0.0), axis=-1, keepdims=True)

    chains = []
    for hl in range(hb):
        h = pl.program_id(1) * hb + hl
        chains.append((qf_ref[0, hl], kf_ref[0, hl], vf_ref[0, hl], pick(gcf_ref, h),
                       grf_ref[0, pl.ds(h, 1), :], pick(btf_ref, h), state_ref.at[0, hl], True))
        chains.append((qb_ref[0, hl], kb_ref[0, hl], vb_ref[0, hl], pick(gcb_ref, HEADS + h),
                       grb_ref[0, pl.ds(HEADS + h, 1), :], pick(btb_ref, HEADS + h), state_ref.at[1, hl], False))
    outs = _dn_block(chains)
    for hl in range(hb):
        of_ref[0, hl] = outs[2 * hl]
        ob_ref[0, hl] = outs[2 * hl + 1]


def deltanet_scan(q, k, v, gc, beta, gc_rows, tb=DN_BLOCK, hb=DN_HEADS_PER_STEP):
    b, _, s, _ = q.shape
    nb = s // tb
    nd = 2 * HEADS
    fwd = pl.BlockSpec((1, hb, tb, HEAD_DIM), lambda bi, h, c: (bi, h, c, 0))
    bwd = pl.BlockSpec((1, hb, tb, HEAD_DIM), lambda bi, h, c: (bi, h, nb - 1 - c, 0))
    vec_f = pl.BlockSpec((1, tb, nd), lambda bi, h, c: (bi, c, 0))
    vec_b = pl.BlockSpec((1, tb, nd), lambda bi, h, c: (bi, nb - 1 - c, 0))
    row_f = pl.BlockSpec((1, nd, tb), lambda bi, h, c: (bi, 0, c))
    row_b = pl.BlockSpec((1, nd, tb), lambda bi, h, c: (bi, 0, nb - 1 - c))
    os_ = jax.ShapeDtypeStruct((b, HEADS, s, HEAD_DIM), F32)
    return pl.pallas_call(
        _dn_kernel,
        grid=(b, HEADS // hb, nb),
        in_specs=[fwd, fwd, fwd, bwd, bwd, bwd, vec_f, vec_b, vec_f, vec_b, row_f, row_b],
        out_specs=[fwd, bwd],
        out_shape=[os_, os_],
        scratch_shapes=[pltpu.VMEM((2, hb, HEAD_DIM, HEAD_DIM), F32)],
        compiler_params=_params("parallel", "parallel", "arbitrary"),
        name="deltanet_scan",
    )(q, k, v, q, k, v, gc, gc, beta, beta, gc_rows, gc_rows)


def _dn_out_kernel(of_ref, ob_ref, z_ref, nw_ref, o_ref):
    nw = nw_ref[...]
    for h in range(HEADS):
        o = _rms(of_ref[0, h] + ob_ref[0, h], nw)
        z = z_ref[0, :, h * HEAD_DIM:(h + 1) * HEAD_DIM].astype(F32)
        o_ref[0, :, h * HEAD_DIM:(h + 1) * HEAD_DIM] = (o * (z * _sigmoid(z))).astype(o_ref.dtype)


def deltanet_out(o_f, o_b, proj, z_col0, out_norm, ts=ROW_TILE):
    b, _, s, _ = o_f.shape
    head_spec = pl.BlockSpec((1, HEADS, ts, HEAD_DIM), lambda bi, i: (bi, 0, i, 0))
    zb = z_col0 // A_V_W
    return pl.pallas_call(
        _dn_out_kernel,
        grid=(b, s // ts),
        in_specs=[head_spec, head_spec,
                  pl.BlockSpec((1, ts, A_V_W), lambda bi, i: (bi, i, zb)),
                  pl.BlockSpec((1, HEAD_DIM), lambda bi, i: (0, 0))],
        out_specs=pl.BlockSpec((1, ts, A_V_W), lambda bi, i: (bi, i, 0)),
        out_shape=jax.ShapeDtypeStruct((b, s, A_V_W), BF16),
        compiler_params=_params("parallel", "parallel"),
        name="deltanet_out",
    )(o_f, o_b, proj, out_norm.reshape(1, HEAD_DIM))


def _rope_tables(s):
    half = ROPE_DIMS // 2
    inv = ROPE_THETA ** (-jnp.arange(0, ROPE_DIMS, 2, dtype=F32) / ROPE_DIMS)
    ang = jnp.arange(s, dtype=F32)[:, None] * inv[None, :]
    cos, sin = jnp.cos(ang), jnp.sin(ang)
    rest = HEAD_DIM - ROPE_DIMS
    c = jnp.concatenate([cos, cos, jnp.ones((s, rest), F32)], axis=1)
    s1 = jnp.concatenate([jnp.zeros((s, half), F32), sin, jnp.zeros((s, rest), F32)], axis=1)
    s2 = jnp.concatenate([-sin, jnp.zeros((s, half + rest), F32)], axis=1)
    return c, s1, s2


def _trunk(x, p, ab_w_in, ab_conv_w, ab_a_log, ab_dt_bias, ab_out_norm, ab_rpb, ab_w_out,
           c_w_in, c_lambda, c_subln, c_w_out, norms, ffn_w_in, ffn_conv_w, ffn_conv_b,
           ffn_w_out, ple_w_proj, ple_w_gate):
    b, s, d = x.shape
    t = b * s
    depth = norms.shape[0]
    h = x.reshape(t, d)
    rope = _rope_tables(s)
    o1 = A_QKV_W
    o2 = o1 + A_V_W
    o3 = o2 + 4 * HEADS
    for layer in range(depth):
        j = layer // 2
        if layer % 2 == 0:
            w_in = ab_w_in[j]
            w_main = jnp.concatenate([w_in[:, :o2], w_in[:, o3:]], axis=1).astype(BF16)
            w_gate = jnp.pad(w_in[:, o2:o3], ((0, 0), (0, LANES - 4 * HEADS))).astype(BF16)
            proj = norm_matmul(h, norms[layer, 0], w_main, BF16).reshape(b, s, -1)
            gates = norm_matmul(h, norms[layer, 0], w_gate, F32).reshape(b, s, LANES)
            qa, ka, va, gc, beta = deltanet_prep(proj, gates, ab_conv_w[j], ab_a_log[j], ab_dt_bias[j])
            o_f, o_b = deltanet_scan(qa, ka, va, gc, beta, jnp.transpose(gc, (0, 2, 1)))
            o_a = deltanet_out(o_f, o_b, proj, o1, ab_out_norm[j])
            o_nb = neighbourhood_attention(proj, o2, _na_bias(ab_rpb[j]))
            w_out = ab_w_out[j].astype(BF16)
            h = proj_norm_res([o_a.reshape(t, -1), o_nb.reshape(t, -1)], [w_out[:A_V_W], w_out[A_V_W:]],
                              norms[layer, 1], h)
        else:
            proj = norm_matmul(h, norms[layer, 0], c_w_in[j].astype(BF16), BF16, rope=rope,
                               rope_tiles=2 * C_QK_W // COL_TILE, seq_len=s).reshape(b, s, -1)
            o_c = diff_attention(proj, c_lambda[j], c_subln[j], layer)
            h = proj_norm_res([o_c.reshape(t, -1)], [c_w_out[j].astype(BF16)], norms[layer, 1], h)
        h = conv_ffn(h, norms[layer, 2], ffn_w_in[layer].astype(BF16), ffn_conv_w[layer], ffn_conv_b[layer],
                     ffn_w_out[layer].astype(BF16), norms[layer, 3], s)
        h = ple(h, p[layer].reshape(t, -1), ple_w_gate[layer].astype(BF16), ple_w_proj[layer].astype(BF16))
    return h.reshape(b, s, d)


def kernel(x_prompt, x_sample, p_prompt, p_sample, ab_w_in, ab_conv_w, ab_a_log, ab_dt_bias, ab_out_norm,
           ab_rpb, ab_w_out, c_w_in, c_lambda, c_subln, c_w_out, norms, ffn_w_in, ffn_conv_w, ffn_conv_b,
           ffn_w_out, ple_w_proj, ple_w_gate):
    nb = x_prompt.shape[0]
    x = jnp.concatenate([x_prompt, x_sample], axis=0)
    p = jnp.concatenate([p_prompt, p_sample], axis=1)
    y = _trunk(x, p, ab_w_in, ab_conv_w, ab_a_log, ab_dt_bias, ab_out_norm, ab_rpb, ab_w_out,
               c_w_in, c_lambda, c_subln, c_w_out, norms, ffn_w_in, ffn_conv_w, ffn_conv_b,
               ffn_w_out, ple_w_proj, ple_w_gate)
    return (y[:nb], y[nb:])
```

```python
import functools
import math

import jax
import jax.numpy as jnp
import numpy as np
from jax import lax
from jax.experimental import pallas as pl
from jax.experimental.pallas import tpu as pltpu

F32 = jnp.float32
BF16 = jnp.bfloat16

D_MODEL = 2048
PLE_DIM = 256
GRID_W = 64
RMS_EPS = 1e-6
HEADS = 8
HEAD_DIM = 128
A_CONV_W = 5
NA_ROWS = 8
NA_COLS = 16
ROPE_THETA = 500000.0
ROPE_DIMS = HEAD_DIM // 4
D_FF = 8192
FFN_CONV_W = 3
A_QKV_W = 3 * HEADS * HEAD_DIM
A_V_W = HEADS * HEAD_DIM
B_W = HEADS * HEAD_DIM
C_QK_W = HEADS * 2 * HEAD_DIM

LANES = 128
SUBLANES_BF16 = 16
VMEM_LIMIT = 56 * 1024 * 1024
NEG_BIG = -1e30

ROW_TILE = 512
MM_ROW_TILE = 1024
COL_TILE = 512
FF_TILE = 512
DN_BLOCK = 256
DN_BASE = 64
DN_GROUP = 128
DN_HEADS_PER_STEP = 2
ATT_Q_TILE = 512
ATT_SUB_ROWS = 256
NA_Q_ROWS = 8
NA_K_ROWS = 16


def _params(*sem):
    return pltpu.CompilerParams(dimension_semantics=sem, vmem_limit_bytes=VMEM_LIMIT)


def _rms(x, w):
    return x * lax.rsqrt(jnp.mean(x * x, axis=-1, keepdims=True) + RMS_EPS) * w


def _dot(a, b):
    return jnp.dot(a, b, preferred_element_type=F32)


def _dot_nt(a, b):
    return lax.dot_general(a, b, (((1,), (1,)), ((), ())), preferred_element_type=F32)


def _dot_tn(a, b):
    return lax.dot_general(a, b, (((0,), (0,)), ((), ())), preferred_element_type=F32)


def _sigmoid(x):
    return 1.0 / (1.0 + jnp.exp(-x))


def _norm_matmul_kernel(*refs, rope_tiles):
    if rope_tiles:
        x_ref, nw_ref, w_ref, rc_ref, rs1_ref, rs2_ref, o_ref, xn_ref = refs
    else:
        x_ref, nw_ref, w_ref, o_ref, xn_ref = refs
    j = pl.program_id(1)

    @pl.when(j == 0)
    def _():
        xn_ref[...] = _rms(x_ref[...], nw_ref[...]).astype(BF16)

    y = _dot(xn_ref[...], w_ref[0])
    if not rope_tiles:
        o_ref[...] = y.astype(o_ref.dtype)
        return

    @pl.when(j < rope_tiles)
    def _():
        n = y.shape[1]
        reps = n // LANES
        c = jnp.concatenate([rc_ref[...]] * reps, axis=1)
        s1 = jnp.concatenate([rs1_ref[...]] * reps, axis=1)
        s2 = jnp.concatenate([rs2_ref[...]] * reps, axis=1)
        half = ROPE_DIMS // 2
        r = y * c + pltpu.roll(y, half, 1) * s1 + pltpu.roll(y, n - half, 1) * s2
        o_ref[...] = r.astype(o_ref.dtype)

    @pl.when(j >= rope_tiles)
    def _():
        o_ref[...] = y.astype(o_ref.dtype)


def _column_tiles(w, tn):
    k, n = w.shape
    return jnp.transpose(w.reshape(k, n // tn, tn), (1, 0, 2)).astype(BF16)


def norm_matmul(x, nw, w, out_dtype, rope=None, rope_tiles=0, seq_len=None, tm=MM_ROW_TILE, tn=COL_TILE):
    t, k = x.shape
    n = w.shape[1]
    tn = min(tn, n)
    tm = min(tm, t)
    assert t % tm == 0 and n % tn == 0
    in_specs = [
        pl.BlockSpec((tm, k), lambda i, j: (i, 0)),
        pl.BlockSpec((1, k), lambda i, j: (0, 0)),
        pl.BlockSpec((1, k, tn), lambda i, j: (j, 0, 0)),
    ]
    args = [x, nw.reshape(1, k), _column_tiles(w, tn)]
    if rope_tiles:
        per_seq = seq_len // tm
        for tab in rope:
            in_specs.append(pl.BlockSpec((tm, LANES), lambda i, j: (i % per_seq, 0)))
            args.append(tab)
    return pl.pallas_call(
        functools.partial(_norm_matmul_kernel, rope_tiles=rope_tiles),
        grid=(t // tm, n // tn),
        in_specs=in_specs,
        out_specs=pl.BlockSpec((tm, tn), lambda i, j: (i, j)),
        out_shape=jax.ShapeDtypeStruct((t, n), out_dtype),
        scratch_shapes=[pltpu.VMEM((tm, k), BF16)],
        compiler_params=_params("parallel", "arbitrary"),
        name="norm_matmul",
    )(*args)


def _proj_norm_res_kernel(*refs, n_in):
    a_refs = refs[:n_in]
    w_refs = refs[n_in:2 * n_in]
    nw_ref, h_ref, o_ref = refs[2 * n_in:]
    y = _dot(a_refs[0][...], w_refs[0][...])
    for a_ref, w_ref in zip(a_refs[1:], w_refs[1:]):
        y = y + _dot(a_ref[...], w_ref[...])
    o_ref[...] = h_ref[...] + _rms(y, nw_ref[...])


def proj_norm_res(acts, ws, nw, h, tm=ROW_TILE):
    t, n = h.shape
    n_in = len(acts)
    in_specs = [pl.BlockSpec((tm, a.shape[1]), lambda i: (i, 0)) for a in acts]
    in_specs += [pl.BlockSpec(w.shape, lambda i: (0, 0)) for w in ws]
    in_specs += [pl.BlockSpec((1, n), lambda i: (0, 0)), pl.BlockSpec((tm, n), lambda i: (i, 0))]
    return pl.pallas_call(
        functools.partial(_proj_norm_res_kernel, n_in=n_in),
        grid=(t // tm,),
        in_specs=in_specs,
        out_specs=pl.BlockSpec((tm, n), lambda i: (i, 0)),
        out_shape=jax.ShapeDtypeStruct((t, n), F32),
        compiler_params=_params("parallel"),
        name="proj_norm_res",
    )(*acts, *ws, nw.reshape(1, n), h)


def _ple_kernel(h_ref, p_ref, wg_ref, wp_ref, o_ref):
    h = h_ref[...]
    gate = _sigmoid(_dot(h.astype(BF16), wg_ref[...]))
    o_ref[...] = h + gate * _dot(p_ref[...].astype(BF16), wp_ref[...])


def ple(h, p, wg, wp, tm=ROW_TILE):
    t, n = h.shape
    return pl.pallas_call(
        _ple_kernel,
        grid=(t // tm,),
        in_specs=[
            pl.BlockSpec((tm, n), lambda i: (i, 0)),
            pl.BlockSpec((tm, p.shape[1]), lambda i: (i, 0)),
            pl.BlockSpec(wg.shape, lambda i: (0, 0)),
            pl.BlockSpec(wp.shape, lambda i: (0, 0)),
        ],
        out_specs=pl.BlockSpec((tm, n), lambda i: (i, 0)),
        out_shape=jax.ShapeDtypeStruct((t, n), F32),
        compiler_params=_params("parallel"),
        name="ple",
    )(h, p, wg, wp)


def _ffn_kernel(x_ref, xp_ref, xx_ref, nw_ref, wg_ref, wu_ref, cwg_ref, cwu_ref, cbg_ref, cbu_ref,
                wo_ref, nw2_ref, o_ref, xn_ref, hg_ref, hu_ref, acc_ref, *, tm, per_seq):
    i = pl.program_id(0)
    j = pl.program_id(1)

    @pl.when(j == 0)
    def _():
        nw = nw_ref[...]
        pos = i % per_seq
        prev = jnp.where(pos == 0, 0.0, _rms(xp_ref[...], nw))
        nxt = jnp.where(pos == per_seq - 1, 0.0, _rms(xx_ref[...], nw))
        xn_ref[0:tm, :] = _rms(x_ref[...], nw).astype(BF16)
        xn_ref[tm:, :] = jnp.concatenate([nxt, prev], axis=0).astype(BF16)
        acc_ref[...] = jnp.zeros_like(acc_ref)

    xe = xn_ref[...]

    def conv(w_ref, h_ref, cw_ref, cb_ref):
        hid = _dot(xe, w_ref[0])
        h_ref[8:, :] = hid
        h_ref[0:8, :] = hid[tm + 8:, :]
        cw = cw_ref[...]
        return (h_ref[pl.ds(7, tm), :] * cw[0:1] + h_ref[pl.ds(8, tm), :] * cw[1:2]
                + h_ref[pl.ds(9, tm), :] * cw[2:3] + cb_ref[...])

    g = conv(wg_ref, hg_ref, cwg_ref, cbg_ref)
    u = conv(wu_ref, hu_ref, cwu_ref, cbu_ref)
    c0 = math.sqrt(2.0 / math.pi)
    gelu = 0.5 * g * (1.0 + jnp.tanh(c0 * (g + 0.044715 * (g * g * g))))
    acc_ref[...] += _dot((gelu * u).astype(BF16), wo_ref[...])

    @pl.when(j == pl.num_programs(1) - 1)
    def _():
        o_ref[...] = x_ref[...] + _rms(acc_ref[...], nw2_ref[...])


def conv_ffn(h, nw_in, w_in, conv_w, conv_b, w_out, nw_out, seq_len, tm=ROW_TILE, tf=FF_TILE):
    t, d = h.shape
    f = w_out.shape[0]
    nf = f // tf
    per_seq = seq_len // tm
    r8 = tm // 8
    last8 = t // 8 - 1
    cb = conv_b.reshape(1, 2 * f)
    w_in = _column_tiles(w_in, tf)
    return pl.pallas_call(
        functools.partial(_ffn_kernel, tm=tm, per_seq=per_seq),
        grid=(t // tm, nf),
        in_specs=[
            pl.BlockSpec((tm, d), lambda i, j: (i, 0)),
            pl.BlockSpec((8, d), lambda i, j: (jnp.maximum(i * r8 - 1, 0), 0)),
            pl.BlockSpec((8, d), lambda i, j: (jnp.minimum((i + 1) * r8, last8), 0)),
            pl.BlockSpec((1, d), lambda i, j: (0, 0)),
            pl.BlockSpec((1, d, tf), lambda i, j: (j, 0, 0)),
            pl.BlockSpec((1, d, tf), lambda i, j: (j + nf, 0, 0)),
            pl.BlockSpec((FFN_CONV_W, tf), lambda i, j: (0, j)),
            pl.BlockSpec((FFN_CONV_W, tf), lambda i, j: (0, j + nf)),
            pl.BlockSpec((1, tf), lambda i, j: (0, j)),
            pl.BlockSpec((1, tf), lambda i, j: (0, j + nf)),
            pl.BlockSpec((tf, d), lambda i, j: (j, 0)),
            pl.BlockSpec((1, d), lambda i, j: (0, 0)),
        ],
        out_specs=pl.BlockSpec((tm, d), lambda i, j: (i, 0)),
        out_shape=jax.ShapeDtypeStruct((t, d), F32),
        scratch_shapes=[
            pltpu.VMEM((tm + SUBLANES_BF16, d), BF16),
            pltpu.VMEM((8 + tm + SUBLANES_BF16, tf), F32),
            pltpu.VMEM((8 + tm + SUBLANES_BF16, tf), F32),
            pltpu.VMEM((tm, d), F32),
        ],
        compiler_params=_params("parallel", "arbitrary"),
        name="conv_ffn",
    )(h, h, h, nw_in.reshape(1, d), w_in, w_in, conv_w, conv_w, cb, cb, w_out, nw_out.reshape(1, d))


def _diff_attn_kernel(lp_ref, sub_ref, q_ref, k_ref, v_ref, o_ref, *, lambda_init):
    lp = lp_ref[...]
    lam = (jnp.exp(jnp.sum(lp[0:1] * lp[1:2], axis=-1, keepdims=True))
           - jnp.exp(jnp.sum(lp[2:3] * lp[3:4], axis=-1, keepdims=True)) + lambda_init)
    sub = sub_ref[...]
    n_sub = q_ref.shape[1] // ATT_SUB_ROWS

    def scores(i):
        q = q_ref[0, i * ATT_SUB_ROWS:(i + 1) * ATT_SUB_ROWS, :].astype(F32)
        q = (q * (HEAD_DIM ** -0.5 * math.log2(math.e))).astype(BF16)
        return [_dot_nt(q[:, lo:lo + HEAD_DIM], k_ref[0, :, lo:lo + HEAD_DIM]) for lo in (0, HEAD_DIM)]

    def weights(s12):
        es = [jnp.exp2(s - jnp.max(s, axis=-1, keepdims=True)) for s in s12]
        l1, l2 = [jnp.sum(e, axis=-1, keepdims=True) for e in es]
        return (es[0] - es[1] * (lam * l1 / l2)).astype(BF16), 1.0 / l1

    def finish(i, a_inv):
        a, inv_l1 = a_inv
        o = _dot(a, v_ref[0]) * inv_l1
        o_ref[0, i * ATT_SUB_ROWS:(i + 1) * ATT_SUB_ROWS, :] = (_rms(o, sub) * (1.0 - lambda_init)).astype(o_ref.dtype)

    s_next = scores(0)
    a_prev = None
    for i in range(n_sub):
        s_cur = s_next
        if i + 1 < n_sub:
            s_next = scores(i + 1)
        a_cur = weights(s_cur)
        if a_prev is not None:
            finish(i - 1, a_prev)
        a_prev = a_cur
    finish(n_sub - 1, a_prev)


def diff_attention(proj, lam_params, subln, layer, tq=ATT_Q_TILE):
    b, s, _ = proj.shape
    w = 2 * HEAD_DIM
    lambda_init = 0.8 - 0.6 * math.exp(-0.3 * layer)
    return pl.pallas_call(
        functools.partial(_diff_attn_kernel, lambda_init=lambda_init),
        grid=(b, HEADS, s // tq),
        in_specs=[
            pl.BlockSpec((4, HEAD_DIM), lambda bi, h, qi: (0, 0)),
            pl.BlockSpec((1, w), lambda bi, h, qi: (0, 0)),
            pl.BlockSpec((1, tq, w), lambda bi, h, qi: (bi, qi, h)),
            pl.BlockSpec((1, s, w), lambda bi, h, qi: (bi, 0, HEADS + h)),
            pl.BlockSpec((1, s, w), lambda bi, h, qi: (bi, 0, 2 * HEADS + h)),
        ],
        out_specs=pl.BlockSpec((1, tq, w), lambda bi, h, qi: (bi, qi, h)),
        out_shape=jax.ShapeDtypeStruct((b, s, C_QK_W), BF16),
        compiler_params=_params("parallel", "parallel", "arbitrary"),
        name="diff_attention",
    )(lam_params, subln.reshape(1, w), proj, proj, proj)


def _na_bias(rpb):
    rows = GRID_W
    sel_r = np.zeros((3, NA_Q_ROWS, NA_K_ROWS, 2 * NA_ROWS - 1), np.float32)
    for case, rb in enumerate((0, 1, rows // NA_Q_ROWS - 1)):
        base = int(np.clip(rb * NA_Q_ROWS - (NA_K_ROWS - NA_Q_ROWS) // 2, 0, rows - NA_K_ROWS))
        for ql in range(NA_Q_ROWS):
            r = rb * NA_Q_ROWS + ql
            row_start = int(np.clip(r - NA_ROWS // 2, 0, rows - NA_ROWS))
            for kl in range(NA_K_ROWS):
                kr = base + kl
                if row_start <= kr < row_start + NA_ROWS:
                    sel_r[case, ql, kl, kr - r + NA_ROWS - 1] = 1.0
    c = np.arange(GRID_W)
    col_start = np.clip(c - NA_COLS // 2, 0, GRID_W - NA_COLS)
    col_ok = (c[None, :] >= col_start[:, None]) & (c[None, :] < col_start[:, None] + NA_COLS)
    dc = np.clip(c[None, :] - c[:, None] + NA_COLS - 1, 0, 2 * NA_COLS - 2)
    sel_c = (np.arange(2 * NA_COLS - 1)[:, None, None] == dc[None]) & col_ok[None]
    valid = (sel_r.sum(-1) > 0)[:, :, None, :, None] & col_ok[None, None, :, None, :]
    t1 = jnp.einsum("hab,cqka->hcqkb", rpb, sel_r, precision=lax.Precision.HIGHEST)
    t2 = jnp.einsum("hcqkb,bxy->hcqxky", t1, sel_c.astype(np.float32), precision=lax.Precision.HIGHEST)
    bias = jnp.where(valid[None], t2 * math.log2(math.e), -jnp.inf)
    return bias.reshape(rpb.shape[0], 3, NA_Q_ROWS * GRID_W, NA_K_ROWS * GRID_W)


def _na_kernel(q_ref, k_ref, v_ref, b_ref, o_ref):
    rb = pl.program_id(2)
    nk = NA_K_ROWS * GRID_W
    lo = (NA_K_ROWS - NA_Q_ROWS) // 2 * GRID_W
    start = jnp.clip(rb * (NA_Q_ROWS * GRID_W) - lo, 0, k_ref.shape[1] - nk)
    start = pl.multiple_of(start, lo)
    q = (q_ref[0].astype(F32) * (HEAD_DIM ** -0.5 * math.log2(math.e))).astype(BF16)
    s = _dot_nt(q, k_ref[0, pl.ds(start, nk), :]) + b_ref[0, 0]
    e = jnp.exp2(s - jnp.max(s, axis=-1, keepdims=True))
    inv_l = 1.0 / jnp.sum(e, axis=-1, keepdims=True)
    o_ref[0] = (_dot(e.astype(BF16), v_ref[0, pl.ds(start, nk), :]) * inv_l).astype(o_ref.dtype)


def neighbourhood_attention(proj, col0, bias):
    b, s, _ = proj.shape
    nq = NA_Q_ROWS * GRID_W
    nblk = s // nq
    c0 = col0 // HEAD_DIM

    def bias_map(bi, h, rb):
        return (h, jnp.where(rb == 0, 0, jnp.where(rb == nblk - 1, 2, 1)), 0, 0)

    return pl.pallas_call(
        _na_kernel,
        grid=(b, HEADS, nblk),
        in_specs=[
            pl.BlockSpec((1, nq, HEAD_DIM), lambda bi, h, rb: (bi, rb, c0 + h)),
            pl.BlockSpec((1, s, HEAD_DIM), lambda bi, h, rb: (bi, 0, c0 + HEADS + h)),
            pl.BlockSpec((1, s, HEAD_DIM), lambda bi, h, rb: (bi, 0, c0 + 2 * HEADS + h)),
            pl.BlockSpec((1, 1) + bias.shape[2:], bias_map),
        ],
        out_specs=pl.BlockSpec((1, nq, HEAD_DIM), lambda bi, h, rb: (bi, rb, h)),
        out_shape=jax.ShapeDtypeStruct((b, s, B_W), BF16),
        compiler_params=_params("parallel", "parallel", "arbitrary"),
        name="neighbourhood_attention",
    )(proj, proj, proj, bias)


HALO = SUBLANES_BF16


def _split3(x):
    hi = x.astype(BF16)
    r1 = x - hi.astype(F32)
    mid = r1.astype(BF16)
    lo = (r1 - mid.astype(F32)).astype(BF16)
    return hi, mid, lo


def _dn_prep_kernel(x_ref, xp_ref, xx_ref, cw_ref, gt_ref, alog_ref, dtb_ref,
                    q_ref, k_ref, v_ref, gc_ref, beta_ref, xs_ref):
    i = pl.program_id(1)
    tb = x_ref.shape[1]
    pad = A_CONV_W // 2
    xs_ref[0:HALO, :] = jnp.where(i == 0, 0.0, xp_ref[0].astype(F32))
    xs_ref[HALO:HALO + tb, :] = x_ref[0].astype(F32)
    xs_ref[HALO + tb:, :] = jnp.where(i == pl.num_programs(1) - 1, 0.0, xx_ref[0].astype(F32))
    cw = cw_ref[...]
    y = xs_ref[pl.ds(HALO - pad, tb), :] * cw[0:1]
    for t in range(1, A_CONV_W):
        y = y + xs_ref[pl.ds(HALO - pad + t, tb), :] * cw[t:t + 1]
    y = y * _sigmoid(y)
    for h in range(HEADS):
        qh = y[:, h * HEAD_DIM:(h + 1) * HEAD_DIM]
        kh = y[:, (HEADS + h) * HEAD_DIM:(HEADS + h + 1) * HEAD_DIM]
        qn = qh * (lax.rsqrt(jnp.sum(qh * qh, axis=-1, keepdims=True) + 1e-6) * (HEAD_DIM ** -0.5))
        kn = kh * lax.rsqrt(jnp.sum(kh * kh, axis=-1, keepdims=True) + 1e-6)
        q_ref[0, h] = qn.astype(q_ref.dtype)
        k_ref[0, h] = kn.astype(k_ref.dtype)
        v_ref[0, h] = y[:, (2 * HEADS + h) * HEAD_DIM:(2 * HEADS + h + 1) * HEAD_DIM].astype(v_ref.dtype)

    nd = 2 * HEADS
    gates = gt_ref[0]
    z = gates[:, 0:nd] + dtb_ref[...]
    softplus = jnp.maximum(z, 0.0) + jnp.log(1.0 + jnp.exp(-jnp.abs(z)))
    g = -jnp.exp(alog_ref[...]) * softplus
    beta_ref[0] = _sigmoid(gates[:, nd:2 * nd])
    r = lax.broadcasted_iota(jnp.int32, (tb, tb), 0)
    c = lax.broadcasted_iota(jnp.int32, (tb, tb), 1)
    lower = jnp.where(r >= c, 1.0, 0.0).astype(BF16)
    upper = jnp.where(r <= c, 1.0, 0.0).astype(BF16)
    parts = _split3(g)
    fwd = _dot(lower, parts[0]) + _dot(lower, parts[1]) + _dot(lower, parts[2])
    bwd = _dot(upper, parts[0]) + _dot(upper, parts[1]) + _dot(upper, parts[2])
    col = lax.broadcasted_iota(jnp.int32, (tb, nd), 1)
    gc_ref[0] = jnp.where(col < HEADS, fwd, bwd)


def deltanet_prep(proj, gates, conv_w, a_log, dt_bias, tb=DN_BLOCK):
    b, s, _ = proj.shape
    nb = s // tb
    rh = tb // HALO
    nd = 2 * HEADS
    hs = jax.ShapeDtypeStruct((b, HEADS, s, HEAD_DIM), BF16)
    head_spec = pl.BlockSpec((1, HEADS, tb, HEAD_DIM), lambda bi, i: (bi, 0, i, 0))
    vec_spec = pl.BlockSpec((1, tb, nd), lambda bi, i: (bi, i, 0))
    return pl.pallas_call(
        _dn_prep_kernel,
        grid=(b, nb),
        in_specs=[
            pl.BlockSpec((1, tb, A_QKV_W), lambda bi, i: (bi, i, 0)),
            pl.BlockSpec((1, HALO, A_QKV_W), lambda bi, i: (bi, jnp.maximum(i * rh - 1, 0), 0)),
            pl.BlockSpec((1, HALO, A_QKV_W), lambda bi, i: (bi, jnp.minimum((i + 1) * rh, s // HALO - 1), 0)),
            pl.BlockSpec((A_CONV_W, A_QKV_W), lambda bi, i: (0, 0)),
            pl.BlockSpec((1, tb, LANES), lambda bi, i: (bi, i, 0)),
            pl.BlockSpec((1, nd), lambda bi, i: (0, 0)),
            pl.BlockSpec((1, nd), lambda bi, i: (0, 0)),
        ],
        out_specs=[head_spec, head_spec, head_spec, vec_spec, vec_spec],
        out_shape=[hs, hs, hs, jax.ShapeDtypeStruct((b, s, nd), F32), jax.ShapeDtypeStruct((b, s, nd), F32)],
        scratch_shapes=[pltpu.VMEM((tb + 2 * HALO, A_QKV_W), F32)],
        compiler_params=_params("parallel", "arbitrary"),
        name="deltanet_prep",
    )(proj, proj, proj, conv_w, gates, a_log.reshape(1, nd), dt_bias.reshape(1, nd))


def _split2(x):
    hi = x.astype(BF16)
    return hi, (x - hi.astype(F32)).astype(BF16)


def _dot3(a, b):
    lhs = jnp.concatenate([a[0], a[0], a[1]], axis=1)
    rhs = jnp.concatenate([b[0], b[1], b[0]], axis=0)
    return _dot(lhs, rhs)


def _unit_triangular_solve(ms, rhss, same_block):
    n = ms[0].shape[0]
    grp = DN_GROUP
    ng = n // grp
    r = lax.broadcasted_iota(jnp.int32, (grp, grp), 0)
    c = lax.broadcasted_iota(jnp.int32, (grp, grp), 1)
    eye = jnp.where(r == c, 1.0, 0.0)
    base_mask = (r // DN_BASE) == (c // DN_BASE)
    mds = [jnp.where(base_mask, m[g * grp:(g + 1) * grp, g * grp:(g + 1) * grp], 0.0)
           for m in ms for g in range(ng)]
    pws = [_split2(md) for md in mds]
    ts = [eye + md for md in mds]
    size = 2
    while size < DN_BASE:
        pws = [_split2(_dot3(pw, pw)) for pw in pws]
        ts = [t + _dot3(pw, _split2(t)) for pw, t in zip(pws, ts)]
        size *= 2
    t_bases = [t.astype(BF16) for t in ts]
    offs = {}
    blk = 2 * DN_BASE
    while blk <= n:
        offs[blk] = [jnp.where(same_block(blk), jnp.where(same_block(blk // 2), 0.0, m), 0.0).astype(BF16)
                     for m in ms]
        blk *= 2

    def apply(blk, ys):
        if blk == DN_BASE:
            ybs = [y.astype(BF16) for y in ys]
            return [jnp.concatenate([_dot(t_bases[i * ng + g], yb[g * grp:(g + 1) * grp]) for g in range(ng)],
                                    axis=0) for i, yb in enumerate(ybs)]
        zs = apply(blk // 2, ys)
        corr = apply(blk // 2, [_dot(off, z.astype(BF16)) for off, z in zip(offs[blk], zs)])
        return [z + cr for z, cr in zip(zs, corr)]

    return apply(n, rhss)


def _dn_block(chains):
    n = chains[0][0].shape[0]
    r = lax.broadcasted_iota(jnp.int32, (n, n), 0)
    c = lax.broadcasted_iota(jnp.int32, (n, n), 1)

    def same_block(size):
        return (r // size) == (c // size)

    ms, rhss, pre = [], [], []
    for q, k, v, gcol, grow, beta, state_ref, forward in chains:
        d = (r - c) if forward else (c - r)
        kf = k.astype(F32)
        kb = kf * beta
        decay = jnp.exp(jnp.where(d >= 0, gcol - grow, NEG_BIG))
        ms.append(jnp.where(d > 0, -(_dot_nt(kb.astype(BF16), k) * decay), 0.0))
        intra = (_dot_nt(q, k) * decay).astype(BF16)
        eg = jnp.exp(gcol)
        rhss.append(jnp.concatenate([v.astype(F32) * beta, kb * eg], axis=1))
        g_last = grow[:, n - 1:n] if forward else grow[:, 0:1]
        pre.append((intra, (q.astype(F32) * eg).astype(BF16), (kf * jnp.exp(g_last - gcol)).astype(BF16),
                    jnp.exp(g_last)))
    sols = _unit_triangular_solve(ms, rhss, same_block)
    outs = []
    for chain, sol, (intra, q_dec, k_dec, blk_decay) in zip(chains, sols, pre):
        state_ref = chain[6]
        state = state_ref[...]
        sb = state.astype(BF16)
        v_new = sol[:, :HEAD_DIM] - _dot(sol[:, HEAD_DIM:].astype(BF16), sb)
        vb = v_new.astype(BF16)
        outs.append(_dot(q_dec, sb) + _dot(intra, vb))
        state_ref[...] = state * blk_decay + _dot_tn(k_dec, vb)
    return outs


def _dn_kernel(qf_ref, kf_ref, vf_ref, qb_ref, kb_ref, vb_ref, gcf_ref, gcb_ref, btf_ref, btb_ref,
               grf_ref, grb_ref, of_ref, ob_ref, state_ref):
    hb = qf_ref.shape[1]

    @pl.when(pl.program_id(2) == 0)
    def _():
        state_ref[...] = jnp.zeros_like(state_ref)

    lane = lax.broadcasted_iota(jnp.int32, gcf_ref.shape[1:], 1)

    def pick(ref, idx):
        return jnp.sum(jnp.where(lane == idx, ref[0], 0.0), axis=-1, keepdims=True)

    chains = []
    for hl in range(hb):
        h = pl.program_id(1) * hb + hl
        chains.append((qf_ref[0, hl], kf_ref[0, hl], vf_ref[0, hl], pick(gcf_ref, h),
                       grf_ref[0, pl.ds(h, 1), :], pick(btf_ref, h), state_ref.at[0, hl], True))
        chains.append((qb_ref[0, hl], kb_ref[0, hl], vb_ref[0, hl], pick(gcb_ref, HEADS + h),
                       grb_ref[0, pl.ds(HEADS + h, 1), :], pick(btb_ref, HEADS + h), state_ref.at[1, hl], False))
    outs = _dn_block(chains)
    for hl in range(hb):
        of_ref[0, hl] = outs[2 * hl]
        ob_ref[0, hl] = outs[2 * hl + 1]


def deltanet_scan(q, k, v, gc, beta, gc_rows, tb=DN_BLOCK, hb=DN_HEADS_PER_STEP):
    b, _, s, _ = q.shape
    nb = s // tb
    nd = 2 * HEADS
    fwd = pl.BlockSpec((1, hb, tb, HEAD_DIM), lambda bi, h, c: (bi, h, c, 0))
    bwd = pl.BlockSpec((1, hb, tb, HEAD_DIM), lambda bi, h, c: (bi, h, nb - 1 - c, 0))
    vec_f = pl.BlockSpec((1, tb, nd), lambda bi, h, c: (bi, c, 0))
    vec_b = pl.BlockSpec((1, tb, nd), lambda bi, h, c: (bi, nb - 1 - c, 0))
    row_f = pl.BlockSpec((1, nd, tb), lambda bi, h, c: (bi, 0, c))
    row_b = pl.BlockSpec((1, nd, tb), lambda bi, h, c: (bi, 0, nb - 1 - c))
    os_ = jax.ShapeDtypeStruct((b, HEADS, s, HEAD_DIM), F32)
    return pl.pallas_call(
        _dn_kernel,
        grid=(b, HEADS // hb, nb),
        in_specs=[fwd, fwd, fwd, bwd, bwd, bwd, vec_f, vec_b, vec_f, vec_b, row_f, row_b],
        out_specs=[fwd, bwd],
        out_shape=[os_, os_],
        scratch_shapes=[pltpu.VMEM((2, hb, HEAD_DIM, HEAD_DIM), F32)],
        compiler_params=_params("parallel", "parallel", "arbitrary"),
        name="deltanet_scan",
    )(q, k, v, q, k, v, gc, gc, beta, beta, gc_rows, gc_rows)


def _dn_out_kernel(of_ref, ob_ref, z_ref, nw_ref, o_ref):
    nw = nw_ref[...]
    for h in range(HEADS):
        o = _rms(of_ref[0, h] + ob_ref[0, h], nw)
        z = z_ref[0, :, h * HEAD_DIM:(h + 1) * HEAD_DIM].astype(F32)
        o_ref[0, :, h * HEAD_DIM:(h + 1) * HEAD_DIM] = (o * (z * _sigmoid(z))).astype(o_ref.dtype)


def deltanet_out(o_f, o_b, proj, z_col0, out_norm, ts=ROW_TILE):
    b, _, s, _ = o_f.shape
    head_spec = pl.BlockSpec((1, HEADS, ts, HEAD_DIM), lambda bi, i: (bi, 0, i, 0))
    zb = z_col0 // A_V_W
    return pl.pallas_call(
        _dn_out_kernel,
        grid=(b, s // ts),
        in_specs=[head_spec, head_spec,
                  pl.BlockSpec((1, ts, A_V_W), lambda bi, i: (bi, i, zb)),
                  pl.BlockSpec((1, HEAD_DIM), lambda bi, i: (0, 0))],
        out_specs=pl.BlockSpec((1, ts, A_V_W), lambda bi, i: (bi, i, 0)),
        out_shape=jax.ShapeDtypeStruct((b, s, A_V_W), BF16),
        compiler_params=_params("parallel", "parallel"),
        name="deltanet_out",
    )(o_f, o_b, proj, out_norm.reshape(1, HEAD_DIM))


def _rope_tables(s):
    half = ROPE_DIMS // 2
    inv = ROPE_THETA ** (-jnp.arange(0, ROPE_DIMS, 2, dtype=F32) / ROPE_DIMS)
    ang = jnp.arange(s, dtype=F32)[:, None] * inv[None, :]
    cos, sin = jnp.cos(ang), jnp.sin(ang)
    rest = HEAD_DIM - ROPE_DIMS
    c = jnp.concatenate([cos, cos, jnp.ones((s, rest), F32)], axis=1)
    s1 = jnp.concatenate([jnp.zeros((s, half), F32), sin, jnp.zeros((s, rest), F32)], axis=1)
    s2 = jnp.concatenate([-sin, jnp.zeros((s, half + rest), F32)], axis=1)
    return c, s1, s2


def _trunk(x, p, ab_w_in, ab_conv_w, ab_a_log, ab_dt_bias, ab_out_norm, ab_rpb, ab_w_out,
           c_w_in, c_lambda, c_subln, c_w_out, norms, ffn_w_in, ffn_conv_w, ffn_conv_b,
           ffn_w_out, ple_w_proj, ple_w_gate):
    b, s, d = x.shape
    t = b * s
    depth = norms.shape[0]
    h = x.reshape(t, d)
    rope = _rope_tables(s)
    o1 = A_QKV_W
    o2 = o1 + A_V_W
    o3 = o2 + 4 * HEADS
    for layer in range(depth):
        j = layer // 2
        if layer % 2 == 0:
            w_in = ab_w_in[j]
            w_main = jnp.concatenate([w_in[:, :o2], w_in[:, o3:]], axis=1)
            w_gate = jnp.pad(w_in[:, o2:o3], ((0, 0), (0, LANES - 4 * HEADS)))
            proj = norm_matmul(h, norms[layer, 0], w_main, BF16).reshape(b, s, -1)
            gates = norm_matmul(h, norms[layer, 0], w_gate, F32).reshape(b, s, LANES)
            qa, ka, va, gc, beta = deltanet_prep(proj, gates, ab_conv_w[j], ab_a_log[j], ab_dt_bias[j])
            o_f, o_b = deltanet_scan(qa, ka, va, gc, beta, jnp.transpose(gc, (0, 2, 1)))
            o_a = deltanet_out(o_f, o_b, proj, o1, ab_out_norm[j])
            o_nb = neighbourhood_attention(proj, o2, _na_bias(ab_rpb[j]))
            w_out = ab_w_out[j].astype(BF16)
            h = proj_norm_res([o_a.reshape(t, -1), o_nb.reshape(t, -1)], [w_out[:A_V_W], w_out[A_V_W:]],
                              norms[layer, 1], h)
        else:
            proj = norm_matmul(h, norms[layer, 0], c_w_in[j], BF16, rope=rope,
                               rope_tiles=2 * C_QK_W // COL_TILE, seq_len=s).reshape(b, s, -1)
            o_c = diff_attention(proj, c_lambda[j], c_subln[j], layer)
            h = proj_norm_res([o_c.reshape(t, -1)], [c_w_out[j].astype(BF16)], norms[layer, 1], h)
        h = conv_ffn(h, norms[layer, 2], ffn_w_in[layer], ffn_conv_w[layer], ffn_conv_b[layer],
                     ffn_w_out[layer].astype(BF16), norms[layer, 3], s)
        h = ple(h, p[layer].reshape(t, -1), ple_w_gate[layer].astype(BF16), ple_w_proj[layer].astype(BF16))
    return h.reshape(b, s, d)


def kernel(x_prompt, x_sample, p_prompt, p_sample, ab_w_in, ab_conv_w, ab_a_log, ab_dt_bias, ab_out_norm,
           ab_rpb, ab_w_out, c_w_in, c_lambda, c_subln, c_w_out, norms, ffn_w_in, ffn_conv_w, ffn_conv_b,
           ffn_w_out, ple_w_proj, ple_w_gate):
    nb = x_prompt.shape[0]
    x = jnp.concatenate([x_prompt, x_sample], axis=0)
    p = jnp.concatenate([p_prompt, p_sample], axis=1)
    y = _trunk(x, p, ab_w_in, ab_conv_w, ab_a_log, ab_dt_bias, ab_out_norm, ab_rpb, ab_w_out,
               c_w_in, c_lambda, c_subln, c_w_out, norms, ffn_w_in, ffn_conv_w, ffn_conv_b,
               ffn_w_out, ple_w_proj, ple_w_gate)
    return (y[:nb], y[nb:])
```

```python
import functools
import math

import jax
import jax.numpy as jnp
import numpy as np
from jax import lax
from jax.experimental import pallas as pl
from jax.experimental.pallas import tpu as pltpu

F32 = jnp.float32
BF16 = jnp.bfloat16

D_MODEL = 2048
PLE_DIM = 256
GRID_W = 64
RMS_EPS = 1e-6
HEADS = 8
HEAD_DIM = 128
A_CONV_W = 5
NA_ROWS = 8
NA_COLS = 16
ROPE_THETA = 500000.0
ROPE_DIMS = HEAD_DIM // 4
D_FF = 8192
FFN_CONV_W = 3
A_QKV_W = 3 * HEADS * HEAD_DIM
A_V_W = HEADS * HEAD_DIM
B_W = HEADS * HEAD_DIM
C_QK_W = HEADS * 2 * HEAD_DIM

LANES = 128
SUBLANES_BF16 = 16
VMEM_LIMIT = 56 * 1024 * 1024
NEG_BIG = -1e30

ROW_TILE = 512
MM_ROW_TILE = 1024
COL_TILE = 512
FF_TILE = 512
DN_BLOCK = 256
DN_BASE = 64
DN_GROUP = 128
DN_HEADS_PER_STEP = 4
ATT_Q_TILE = 512
ATT_SUB_ROWS = 256
NA_Q_ROWS = 8
NA_SUB_Q_ROWS = 4
NA_SUB_K_ROWS = 12


def _params(*sem):
    return pltpu.CompilerParams(dimension_semantics=sem, vmem_limit_bytes=VMEM_LIMIT)


def _rms(x, w):
    return x * lax.rsqrt(jnp.mean(x * x, axis=-1, keepdims=True) + RMS_EPS) * w


def _dot(a, b):
    return jnp.dot(a, b, preferred_element_type=F32)


def _dot_nt(a, b):
    return lax.dot_general(a, b, (((1,), (1,)), ((), ())), preferred_element_type=F32)


def _dot_tn(a, b):
    return lax.dot_general(a, b, (((0,), (0,)), ((), ())), preferred_element_type=F32)


def _sigmoid(x):
    return 1.0 / (1.0 + jnp.exp(-x))


def _norm_matmul_kernel(*refs, rope_tiles):
    if rope_tiles:
        x_ref, nw_ref, w_ref, rc_ref, rs1_ref, rs2_ref, o_ref, xn_ref = refs
    else:
        x_ref, nw_ref, w_ref, o_ref, xn_ref = refs
    j = pl.program_id(1)

    @pl.when(j == 0)
    def _():
        xn_ref[...] = _rms(x_ref[...], nw_ref[...]).astype(BF16)

    y = _dot(xn_ref[...], w_ref[0])
    if not rope_tiles:
        o_ref[...] = y.astype(o_ref.dtype)
        return

    @pl.when(j < rope_tiles)
    def _():
        n = y.shape[1]
        reps = n // LANES
        c = jnp.concatenate([rc_ref[...]] * reps, axis=1)
        s1 = jnp.concatenate([rs1_ref[...]] * reps, axis=1)
        s2 = jnp.concatenate([rs2_ref[...]] * reps, axis=1)
        half = ROPE_DIMS // 2
        r = y * c + pltpu.roll(y, half, 1) * s1 + pltpu.roll(y, n - half, 1) * s2
        o_ref[...] = r.astype(o_ref.dtype)

    @pl.when(j >= rope_tiles)
    def _():
        o_ref[...] = y.astype(o_ref.dtype)


def _column_tiles(w, tn):
    k, n = w.shape
    return jnp.transpose(w.reshape(k, n // tn, tn), (1, 0, 2)).astype(BF16)


def norm_matmul(x, nw, w, out_dtype, rope=None, rope_tiles=0, seq_len=None, tm=MM_ROW_TILE, tn=COL_TILE):
    t, k = x.shape
    n = w.shape[1]
    tn = min(tn, n)
    tm = min(tm, t)
    assert t % tm == 0 and n % tn == 0
    in_specs = [
        pl.BlockSpec((tm, k), lambda i, j: (i, 0)),
        pl.BlockSpec((1, k), lambda i, j: (0, 0)),
        pl.BlockSpec((1, k, tn), lambda i, j: (j, 0, 0)),
    ]
    args = [x, nw.reshape(1, k), _column_tiles(w, tn)]
    if rope_tiles:
        per_seq = seq_len // tm
        for tab in rope:
            in_specs.append(pl.BlockSpec((tm, LANES), lambda i, j: (i % per_seq, 0)))
            args.append(tab)
    return pl.pallas_call(
        functools.partial(_norm_matmul_kernel, rope_tiles=rope_tiles),
        grid=(t // tm, n // tn),
        in_specs=in_specs,
        out_specs=pl.BlockSpec((tm, tn), lambda i, j: (i, j)),
        out_shape=jax.ShapeDtypeStruct((t, n), out_dtype),
        scratch_shapes=[pltpu.VMEM((tm, k), BF16)],
        compiler_params=_params("parallel", "arbitrary"),
        name="norm_matmul",
    )(*args)


def _proj_norm_res_kernel(*refs, n_in):
    a_refs = refs[:n_in]
    w_refs = refs[n_in:2 * n_in]
    nw_ref, h_ref, o_ref = refs[2 * n_in:]
    y = _dot(a_refs[0][...], w_refs[0][...])
    for a_ref, w_ref in zip(a_refs[1:], w_refs[1:]):
        y = y + _dot(a_ref[...], w_ref[...])
    o_ref[...] = h_ref[...] + _rms(y, nw_ref[...])


def proj_norm_res(acts, ws, nw, h, tm=ROW_TILE):
    t, n = h.shape
    n_in = len(acts)
    in_specs = [pl.BlockSpec((tm, a.shape[1]), lambda i: (i, 0)) for a in acts]
    in_specs += [pl.BlockSpec(w.shape, lambda i: (0, 0)) for w in ws]
    in_specs += [pl.BlockSpec((1, n), lambda i: (0, 0)), pl.BlockSpec((tm, n), lambda i: (i, 0))]
    return pl.pallas_call(
        functools.partial(_proj_norm_res_kernel, n_in=n_in),
        grid=(t // tm,),
        in_specs=in_specs,
        out_specs=pl.BlockSpec((tm, n), lambda i: (i, 0)),
        out_shape=jax.ShapeDtypeStruct((t, n), F32),
        compiler_params=_params("parallel"),
        name="proj_norm_res",
    )(*acts, *ws, nw.reshape(1, n), h)


def _ple_kernel(h_ref, pa_ref, pb_ref, wg_ref, wp_ref, o_ref, *, first_block, a_blocks):
    h = h_ref[...]
    gate = _sigmoid(_dot(h.astype(BF16), wg_ref[...]))
    p = jnp.where(first_block + pl.program_id(0) < a_blocks, pa_ref[...], pb_ref[...])
    o_ref[...] = h + gate * _dot(p.astype(BF16), wp_ref[...])


def ple(h, p_a, p_b, wg, wp, row0=0, rows=None, tm=ROW_TILE):
    n = h.shape[1]
    rows = h.shape[0] if rows is None else rows
    assert row0 % tm == 0 and rows % tm == 0 and p_a.shape[0] % tm == 0 and p_b.shape[0] % tm == 0
    i0 = row0 // tm
    na, nb = p_a.shape[0] // tm, p_b.shape[0] // tm
    return pl.pallas_call(
        functools.partial(_ple_kernel, first_block=i0, a_blocks=na),
        grid=(rows // tm,),
        in_specs=[
            pl.BlockSpec((tm, n), lambda i: (i0 + i, 0)),
            pl.BlockSpec((tm, p_a.shape[1]), lambda i: (jnp.minimum(i0 + i, na - 1), 0)),
            pl.BlockSpec((tm, p_b.shape[1]), lambda i: (jnp.clip(i0 + i - na, 0, nb - 1), 0)),
            pl.BlockSpec(wg.shape, lambda i: (0, 0)),
            pl.BlockSpec(wp.shape, lambda i: (0, 0)),
        ],
        out_specs=pl.BlockSpec((tm, n), lambda i: (i, 0)),
        out_shape=jax.ShapeDtypeStruct((rows, n), F32),
        compiler_params=_params("parallel"),
        name="ple",
    )(h, p_a, p_b, wg, wp)


def _ffn_kernel(x_ref, xp_ref, xx_ref, nw_ref, wg_ref, wu_ref, cwg_ref, cwu_ref, cbg_ref, cbu_ref,
                wo_ref, nw2_ref, o_ref, xn_ref, hg_ref, hu_ref, acc_ref, *, tm, per_seq):
    i = pl.program_id(0)
    j = pl.program_id(1)

    @pl.when(j == 0)
    def _():
        nw = nw_ref[...]
        pos = i % per_seq
        prev = jnp.where(pos == 0, 0.0, _rms(xp_ref[...], nw))
        nxt = jnp.where(pos == per_seq - 1, 0.0, _rms(xx_ref[...], nw))
        xn_ref[0:tm, :] = _rms(x_ref[...], nw).astype(BF16)
        xn_ref[tm:, :] = jnp.concatenate([nxt, prev], axis=0).astype(BF16)
        acc_ref[...] = jnp.zeros_like(acc_ref)

    xe = xn_ref[...]

    def conv(w_ref, h_ref, cw_ref, cb_ref):
        hid = _dot(xe, w_ref[0])
        h_ref[8:, :] = hid
        h_ref[0:8, :] = hid[tm + 8:, :]
        cw = cw_ref[...]
        return (h_ref[pl.ds(7, tm), :] * cw[0:1] + h_ref[pl.ds(8, tm), :] * cw[1:2]
                + h_ref[pl.ds(9, tm), :] * cw[2:3] + cb_ref[...])

    g = conv(wg_ref, hg_ref, cwg_ref, cbg_ref)
    u = conv(wu_ref, hu_ref, cwu_ref, cbu_ref)
    c0 = math.sqrt(2.0 / math.pi)
    gelu = 0.5 * g * (1.0 + jnp.tanh(c0 * (g + 0.044715 * (g * g * g))))
    acc_ref[...] += _dot((gelu * u).astype(BF16), wo_ref[...])

    @pl.when(j == pl.num_programs(1) - 1)
    def _():
        o_ref[...] = x_ref[...] + _rms(acc_ref[...], nw2_ref[...])


def conv_ffn(h, nw_in, w_in, conv_w, conv_b, w_out, nw_out, seq_len, tm=ROW_TILE, tf=FF_TILE):
    t, d = h.shape
    f = w_out.shape[0]
    nf = f // tf
    per_seq = seq_len // tm
    r8 = tm // 8
    last8 = t // 8 - 1
    cb = conv_b.reshape(1, 2 * f)
    w_in = _column_tiles(w_in, tf)
    return pl.pallas_call(
        functools.partial(_ffn_kernel, tm=tm, per_seq=per_seq),
        grid=(t // tm, nf),
        in_specs=[
            pl.BlockSpec((tm, d), lambda i, j: (i, 0)),
            pl.BlockSpec((8, d), lambda i, j: (jnp.maximum(i * r8 - 1, 0), 0)),
            pl.BlockSpec((8, d), lambda i, j: (jnp.minimum((i + 1) * r8, last8), 0)),
            pl.BlockSpec((1, d), lambda i, j: (0, 0)),
            pl.BlockSpec((1, d, tf), lambda i, j: (j, 0, 0)),
            pl.BlockSpec((1, d, tf), lambda i, j: (j + nf, 0, 0)),
            pl.BlockSpec((FFN_CONV_W, tf), lambda i, j: (0, j)),
            pl.BlockSpec((FFN_CONV_W, tf), lambda i, j: (0, j + nf)),
            pl.BlockSpec((1, tf), lambda i, j: (0, j)),
            pl.BlockSpec((1, tf), lambda i, j: (0, j + nf)),
            pl.BlockSpec((tf, d), lambda i, j: (j, 0)),
            pl.BlockSpec((1, d), lambda i, j: (0, 0)),
        ],
        out_specs=pl.BlockSpec((tm, d), lambda i, j: (i, 0)),
        out_shape=jax.ShapeDtypeStruct((t, d), F32),
        scratch_shapes=[
            pltpu.VMEM((tm + SUBLANES_BF16, d), BF16),
            pltpu.VMEM((8 + tm + SUBLANES_BF16, tf), F32),
            pltpu.VMEM((8 + tm + SUBLANES_BF16, tf), F32),
            pltpu.VMEM((tm, d), F32),
        ],
        compiler_params=_params("parallel", "arbitrary"),
        name="conv_ffn",
    )(h, h, h, nw_in.reshape(1, d), w_in, w_in, conv_w, conv_w, cb, cb, w_out, nw_out.reshape(1, d))


def _diff_attn_kernel(lp_ref, sub_ref, q_ref, k_ref, v_ref, o_ref, *, lambda_init):
    lp = lp_ref[...]
    lam = (jnp.exp(jnp.sum(lp[0:1] * lp[1:2], axis=-1, keepdims=True))
           - jnp.exp(jnp.sum(lp[2:3] * lp[3:4], axis=-1, keepdims=True)) + lambda_init)
    sub = sub_ref[...]
    n_sub = q_ref.shape[1] // ATT_SUB_ROWS

    def scores(i):
        q = q_ref[0, i * ATT_SUB_ROWS:(i + 1) * ATT_SUB_ROWS, :].astype(F32)
        q = (q * (HEAD_DIM ** -0.5 * math.log2(math.e))).astype(BF16)
        return [_dot_nt(q[:, lo:lo + HEAD_DIM], k_ref[0, :, lo:lo + HEAD_DIM]) for lo in (0, HEAD_DIM)]

    def weights(s12):
        es = [jnp.exp2(s - jnp.max(s, axis=-1, keepdims=True)) for s in s12]
        l1, l2 = [jnp.sum(e, axis=-1, keepdims=True) for e in es]
        return (es[0] - es[1] * (lam * l1 / l2)).astype(BF16), 1.0 / l1

    def finish(i, a_inv):
        a, inv_l1 = a_inv
        o = _dot(a, v_ref[0]) * inv_l1
        o_ref[0, i * ATT_SUB_ROWS:(i + 1) * ATT_SUB_ROWS, :] = (_rms(o, sub) * (1.0 - lambda_init)).astype(o_ref.dtype)

    s_next = scores(0)
    a_prev = None
    for i in range(n_sub):
        s_cur = s_next
        if i + 1 < n_sub:
            s_next = scores(i + 1)
        a_cur = weights(s_cur)
        if a_prev is not None:
            finish(i - 1, a_prev)
        a_prev = a_cur
    finish(n_sub - 1, a_prev)


def diff_attention(proj, lam_params, subln, layer, tq=ATT_Q_TILE):
    b, s, _ = proj.shape
    w = 2 * HEAD_DIM
    lambda_init = 0.8 - 0.6 * math.exp(-0.3 * layer)
    return pl.pallas_call(
        functools.partial(_diff_attn_kernel, lambda_init=lambda_init),
        grid=(b, HEADS, s // tq),
        in_specs=[
            pl.BlockSpec((4, HEAD_DIM), lambda bi, h, qi: (0, 0)),
            pl.BlockSpec((1, w), lambda bi, h, qi: (0, 0)),
            pl.BlockSpec((1, tq, w), lambda bi, h, qi: (bi, qi, h)),
            pl.BlockSpec((1, s, w), lambda bi, h, qi: (bi, 0, HEADS + h)),
            pl.BlockSpec((1, s, w), lambda bi, h, qi: (bi, 0, 2 * HEADS + h)),
        ],
        out_specs=pl.BlockSpec((1, tq, w), lambda bi, h, qi: (bi, qi, h)),
        out_shape=jax.ShapeDtypeStruct((b, s, C_QK_W), BF16),
        compiler_params=_params("parallel", "parallel", "arbitrary"),
        name="diff_attention",
    )(lam_params, subln.reshape(1, w), proj, proj, proj)


def _na_key_start_row(first_query_row):
    return (first_query_row - NA_ROWS // 2, 0, GRID_W - NA_SUB_K_ROWS)


def _na_bias(rpb):
    rows = GRID_W
    n_sub = NA_Q_ROWS // NA_SUB_Q_ROWS
    sel_r = np.zeros((3, n_sub, NA_SUB_Q_ROWS, NA_SUB_K_ROWS, 2 * NA_ROWS - 1), np.float32)
    for case, rb in enumerate((0, 1, rows // NA_Q_ROWS - 1)):
        for p in range(n_sub):
            r0 = rb * NA_Q_ROWS + p * NA_SUB_Q_ROWS
            base = int(np.clip(*_na_key_start_row(r0)))
            for ql in range(NA_SUB_Q_ROWS):
                r = r0 + ql
                row_start = int(np.clip(r - NA_ROWS // 2, 0, rows - NA_ROWS))
                assert base <= row_start and row_start + NA_ROWS <= base + NA_SUB_K_ROWS
                for kl in range(NA_SUB_K_ROWS):
                    kr = base + kl
                    if row_start <= kr < row_start + NA_ROWS:
                        sel_r[case, p, ql, kl, kr - r + NA_ROWS - 1] = 1.0
    c = np.arange(GRID_W)
    col_start = np.clip(c - NA_COLS // 2, 0, GRID_W - NA_COLS)
    col_ok = (c[None, :] >= col_start[:, None]) & (c[None, :] < col_start[:, None] + NA_COLS)
    dc = np.clip(c[None, :] - c[:, None] + NA_COLS - 1, 0, 2 * NA_COLS - 2)
    sel_c = (np.arange(2 * NA_COLS - 1)[:, None, None] == dc[None]) & col_ok[None]
    valid = (sel_r.sum(-1) > 0)[:, :, :, None, :, None] & col_ok[None, None, None, :, None, :]
    t1 = jnp.einsum("hab,cpqka->hcpqkb", rpb, sel_r, precision=lax.Precision.HIGHEST)
    t2 = jnp.einsum("hcpqkb,bxy->hcpqxky", t1, sel_c.astype(np.float32), precision=lax.Precision.HIGHEST)
    bias = jnp.where(valid[None], t2 * math.log2(math.e), -jnp.inf)
    return bias.reshape(rpb.shape[0], 3, n_sub, NA_SUB_Q_ROWS * GRID_W, NA_SUB_K_ROWS * GRID_W)


def _na_kernel(q_ref, k_ref, v_ref, b_ref, o_ref):
    rb = pl.program_id(2)
    nq = NA_SUB_Q_ROWS * GRID_W
    nk = NA_SUB_K_ROWS * GRID_W
    n_sub = NA_Q_ROWS // NA_SUB_Q_ROWS

    def key_start(p):
        row, lo, hi = _na_key_start_row(rb * NA_Q_ROWS + p * NA_SUB_Q_ROWS)
        return pl.multiple_of(jnp.clip(row, lo, hi) * GRID_W, NA_SUB_Q_ROWS * GRID_W)

    def scores(p):
        q = (q_ref[0, p * nq:(p + 1) * nq, :].astype(F32) * (HEAD_DIM ** -0.5 * math.log2(math.e))).astype(BF16)
        return _dot_nt(q, k_ref[0, pl.ds(key_start(p), nk), :]) + b_ref[0, 0, p]

    def weights(s):
        e = jnp.exp2(s - jnp.max(s, axis=-1, keepdims=True))
        return e.astype(BF16), 1.0 / jnp.sum(e, axis=-1, keepdims=True)

    def finish(p, e_inv):
        e, inv_l = e_inv
        o = _dot(e, v_ref[0, pl.ds(key_start(p), nk), :]) * inv_l
        o_ref[0, p * nq:(p + 1) * nq, :] = o.astype(o_ref.dtype)

    s_next = scores(0)
    e_prev = None
    for p in range(n_sub):
        s_cur = s_next
        if p + 1 < n_sub:
            s_next = scores(p + 1)
        e_cur = weights(s_cur)
        if e_prev is not None:
            finish(p - 1, e_prev)
        e_prev = e_cur
    finish(n_sub - 1, e_prev)


def neighbourhood_attention(proj, col0, bias):
    b, s, _ = proj.shape
    nq = NA_Q_ROWS * GRID_W
    nblk = s // nq
    c0 = col0 // HEAD_DIM

    def bias_map(bi, h, rb):
        return (h, jnp.where(rb == 0, 0, jnp.where(rb == nblk - 1, 2, 1)), 0, 0, 0)

    return pl.pallas_call(
        _na_kernel,
        grid=(b, HEADS, nblk),
        in_specs=[
            pl.BlockSpec((1, nq, HEAD_DIM), lambda bi, h, rb: (bi, rb, c0 + h)),
            pl.BlockSpec((1, s, HEAD_DIM), lambda bi, h, rb: (bi, 0, c0 + HEADS + h)),
            pl.BlockSpec((1, s, HEAD_DIM), lambda bi, h, rb: (bi, 0, c0 + 2 * HEADS + h)),
            pl.BlockSpec((1, 1) + bias.shape[2:], bias_map),
        ],
        out_specs=pl.BlockSpec((1, nq, HEAD_DIM), lambda bi, h, rb: (bi, rb, h)),
        out_shape=jax.ShapeDtypeStruct((b, s, B_W), BF16),
        compiler_params=_params("parallel", "parallel", "arbitrary"),
        name="neighbourhood_attention",
    )(proj, proj, proj, bias)


HALO = SUBLANES_BF16


def _split3(x):
    hi = x.astype(BF16)
    r1 = x - hi.astype(F32)
    mid = r1.astype(BF16)
    lo = (r1 - mid.astype(F32)).astype(BF16)
    return hi, mid, lo


def _dn_prep_kernel(x_ref, xp_ref, xx_ref, cw_ref, gt_ref, alog_ref, dtb_ref,
                    q_ref, k_ref, v_ref, gc_ref, beta_ref, xs_ref):
    i = pl.program_id(1)
    tb = x_ref.shape[1]
    pad = A_CONV_W // 2
    xs_ref[0:HALO, :] = jnp.where(i == 0, 0.0, xp_ref[0].astype(F32))
    xs_ref[HALO:HALO + tb, :] = x_ref[0].astype(F32)
    xs_ref[HALO + tb:, :] = jnp.where(i == pl.num_programs(1) - 1, 0.0, xx_ref[0].astype(F32))
    cw = cw_ref[...]
    y = xs_ref[pl.ds(HALO - pad, tb), :] * cw[0:1]
    for t in range(1, A_CONV_W):
        y = y + xs_ref[pl.ds(HALO - pad + t, tb), :] * cw[t:t + 1]
    y = y * _sigmoid(y)
    for h in range(HEADS):
        qh = y[:, h * HEAD_DIM:(h + 1) * HEAD_DIM]
        kh = y[:, (HEADS + h) * HEAD_DIM:(HEADS + h + 1) * HEAD_DIM]
        qn = qh * (lax.rsqrt(jnp.sum(qh * qh, axis=-1, keepdims=True) + 1e-6) * (HEAD_DIM ** -0.5))
        kn = kh * lax.rsqrt(jnp.sum(kh * kh, axis=-1, keepdims=True) + 1e-6)
        q_ref[0, h] = qn.astype(q_ref.dtype)
        k_ref[0, h] = kn.astype(k_ref.dtype)
        v_ref[0, h] = y[:, (2 * HEADS + h) * HEAD_DIM:(2 * HEADS + h + 1) * HEAD_DIM].astype(v_ref.dtype)

    nd = 2 * HEADS
    gates = gt_ref[0]
    z = gates[:, 0:nd] + dtb_ref[...]
    softplus = jnp.maximum(z, 0.0) + jnp.log(1.0 + jnp.exp(-jnp.abs(z)))
    g = -jnp.exp(alog_ref[...]) * softplus
    beta_ref[0] = _sigmoid(gates[:, nd:2 * nd])
    r = lax.broadcasted_iota(jnp.int32, (tb, tb), 0)
    c = lax.broadcasted_iota(jnp.int32, (tb, tb), 1)
    lower = jnp.where(r >= c, 1.0, 0.0).astype(BF16)
    upper = jnp.where(r <= c, 1.0, 0.0).astype(BF16)
    parts = _split3(g)
    fwd = _dot(lower, parts[0]) + _dot(lower, parts[1]) + _dot(lower, parts[2])
    bwd = _dot(upper, parts[0]) + _dot(upper, parts[1]) + _dot(upper, parts[2])
    col = lax.broadcasted_iota(jnp.int32, (tb, nd), 1)
    gc_ref[0] = jnp.where(col < HEADS, fwd, bwd)


def deltanet_prep(proj, gates, conv_w, a_log, dt_bias, tb=DN_BLOCK):
    b, s, _ = proj.shape
    nb = s // tb
    rh = tb // HALO
    nd = 2 * HEADS
    hs = jax.ShapeDtypeStruct((b, HEADS, s, HEAD_DIM), BF16)
    head_spec = pl.BlockSpec((1, HEADS, tb, HEAD_DIM), lambda bi, i: (bi, 0, i, 0))
    vec_spec = pl.BlockSpec((1, tb, nd), lambda bi, i: (bi, i, 0))
    return pl.pallas_call(
        _dn_prep_kernel,
        grid=(b, nb),
        in_specs=[
            pl.BlockSpec((1, tb, A_QKV_W), lambda bi, i: (bi, i, 0)),
            pl.BlockSpec((1, HALO, A_QKV_W), lambda bi, i: (bi, jnp.maximum(i * rh - 1, 0), 0)),
            pl.BlockSpec((1, HALO, A_QKV_W), lambda bi, i: (bi, jnp.minimum((i + 1) * rh, s // HALO - 1), 0)),
            pl.BlockSpec((A_CONV_W, A_QKV_W), lambda bi, i: (0, 0)),
            pl.BlockSpec((1, tb, LANES), lambda bi, i: (bi, i, 0)),
            pl.BlockSpec((1, nd), lambda bi, i: (0, 0)),
            pl.BlockSpec((1, nd), lambda bi, i: (0, 0)),
        ],
        out_specs=[head_spec, head_spec, head_spec, vec_spec, vec_spec],
        out_shape=[hs, hs, hs, jax.ShapeDtypeStruct((b, s, nd), F32), jax.ShapeDtypeStruct((b, s, nd), F32)],
        scratch_shapes=[pltpu.VMEM((tb + 2 * HALO, A_QKV_W), F32)],
        compiler_params=_params("parallel", "arbitrary"),
        name="deltanet_prep",
    )(proj, proj, proj, conv_w, gates, a_log.reshape(1, nd), dt_bias.reshape(1, nd))


def _split2(x):
    hi = x.astype(BF16)
    return hi, (x - hi.astype(F32)).astype(BF16)


def _dot3(a, b):
    lhs = jnp.concatenate([a[0], a[0], a[1]], axis=1)
    rhs = jnp.concatenate([b[0], b[1], b[0]], axis=0)
    return _dot(lhs, rhs)


def _unit_triangular_solve(ms, rhss, same_block):
    n = ms[0].shape[0]
    grp = DN_GROUP
    ng = n // grp
    r = lax.broadcasted_iota(jnp.int32, (grp, grp), 0)
    c = lax.broadcasted_iota(jnp.int32, (grp, grp), 1)
    eye = jnp.where(r == c, 1.0, 0.0)
    base_mask = (r // DN_BASE) == (c // DN_BASE)
    mds = [jnp.where(base_mask, m[g * grp:(g + 1) * grp, g * grp:(g + 1) * grp], 0.0)
           for m in ms for g in range(ng)]
    pws = [_split2(md) for md in mds]
    ts = [eye + md for md in mds]
    size = 2
    while size < DN_BASE:
        pws = [_split2(_dot3(pw, pw)) for pw in pws]
        ts = [t + _dot3(pw, _split2(t)) for pw, t in zip(pws, ts)]
        size *= 2
    t_bases = [t.astype(BF16) for t in ts]
    offs = {}
    blk = 2 * DN_BASE
    while blk <= n:
        offs[blk] = [jnp.where(same_block(blk), jnp.where(same_block(blk // 2), 0.0, m), 0.0).astype(BF16)
                     for m in ms]
        blk *= 2

    def apply(blk, ys):
        if blk == DN_BASE:
            ybs = [y.astype(BF16) for y in ys]
            return [jnp.concatenate([_dot(t_bases[i * ng + g], yb[g * grp:(g + 1) * grp]) for g in range(ng)],
                                    axis=0) for i, yb in enumerate(ybs)]
        zs = apply(blk // 2, ys)
        corr = apply(blk // 2, [_dot(off, z.astype(BF16)) for off, z in zip(offs[blk], zs)])
        return [z + cr for z, cr in zip(zs, corr)]

    return apply(n, rhss)


def _dn_block(chains):
    n = chains[0][0].shape[0]
    r = lax.broadcasted_iota(jnp.int32, (n, n), 0)
    c = lax.broadcasted_iota(jnp.int32, (n, n), 1)

    def same_block(size):
        return (r // size) == (c // size)

    ms, rhss, pre = [], [], []
    for q, k, v, gcol, grow, beta, state_ref, forward in chains:
        d = (r - c) if forward else (c - r)
        kf = k.astype(F32)
        kb = kf * beta
        decay = jnp.exp(jnp.where(d >= 0, gcol - grow, NEG_BIG))
        ms.append(jnp.where(d > 0, -(_dot_nt(kb.astype(BF16), k) * decay), 0.0))
        intra = (_dot_nt(q, k) * decay).astype(BF16)
        eg = jnp.exp(gcol)
        rhss.append(jnp.concatenate([v.astype(F32) * beta, kb * eg], axis=1))
        g_last = grow[:, n - 1:n] if forward else grow[:, 0:1]
        pre.append((intra, (q.astype(F32) * eg).astype(BF16), (kf * jnp.exp(g_last - gcol)).astype(BF16),
                    jnp.exp(g_last)))
    sols = _unit_triangular_solve(ms, rhss, same_block)
    outs = []
    for chain, sol, (intra, q_dec, k_dec, blk_decay) in zip(chains, sols, pre):
        state_ref = chain[6]
        state = state_ref[...]
        sb = state.astype(BF16)
        v_new = sol[:, :HEAD_DIM] - _dot(sol[:, HEAD_DIM:].astype(BF16), sb)
        vb = v_new.astype(BF16)
        outs.append(_dot(q_dec, sb) + _dot(intra, vb))
        state_ref[...] = state * blk_decay + _dot_tn(k_dec, vb)
    return outs


def _dn_kernel(qf_ref, kf_ref, vf_ref, qb_ref, kb_ref, vb_ref, gcf_ref, gcb_ref, btf_ref, btb_ref,
               grf_ref, grb_ref, of_ref, ob_ref, state_ref):
    hb = qf_ref.shape[1]

    @pl.when(pl.program_id(2) == 0)
    def _():
        state_ref[...] = jnp.zeros_like(state_ref)

    lane = lax.broadcasted_iota(jnp.int32, gcf_ref.shape[1:], 1)

    def pick(ref, idx):
        return jnp.sum(jnp.where(lane == idx, ref[0], 0.0), axis=-1, keepdims=True)

    chains = []
    for hl in range(hb):
        h = pl.program_id(1) * hb + hl
        chains.append((qf_ref[0, hl], kf_ref[0, hl], vf_ref[0, hl], pick(gcf_ref, h),
                       grf_ref[0, pl.ds(h, 1), :], pick(btf_ref, h), state_ref.at[0, hl], True))
        chains.append((qb_ref[0, hl], kb_ref[0, hl], vb_ref[0, hl], pick(gcb_ref, HEADS + h),
                       grb_ref[0, pl.ds(HEADS + h, 1), :], pick(btb_ref, HEADS + h), state_ref.at[1, hl], False))
    outs = _dn_block(chains)
    for hl in range(hb):
        of_ref[0, hl] = outs[2 * hl]
        ob_ref[0, hl] = outs[2 * hl + 1]


def deltanet_scan(q, k, v, gc, beta, gc_rows, tb=DN_BLOCK, hb=DN_HEADS_PER_STEP):
    b, _, s, _ = q.shape
    nb = s // tb
    nd = 2 * HEADS
    fwd = pl.BlockSpec((1, hb, tb, HEAD_DIM), lambda bi, h, c: (bi, h, c, 0))
    bwd = pl.BlockSpec((1, hb, tb, HEAD_DIM), lambda bi, h, c: (bi, h, nb - 1 - c, 0))
    vec_f = pl.BlockSpec((1, tb, nd), lambda bi, h, c: (bi, c, 0))
    vec_b = pl.BlockSpec((1, tb, nd), lambda bi, h, c: (bi, nb - 1 - c, 0))
    row_f = pl.BlockSpec((1, nd, tb), lambda bi, h, c: (bi, 0, c))
    row_b = pl.BlockSpec((1, nd, tb), lambda bi, h, c: (bi, 0, nb - 1 - c))
    os_ = jax.ShapeDtypeStruct((b, HEADS, s, HEAD_DIM), F32)
    return pl.pallas_call(
        _dn_kernel,
        grid=(b, HEADS // hb, nb),
        in_specs=[fwd, fwd, fwd, bwd, bwd, bwd, vec_f, vec_b, vec_f, vec_b, row_f, row_b],
        out_specs=[fwd, bwd],
        out_shape=[os_, os_],
        scratch_shapes=[pltpu.VMEM((2, hb, HEAD_DIM, HEAD_DIM), F32)],
        compiler_params=_params("parallel", "parallel", "arbitrary"),
        name="deltanet_scan",
    )(q, k, v, q, k, v, gc, gc, beta, beta, gc_rows, gc_rows)


def _dn_out_kernel(of_ref, ob_ref, z_ref, nw_ref, o_ref):
    nw = nw_ref[...]
    for h in range(HEADS):
        o = _rms(of_ref[0, h] + ob_ref[0, h], nw)
        z = z_ref[0, :, h * HEAD_DIM:(h + 1) * HEAD_DIM].astype(F32)
        o_ref[0, :, h * HEAD_DIM:(h + 1) * HEAD_DIM] = (o * (z * _sigmoid(z))).astype(o_ref.dtype)


def deltanet_out(o_f, o_b, proj, z_col0, out_norm, ts=ROW_TILE):
    b, _, s, _ = o_f.shape
    head_spec = pl.BlockSpec((1, HEADS, ts, HEAD_DIM), lambda bi, i: (bi, 0, i, 0))
    zb = z_col0 // A_V_W
    return pl.pallas_call(
        _dn_out_kernel,
        grid=(b, s // ts),
        in_specs=[head_spec, head_spec,
                  pl.BlockSpec((1, ts, A_V_W), lambda bi, i: (bi, i, zb)),
                  pl.BlockSpec((1, HEAD_DIM), lambda bi, i: (0, 0))],
        out_specs=pl.BlockSpec((1, ts, A_V_W), lambda bi, i: (bi, i, 0)),
        out_shape=jax.ShapeDtypeStruct((b, s, A_V_W), BF16),
        compiler_params=_params("parallel", "parallel"),
        name="deltanet_out",
    )(o_f, o_b, proj, out_norm.reshape(1, HEAD_DIM))


def _rope_tables(s):
    half = ROPE_DIMS // 2
    inv = ROPE_THETA ** (-jnp.arange(0, ROPE_DIMS, 2, dtype=F32) / ROPE_DIMS)
    ang = jnp.arange(s, dtype=F32)[:, None] * inv[None, :]
    cos, sin = jnp.cos(ang), jnp.sin(ang)
    rest = HEAD_DIM - ROPE_DIMS
    c = jnp.concatenate([cos, cos, jnp.ones((s, rest), F32)], axis=1)
    s1 = jnp.concatenate([jnp.zeros((s, half), F32), sin, jnp.zeros((s, rest), F32)], axis=1)
    s2 = jnp.concatenate([-sin, jnp.zeros((s, half + rest), F32)], axis=1)
    return c, s1, s2


def _trunk(x_a, x_b, p_a, p_b, ab_w_in, ab_conv_w, ab_a_log, ab_dt_bias, ab_out_norm, ab_rpb, ab_w_out,
           c_w_in, c_lambda, c_subln, c_w_out, norms, ffn_w_in, ffn_conv_w, ffn_conv_b,
           ffn_w_out, ple_w_proj, ple_w_gate):
    b_a, s, d = x_a.shape
    b = b_a + x_b.shape[0]
    t, t_a = b * s, b_a * s
    depth = norms.shape[0]
    h = jnp.concatenate([x_a, x_b], axis=0).reshape(t, d)
    rope = _rope_tables(s)
    o1 = A_QKV_W
    o2 = o1 + A_V_W
    o3 = o2 + 4 * HEADS
    for layer in range(depth):
        j = layer // 2
        if layer % 2 == 0:
            w_in = ab_w_in[j]
            w_main = jnp.concatenate([w_in[:, :o2], w_in[:, o3:]], axis=1)
            w_gate = jnp.pad(w_in[:, o2:o3], ((0, 0), (0, LANES - 4 * HEADS)))
            proj = norm_matmul(h, norms[layer, 0], w_main, BF16).reshape(b, s, -1)
            gates = norm_matmul(h, norms[layer, 0], w_gate, F32).reshape(b, s, LANES)
            qa, ka, va, gc, beta = deltanet_prep(proj, gates, ab_conv_w[j], ab_a_log[j], ab_dt_bias[j])
            o_f, o_b = deltanet_scan(qa, ka, va, gc, beta, jnp.transpose(gc, (0, 2, 1)))
            o_a = deltanet_out(o_f, o_b, proj, o1, ab_out_norm[j])
            o_nb = neighbourhood_attention(proj, o2, _na_bias(ab_rpb[j]))
            w_out = ab_w_out[j].astype(BF16)
            h = proj_norm_res([o_a.reshape(t, -1), o_nb.reshape(t, -1)], [w_out[:A_V_W], w_out[A_V_W:]],
                              norms[layer, 1], h)
        else:
            proj = norm_matmul(h, norms[layer, 0], c_w_in[j], BF16, rope=rope,
                               rope_tiles=2 * C_QK_W // COL_TILE, seq_len=s).reshape(b, s, -1)
            o_c = diff_attention(proj, c_lambda[j], c_subln[j], layer)
            h = proj_norm_res([o_c.reshape(t, -1)], [c_w_out[j].astype(BF16)], norms[layer, 1], h)
        h = conv_ffn(h, norms[layer, 2], ffn_w_in[layer], ffn_conv_w[layer], ffn_conv_b[layer],
                     ffn_w_out[layer].astype(BF16), norms[layer, 3], s)
        ple_args = (p_a[layer].reshape(t_a, -1), p_b[layer].reshape(t - t_a, -1),
                    ple_w_gate[layer].astype(BF16), ple_w_proj[layer].astype(BF16))
        if layer + 1 < depth:
            h = ple(h, *ple_args)
    y_a = ple(h, *ple_args, row0=0, rows=t_a)
    y_b = ple(h, *ple_args, row0=t_a, rows=t - t_a)
    return y_a.reshape(b_a, s, d), y_b.reshape(b - b_a, s, d)


def kernel(x_prompt, x_sample, p_prompt, p_sample, ab_w_in, ab_conv_w, ab_a_log, ab_dt_bias, ab_out_norm,
           ab_rpb, ab_w_out, c_w_in, c_lambda, c_subln, c_w_out, norms, ffn_w_in, ffn_conv_w, ffn_conv_b,
           ffn_w_out, ple_w_proj, ple_w_gate):
    return _trunk(x_prompt, x_sample, p_prompt, p_sample, ab_w_in, ab_conv_w, ab_a_log, ab_dt_bias,
                  ab_out_norm, ab_rpb, ab_w_out, c_w_in, c_lambda, c_subln, c_w_out, norms, ffn_w_in,
                  ffn_conv_w, ffn_conv_b, ffn_w_out, ple_w_proj, ple_w_gate)
```

```python
import functools
import math

import jax
import jax.numpy as jnp
import numpy as np
from jax import lax
from jax.experimental import pallas as pl
from jax.experimental.pallas import tpu as pltpu

F32 = jnp.float32
BF16 = jnp.bfloat16

D_MODEL = 2048
PLE_DIM = 256
GRID_W = 64
RMS_EPS = 1e-6
HEADS = 8
HEAD_DIM = 128
A_CONV_W = 5
NA_ROWS = 8
NA_COLS = 16
ROPE_THETA = 500000.0
ROPE_DIMS = HEAD_DIM // 4
D_FF = 8192
FFN_CONV_W = 3
A_QKV_W = 3 * HEADS * HEAD_DIM
A_V_W = HEADS * HEAD_DIM
B_W = HEADS * HEAD_DIM
C_QK_W = HEADS * 2 * HEAD_DIM

LANES = 128
SUBLANES_BF16 = 16
VMEM_LIMIT = 56 * 1024 * 1024
NEG_BIG = -1e30

ROW_TILE = 512
MM_ROW_TILE = 1024
COL_TILE = 512
FF_TILE = 512
DN_BLOCK = 256
DN_BASE = 64
DN_GROUP = 128
DN_HEADS_PER_STEP = 4
ATT_Q_TILE = 512
ATT_SUB_ROWS = 256
NA_Q_ROWS = 8
NA_SUB_Q_ROWS = 4
NA_SUB_K_ROWS = 12


def _params(*sem):
    return pltpu.CompilerParams(dimension_semantics=sem, vmem_limit_bytes=VMEM_LIMIT)


def _rms(x, w):
    return x * lax.rsqrt(jnp.mean(x * x, axis=-1, keepdims=True) + RMS_EPS) * w


def _dot(a, b):
    return jnp.dot(a, b, preferred_element_type=F32)


def _dot_nt(a, b):
    return lax.dot_general(a, b, (((1,), (1,)), ((), ())), preferred_element_type=F32)


def _dot_tn(a, b):
    return lax.dot_general(a, b, (((0,), (0,)), ((), ())), preferred_element_type=F32)


def _sigmoid(x):
    return 1.0 / (1.0 + jnp.exp(-x))


def _norm_matmul_kernel(*refs, rope_tiles):
    if rope_tiles:
        x_ref, nw_ref, w_ref, rc_ref, rs1_ref, rs2_ref, o_ref, xn_ref = refs
    else:
        x_ref, nw_ref, w_ref, o_ref, xn_ref = refs
    j = pl.program_id(1)

    @pl.when(j == 0)
    def _():
        xn_ref[...] = _rms(x_ref[...], nw_ref[...]).astype(BF16)

    y = _dot(xn_ref[...], w_ref[0, 0])
    if not rope_tiles:
        o_ref[...] = y.astype(o_ref.dtype)
        return

    @pl.when(j < rope_tiles)
    def _():
        n = y.shape[1]
        reps = n // LANES
        c = jnp.concatenate([rc_ref[...]] * reps, axis=1)
        s1 = jnp.concatenate([rs1_ref[...]] * reps, axis=1)
        s2 = jnp.concatenate([rs2_ref[...]] * reps, axis=1)
        half = ROPE_DIMS // 2
        r = y * c + pltpu.roll(y, half, 1) * s1 + pltpu.roll(y, n - half, 1) * s2
        o_ref[...] = r.astype(o_ref.dtype)

    @pl.when(j >= rope_tiles)
    def _():
        o_ref[...] = y.astype(o_ref.dtype)


def _column_tiles(w, tn):
    nl, k, n = w.shape
    return jnp.transpose(w.reshape(nl, k, n // tn, tn), (0, 2, 1, 3)).astype(BF16)


def norm_matmul(x, nw, w_tiles, layer, out_dtype, rope=None, rope_tiles=0, seq_len=None, tm=MM_ROW_TILE):
    t, k = x.shape
    _, n_tiles, _, tn = w_tiles.shape
    n = n_tiles * tn
    tm = min(tm, t)
    assert t % tm == 0
    in_specs = [
        pl.BlockSpec((tm, k), lambda i, j: (i, 0)),
        pl.BlockSpec((1, k), lambda i, j: (0, 0)),
        pl.BlockSpec((1, 1, k, tn), lambda i, j: (layer, j, 0, 0)),
    ]
    args = [x, nw.reshape(1, k), w_tiles]
    if rope_tiles:
        per_seq = seq_len // tm
        for tab in rope:
            in_specs.append(pl.BlockSpec((tm, LANES), lambda i, j: (i % per_seq, 0)))
            args.append(tab)
    return pl.pallas_call(
        functools.partial(_norm_matmul_kernel, rope_tiles=rope_tiles),
        grid=(t // tm, n // tn),
        in_specs=in_specs,
        out_specs=pl.BlockSpec((tm, tn), lambda i, j: (i, j)),
        out_shape=jax.ShapeDtypeStruct((t, n), out_dtype),
        scratch_shapes=[pltpu.VMEM((tm, k), BF16)],
        compiler_params=_params("parallel", "arbitrary"),
        name="norm_matmul",
    )(*args)


def _proj_norm_res_kernel(*refs, n_in):
    a_refs = refs[:n_in]
    w_refs = refs[n_in:2 * n_in]
    nw_ref, h_ref, o_ref = refs[2 * n_in:]
    y = _dot(a_refs[0][...], w_refs[0][0])
    for a_ref, w_ref in zip(a_refs[1:], w_refs[1:]):
        y = y + _dot(a_ref[...], w_ref[0])
    o_ref[...] = h_ref[...] + _rms(y, nw_ref[...])


def proj_norm_res(acts, w, layer, nw, h, tm=ROW_TILE):
    t, n = h.shape
    n_in = len(acts)
    kw = acts[0].shape[1]
    assert all(a.shape[1] == kw for a in acts) and w.shape[1] == n_in * kw
    in_specs = [pl.BlockSpec((tm, kw), lambda i: (i, 0)) for _ in acts]
    in_specs += [pl.BlockSpec((1, kw, n), functools.partial(lambda piece, i: (layer, piece, 0), piece))
                 for piece in range(n_in)]
    in_specs += [pl.BlockSpec((1, n), lambda i: (0, 0)), pl.BlockSpec((tm, n), lambda i: (i, 0))]
    ws = [w] * n_in
    return pl.pallas_call(
        functools.partial(_proj_norm_res_kernel, n_in=n_in),
        grid=(t // tm,),
        in_specs=in_specs,
        out_specs=pl.BlockSpec((tm, n), lambda i: (i, 0)),
        out_shape=jax.ShapeDtypeStruct((t, n), F32),
        compiler_params=_params("parallel"),
        name="proj_norm_res",
    )(*acts, *ws, nw.reshape(1, n), h)


def _ple_kernel(h_ref, pa_ref, pb_ref, wg_ref, wp_ref, o_ref, *, first_block, a_blocks):
    h = h_ref[...]
    gate = _sigmoid(_dot(h.astype(BF16), wg_ref[0]))
    p = jnp.where(first_block + pl.program_id(0) < a_blocks, pa_ref[...], pb_ref[...])
    o_ref[...] = h + gate * _dot(p.astype(BF16), wp_ref[0])


def ple(h, p_a, p_b, wg, wp, layer, row0=0, rows=None, tm=ROW_TILE):
    n = h.shape[1]
    rows = h.shape[0] if rows is None else rows
    assert row0 % tm == 0 and rows % tm == 0 and p_a.shape[0] % tm == 0 and p_b.shape[0] % tm == 0
    i0 = row0 // tm
    na, nb = p_a.shape[0] // tm, p_b.shape[0] // tm
    return pl.pallas_call(
        functools.partial(_ple_kernel, first_block=i0, a_blocks=na),
        grid=(rows // tm,),
        in_specs=[
            pl.BlockSpec((tm, n), lambda i: (i0 + i, 0)),
            pl.BlockSpec((tm, p_a.shape[1]), lambda i: (jnp.minimum(i0 + i, na - 1), 0)),
            pl.BlockSpec((tm, p_b.shape[1]), lambda i: (jnp.clip(i0 + i - na, 0, nb - 1), 0)),
            pl.BlockSpec((1,) + wg.shape[1:], lambda i: (layer, 0, 0)),
            pl.BlockSpec((1,) + wp.shape[1:], lambda i: (layer, 0, 0)),
        ],
        out_specs=pl.BlockSpec((tm, n), lambda i: (i, 0)),
        out_shape=jax.ShapeDtypeStruct((rows, n), F32),
        compiler_params=_params("parallel"),
        name="ple",
    )(h, p_a, p_b, wg, wp)


def _ffn_kernel(x_ref, xp_ref, xx_ref, nw_ref, wg_ref, wu_ref, cwg_ref, cwu_ref, cbg_ref, cbu_ref,
                wo_ref, nw2_ref, o_ref, xn_ref, hg_ref, hu_ref, acc_ref, *, tm, per_seq):
    i = pl.program_id(0)
    j = pl.program_id(1)

    @pl.when(j == 0)
    def _():
        nw = nw_ref[...]
        pos = i % per_seq
        prev = jnp.where(pos == 0, 0.0, _rms(xp_ref[...], nw))
        nxt = jnp.where(pos == per_seq - 1, 0.0, _rms(xx_ref[...], nw))
        xn_ref[0:tm, :] = _rms(x_ref[...], nw).astype(BF16)
        xn_ref[tm:, :] = jnp.concatenate([nxt, prev], axis=0).astype(BF16)
        acc_ref[...] = jnp.zeros_like(acc_ref)

    xe = xn_ref[...]

    def conv(w_ref, h_ref, cw_ref, cb_ref):
        hid = _dot(xe, w_ref[0, 0])
        h_ref[8:, :] = hid
        h_ref[0:8, :] = hid[tm + 8:, :]
        cw = cw_ref[...]
        return (h_ref[pl.ds(7, tm), :] * cw[0:1] + h_ref[pl.ds(8, tm), :] * cw[1:2]
                + h_ref[pl.ds(9, tm), :] * cw[2:3] + cb_ref[...])

    g = conv(wg_ref, hg_ref, cwg_ref, cbg_ref)
    u = conv(wu_ref, hu_ref, cwu_ref, cbu_ref)
    c0 = math.sqrt(2.0 / math.pi)
    gelu = 0.5 * g * (1.0 + jnp.tanh(c0 * (g + 0.044715 * (g * g * g))))
    acc_ref[...] += _dot((gelu * u).astype(BF16), wo_ref[0])

    @pl.when(j == pl.num_programs(1) - 1)
    def _():
        o_ref[...] = x_ref[...] + _rms(acc_ref[...], nw2_ref[...])


def conv_ffn(h, nw_in, w_in, w_out, layer, conv_w, conv_b, nw_out, seq_len, tm=ROW_TILE):
    t, d = h.shape
    f = w_out.shape[1]
    tf = w_in.shape[3]
    nf = f // tf
    per_seq = seq_len // tm
    r8 = tm // 8
    last8 = t // 8 - 1
    cb = conv_b.reshape(1, 2 * f)
    return pl.pallas_call(
        functools.partial(_ffn_kernel, tm=tm, per_seq=per_seq),
        grid=(t // tm, nf),
        in_specs=[
            pl.BlockSpec((tm, d), lambda i, j: (i, 0)),
            pl.BlockSpec((8, d), lambda i, j: (jnp.maximum(i * r8 - 1, 0), 0)),
            pl.BlockSpec((8, d), lambda i, j: (jnp.minimum((i + 1) * r8, last8), 0)),
            pl.BlockSpec((1, d), lambda i, j: (0, 0)),
            pl.BlockSpec((1, 1, d, tf), lambda i, j: (layer, j, 0, 0)),
            pl.BlockSpec((1, 1, d, tf), lambda i, j: (layer, j + nf, 0, 0)),
            pl.BlockSpec((FFN_CONV_W, tf), lambda i, j: (0, j)),
            pl.BlockSpec((FFN_CONV_W, tf), lambda i, j: (0, j + nf)),
            pl.BlockSpec((1, tf), lambda i, j: (0, j)),
            pl.BlockSpec((1, tf), lambda i, j: (0, j + nf)),
            pl.BlockSpec((1, tf, d), lambda i, j: (layer, j, 0)),
            pl.BlockSpec((1, d), lambda i, j: (0, 0)),
        ],
        out_specs=pl.BlockSpec((tm, d), lambda i, j: (i, 0)),
        out_shape=jax.ShapeDtypeStruct((t, d), F32),
        scratch_shapes=[
            pltpu.VMEM((tm + SUBLANES_BF16, d), BF16),
            pltpu.VMEM((8 + tm + SUBLANES_BF16, tf), F32),
            pltpu.VMEM((8 + tm + SUBLANES_BF16, tf), F32),
            pltpu.VMEM((tm, d), F32),
        ],
        compiler_params=_params("parallel", "arbitrary"),
        name="conv_ffn",
    )(h, h, h, nw_in.reshape(1, d), w_in, w_in, conv_w, conv_w, cb, cb, w_out, nw_out.reshape(1, d))


def _diff_attn_kernel(lp_ref, sub_ref, q_ref, k_ref, v_ref, o_ref, *, lambda_init):
    lp = lp_ref[...]
    lam = (jnp.exp(jnp.sum(lp[0:1] * lp[1:2], axis=-1, keepdims=True))
           - jnp.exp(jnp.sum(lp[2:3] * lp[3:4], axis=-1, keepdims=True)) + lambda_init)
    sub = sub_ref[...]
    n_sub = q_ref.shape[1] // ATT_SUB_ROWS

    def scores(i):
        q = q_ref[0, i * ATT_SUB_ROWS:(i + 1) * ATT_SUB_ROWS, :].astype(F32)
        q = (q * (HEAD_DIM ** -0.5 * math.log2(math.e))).astype(BF16)
        return [_dot_nt(q[:, lo:lo + HEAD_DIM], k_ref[0, :, lo:lo + HEAD_DIM]) for lo in (0, HEAD_DIM)]

    def weights(s12):
        es = [jnp.exp2(s - jnp.max(s, axis=-1, keepdims=True)) for s in s12]
        l1, l2 = [jnp.sum(e, axis=-1, keepdims=True) for e in es]
        return (es[0] - es[1] * (lam * l1 / l2)).astype(BF16), 1.0 / l1

    def finish(i, a_inv):
        a, inv_l1 = a_inv
        o = _dot(a, v_ref[0]) * inv_l1
        o_ref[0, i * ATT_SUB_ROWS:(i + 1) * ATT_SUB_ROWS, :] = (_rms(o, sub) * (1.0 - lambda_init)).astype(o_ref.dtype)

    s_next = scores(0)
    a_prev = None
    for i in range(n_sub):
        s_cur = s_next
        if i + 1 < n_sub:
            s_next = scores(i + 1)
        a_cur = weights(s_cur)
        if a_prev is not None:
            finish(i - 1, a_prev)
        a_prev = a_cur
    finish(n_sub - 1, a_prev)


def diff_attention(proj, lam_params, subln, layer, tq=ATT_Q_TILE):
    b, s, _ = proj.shape
    w = 2 * HEAD_DIM
    lambda_init = 0.8 - 0.6 * math.exp(-0.3 * layer)
    return pl.pallas_call(
        functools.partial(_diff_attn_kernel, lambda_init=lambda_init),
        grid=(b, HEADS, s // tq),
        in_specs=[
            pl.BlockSpec((4, HEAD_DIM), lambda bi, h, qi: (0, 0)),
            pl.BlockSpec((1, w), lambda bi, h, qi: (0, 0)),
            pl.BlockSpec((1, tq, w), lambda bi, h, qi: (bi, qi, h)),
            pl.BlockSpec((1, s, w), lambda bi, h, qi: (bi, 0, HEADS + h)),
            pl.BlockSpec((1, s, w), lambda bi, h, qi: (bi, 0, 2 * HEADS + h)),
        ],
        out_specs=pl.BlockSpec((1, tq, w), lambda bi, h, qi: (bi, qi, h)),
        out_shape=jax.ShapeDtypeStruct((b, s, C_QK_W), BF16),
        compiler_params=_params("parallel", "parallel", "arbitrary"),
        name="diff_attention",
    )(lam_params, subln.reshape(1, w), proj, proj, proj)


def _na_key_start_row(first_query_row):
    return (first_query_row - NA_ROWS // 2, 0, GRID_W - NA_SUB_K_ROWS)


def _na_bias(rpb):
    rows = GRID_W
    n_sub = NA_Q_ROWS // NA_SUB_Q_ROWS
    sel_r = np.zeros((3, n_sub, NA_SUB_Q_ROWS, NA_SUB_K_ROWS, 2 * NA_ROWS - 1), np.float32)
    for case, rb in enumerate((0, 1, rows // NA_Q_ROWS - 1)):
        for p in range(n_sub):
            r0 = rb * NA_Q_ROWS + p * NA_SUB_Q_ROWS
            base = int(np.clip(*_na_key_start_row(r0)))
            for ql in range(NA_SUB_Q_ROWS):
                r = r0 + ql
                row_start = int(np.clip(r - NA_ROWS // 2, 0, rows - NA_ROWS))
                assert base <= row_start and row_start + NA_ROWS <= base + NA_SUB_K_ROWS
                for kl in range(NA_SUB_K_ROWS):
                    kr = base + kl
                    if row_start <= kr < row_start + NA_ROWS:
                        sel_r[case, p, ql, kl, kr - r + NA_ROWS - 1] = 1.0
    c = np.arange(GRID_W)
    col_start = np.clip(c - NA_COLS // 2, 0, GRID_W - NA_COLS)
    col_ok = (c[None, :] >= col_start[:, None]) & (c[None, :] < col_start[:, None] + NA_COLS)
    dc = np.clip(c[None, :] - c[:, None] + NA_COLS - 1, 0, 2 * NA_COLS - 2)
    sel_c = (np.arange(2 * NA_COLS - 1)[:, None, None] == dc[None]) & col_ok[None]
    valid = (sel_r.sum(-1) > 0)[:, :, :, None, :, None] & col_ok[None, None, None, :, None, :]
    t1 = jnp.einsum("hab,cpqka->hcpqkb", rpb, sel_r, precision=lax.Precision.HIGHEST)
    t2 = jnp.einsum("hcpqkb,bxy->hcpqxky", t1, sel_c.astype(np.float32), precision=lax.Precision.HIGHEST)
    bias = jnp.where(valid[None], t2 * math.log2(math.e), -jnp.inf)
    return bias.reshape(rpb.shape[0], 3, n_sub, NA_SUB_Q_ROWS * GRID_W, NA_SUB_K_ROWS * GRID_W)


def _na_kernel(q_ref, k_ref, v_ref, b_ref, o_ref):
    rb = pl.program_id(2)
    nq = NA_SUB_Q_ROWS * GRID_W
    nk = NA_SUB_K_ROWS * GRID_W
    n_sub = NA_Q_ROWS // NA_SUB_Q_ROWS

    def key_start(p):
        row, lo, hi = _na_key_start_row(rb * NA_Q_ROWS + p * NA_SUB_Q_ROWS)
        return pl.multiple_of(jnp.clip(row, lo, hi) * GRID_W, NA_SUB_Q_ROWS * GRID_W)

    def scores(p):
        q = (q_ref[0, p * nq:(p + 1) * nq, :].astype(F32) * (HEAD_DIM ** -0.5 * math.log2(math.e))).astype(BF16)
        return _dot_nt(q, k_ref[0, pl.ds(key_start(p), nk), :]) + b_ref[0, 0, p]

    def weights(s):
        e = jnp.exp2(s - jnp.max(s, axis=-1, keepdims=True))
        return e.astype(BF16), 1.0 / jnp.sum(e, axis=-1, keepdims=True)

    def finish(p, e_inv):
        e, inv_l = e_inv
        o = _dot(e, v_ref[0, pl.ds(key_start(p), nk), :]) * inv_l
        o_ref[0, p * nq:(p + 1) * nq, :] = o.astype(o_ref.dtype)

    s_next = scores(0)
    e_prev = None
    for p in range(n_sub):
        s_cur = s_next
        if p + 1 < n_sub:
            s_next = scores(p + 1)
        e_cur = weights(s_cur)
        if e_prev is not None:
            finish(p - 1, e_prev)
        e_prev = e_cur
    finish(n_sub - 1, e_prev)


def neighbourhood_attention(proj, col0, bias):
    b, s, _ = proj.shape
    nq = NA_Q_ROWS * GRID_W
    nblk = s // nq
    c0 = col0 // HEAD_DIM

    def bias_map(bi, h, rb):
        return (h, jnp.where(rb == 0, 0, jnp.where(rb == nblk - 1, 2, 1)), 0, 0, 0)

    return pl.pallas_call(
        _na_kernel,
        grid=(b, HEADS, nblk),
        in_specs=[
            pl.BlockSpec((1, nq, HEAD_DIM), lambda bi, h, rb: (bi, rb, c0 + h)),
            pl.BlockSpec((1, s, HEAD_DIM), lambda bi, h, rb: (bi, 0, c0 + HEADS + h)),
            pl.BlockSpec((1, s, HEAD_DIM), lambda bi, h, rb: (bi, 0, c0 + 2 * HEADS + h)),
            pl.BlockSpec((1, 1) + bias.shape[2:], bias_map),
        ],
        out_specs=pl.BlockSpec((1, nq, HEAD_DIM), lambda bi, h, rb: (bi, rb, h)),
        out_shape=jax.ShapeDtypeStruct((b, s, B_W), BF16),
        compiler_params=_params("parallel", "parallel", "arbitrary"),
        name="neighbourhood_attention",
    )(proj, proj, proj, bias)


HALO = SUBLANES_BF16


def _split3(x):
    hi = x.astype(BF16)
    r1 = x - hi.astype(F32)
    mid = r1.astype(BF16)
    lo = (r1 - mid.astype(F32)).astype(BF16)
    return hi, mid, lo


def _dn_prep_kernel(x_ref, xp_ref, xx_ref, cw_ref, gt_ref, alog_ref, dtb_ref,
                    q_ref, k_ref, v_ref, gc_ref, beta_ref, xs_ref):
    i = pl.program_id(1)
    tb = x_ref.shape[1]
    pad = A_CONV_W // 2
    xs_ref[0:HALO, :] = jnp.where(i == 0, 0.0, xp_ref[0].astype(F32))
    xs_ref[HALO:HALO + tb, :] = x_ref[0].astype(F32)
    xs_ref[HALO + tb:, :] = jnp.where(i == pl.num_programs(1) - 1, 0.0, xx_ref[0].astype(F32))
    cw = cw_ref[...]
    y = xs_ref[pl.ds(HALO - pad, tb), :] * cw[0:1]
    for t in range(1, A_CONV_W):
        y = y + xs_ref[pl.ds(HALO - pad + t, tb), :] * cw[t:t + 1]
    y = y * _sigmoid(y)
    for h in range(HEADS):
        qh = y[:, h * HEAD_DIM:(h + 1) * HEAD_DIM]
        kh = y[:, (HEADS + h) * HEAD_DIM:(HEADS + h + 1) * HEAD_DIM]
        qn = qh * (lax.rsqrt(jnp.sum(qh * qh, axis=-1, keepdims=True) + 1e-6) * (HEAD_DIM ** -0.5))
        kn = kh * lax.rsqrt(jnp.sum(kh * kh, axis=-1, keepdims=True) + 1e-6)
        q_ref[0, h] = qn.astype(q_ref.dtype)
        k_ref[0, h] = kn.astype(k_ref.dtype)
        v_ref[0, h] = y[:, (2 * HEADS + h) * HEAD_DIM:(2 * HEADS + h + 1) * HEAD_DIM].astype(v_ref.dtype)

    nd = 2 * HEADS
    gates = gt_ref[0]
    z = gates[:, 0:nd] + dtb_ref[...]
    softplus = jnp.maximum(z, 0.0) + jnp.log(1.0 + jnp.exp(-jnp.abs(z)))
    g = -jnp.exp(alog_ref[...]) * softplus
    beta_ref[0] = _sigmoid(gates[:, nd:2 * nd])
    r = lax.broadcasted_iota(jnp.int32, (tb, tb), 0)
    c = lax.broadcasted_iota(jnp.int32, (tb, tb), 1)
    lower = jnp.where(r >= c, 1.0, 0.0).astype(BF16)
    upper = jnp.where(r <= c, 1.0, 0.0).astype(BF16)
    parts = _split3(g)
    fwd = _dot(lower, parts[0]) + _dot(lower, parts[1]) + _dot(lower, parts[2])
    bwd = _dot(upper, parts[0]) + _dot(upper, parts[1]) + _dot(upper, parts[2])
    col = lax.broadcasted_iota(jnp.int32, (tb, nd), 1)
    gc_ref[0] = jnp.where(col < HEADS, fwd, bwd)


def deltanet_prep(proj, gates, conv_w, a_log, dt_bias, tb=DN_BLOCK):
    b, s, _ = proj.shape
    nb = s // tb
    rh = tb // HALO
    nd = 2 * HEADS
    hs = jax.ShapeDtypeStruct((b, HEADS, s, HEAD_DIM), BF16)
    head_spec = pl.BlockSpec((1, HEADS, tb, HEAD_DIM), lambda bi, i: (bi, 0, i, 0))
    vec_spec = pl.BlockSpec((1, tb, nd), lambda bi, i: (bi, i, 0))
    return pl.pallas_call(
        _dn_prep_kernel,
        grid=(b, nb),
        in_specs=[
            pl.BlockSpec((1, tb, A_QKV_W), lambda bi, i: (bi, i, 0)),
            pl.BlockSpec((1, HALO, A_QKV_W), lambda bi, i: (bi, jnp.maximum(i * rh - 1, 0), 0)),
            pl.BlockSpec((1, HALO, A_QKV_W), lambda bi, i: (bi, jnp.minimum((i + 1) * rh, s // HALO - 1), 0)),
            pl.BlockSpec((A_CONV_W, A_QKV_W), lambda bi, i: (0, 0)),
            pl.BlockSpec((1, tb, LANES), lambda bi, i: (bi, i, 0)),
            pl.BlockSpec((1, nd), lambda bi, i: (0, 0)),
            pl.BlockSpec((1, nd), lambda bi, i: (0, 0)),
        ],
        out_specs=[head_spec, head_spec, head_spec, vec_spec, vec_spec],
        out_shape=[hs, hs, hs, jax.ShapeDtypeStruct((b, s, nd), F32), jax.ShapeDtypeStruct((b, s, nd), F32)],
        scratch_shapes=[pltpu.VMEM((tb + 2 * HALO, A_QKV_W), F32)],
        compiler_params=_params("parallel", "arbitrary"),
        name="deltanet_prep",
    )(proj, proj, proj, conv_w, gates, a_log.reshape(1, nd), dt_bias.reshape(1, nd))


def _split2(x):
    hi = x.astype(BF16)
    return hi, (x - hi.astype(F32)).astype(BF16)


def _unit_triangular_solve(ms, rhss, same_block):
    n = ms[0].shape[0]
    grp = DN_GROUP
    ng = n // grp
    r = lax.broadcasted_iota(jnp.int32, (grp, grp), 0)
    c = lax.broadcasted_iota(jnp.int32, (grp, grp), 1)
    base_mask = (r // DN_BASE) == (c // DN_BASE)
    row = lax.broadcasted_iota(jnp.int32, (DN_BASE, grp), 0)
    lane = lax.broadcasted_iota(jnp.int32, (DN_BASE, grp), 1)
    left = lane < DN_BASE

    def pair_lhs(hi, lo):
        hi_f, lo_f = hi.astype(F32), lo.astype(F32)
        swapped = pltpu.roll(hi_f, DN_BASE, 1)
        top = jnp.concatenate([jnp.where(left, hi_f, swapped), jnp.where(left, lo_f, 0.0)], axis=1)
        bottom = jnp.concatenate([jnp.where(left, swapped, hi_f), jnp.where(left, 0.0, lo_f)], axis=1)
        return jnp.concatenate([top, bottom], axis=0).astype(BF16)

    def pair_product(lhs, rhs_hi, rhs_lo):
        res = _dot(lhs, jnp.concatenate([rhs_hi, rhs_lo, rhs_hi, rhs_hi], axis=0))
        return jnp.where(left, res[:DN_BASE], res[DN_BASE:])

    pieces = [jnp.where(base_mask, m[g * grp:(g + 1) * grp, g * grp:(g + 1) * grp], 0.0)
              for m in ms for g in range(ng)]
    pws = [piece[:DN_BASE] + piece[DN_BASE:] for piece in pieces]
    ts = [jnp.where(lane % DN_BASE == row, 1.0, 0.0) + pw for pw in pws]
    splits = [_split2(pw) for pw in pws]
    lhss = [pair_lhs(hi, lo) for hi, lo in splits]
    size = 2
    while size < DN_BASE:
        pws = [pair_product(lhs, hi, lo) for lhs, (hi, lo) in zip(lhss, splits)]
        splits = [_split2(pw) for pw in pws]
        lhss = [pair_lhs(hi, lo) for hi, lo in splits]
        ts = [t + pair_product(lhs, *_split2(t)) for lhs, t in zip(lhss, ts)]
        size *= 2
    t_bases = [jnp.concatenate([jnp.where(left, t, 0.0), jnp.where(left, 0.0, t)], axis=0).astype(BF16)
               for t in ts]
    offs = {}
    blk = 2 * DN_BASE
    while blk <= n:
        offs[blk] = [jnp.where(same_block(blk), jnp.where(same_block(blk // 2), 0.0, m), 0.0).astype(BF16)
                     for m in ms]
        blk *= 2

    def apply(blk, ys):
        if blk == DN_BASE:
            ybs = [y.astype(BF16) for y in ys]
            return [jnp.concatenate([_dot(t_bases[i * ng + g], yb[g * grp:(g + 1) * grp]) for g in range(ng)],
                                    axis=0) for i, yb in enumerate(ybs)]
        zs = apply(blk // 2, ys)
        corr = apply(blk // 2, [_dot(off, z.astype(BF16)) for off, z in zip(offs[blk], zs)])
        return [z + cr for z, cr in zip(zs, corr)]

    return apply(n, rhss)


def _dn_block(chains):
    n = chains[0][0].shape[0]
    r = lax.broadcasted_iota(jnp.int32, (n, n), 0)
    c = lax.broadcasted_iota(jnp.int32, (n, n), 1)

    def same_block(size):
        return (r // size) == (c // size)

    ms, rhss, pre = [], [], []
    for q, k, v, gcol, grow, beta, state_ref, forward in chains:
        d = (r - c) if forward else (c - r)
        kf = k.astype(F32)
        kb = kf * beta
        decay = jnp.exp(jnp.where(d >= 0, gcol - grow, NEG_BIG))
        ms.append(jnp.where(d > 0, -(_dot_nt(kb.astype(BF16), k) * decay), 0.0))
        intra = (_dot_nt(q, k) * decay).astype(BF16)
        eg = jnp.exp(gcol)
        rhss.append(jnp.concatenate([v.astype(F32) * beta, kb * eg], axis=1))
        g_last = grow[:, n - 1:n] if forward else grow[:, 0:1]
        pre.append((intra, (q.astype(F32) * eg).astype(BF16), (kf * jnp.exp(g_last - gcol)).astype(BF16),
                    jnp.exp(g_last)))
    sols = _unit_triangular_solve(ms, rhss, same_block)
    outs = []
    for chain, sol, (intra, q_dec, k_dec, blk_decay) in zip(chains, sols, pre):
        state_ref = chain[6]
        state = state_ref[...]
        sb = state.astype(BF16)
        v_new = sol[:, :HEAD_DIM] - _dot(sol[:, HEAD_DIM:].astype(BF16), sb)
        vb = v_new.astype(BF16)
        outs.append(_dot(q_dec, sb) + _dot(intra, vb))
        state_ref[...] = state * blk_decay + _dot_tn(k_dec, vb)
    return outs


def _dn_kernel(qf_ref, kf_ref, vf_ref, qb_ref, kb_ref, vb_ref, gcf_ref, gcb_ref, btf_ref, btb_ref,
               grf_ref, grb_ref, of_ref, ob_ref, state_ref):
    hb = qf_ref.shape[1]

    @pl.when(pl.program_id(2) == 0)
    def _():
        state_ref[...] = jnp.zeros_like(state_ref)

    lane = lax.broadcasted_iota(jnp.int32, gcf_ref.shape[1:], 1)

    def pick(ref, idx):
        return jnp.sum(jnp.where(lane == idx, ref[0], 0.0), axis=-1, keepdims=True)

    chains = []
    for hl in range(hb):
        h = pl.program_id(1) * hb + hl
        chains.append((qf_ref[0, hl], kf_ref[0, hl], vf_ref[0, hl], pick(gcf_ref, h),
                       grf_ref[0, pl.ds(h, 1), :], pick(btf_ref, h), state_ref.at[0, hl], True))
        chains.append((qb_ref[0, hl], kb_ref[0, hl], vb_ref[0, hl], pick(gcb_ref, HEADS + h),
                       grb_ref[0, pl.ds(HEADS + h, 1), :], pick(btb_ref, HEADS + h), state_ref.at[1, hl], False))
    outs = _dn_block(chains)
    for hl in range(hb):
        of_ref[0, hl] = outs[2 * hl]
        ob_ref[0, hl] = outs[2 * hl + 1]


def deltanet_scan(q, k, v, gc, beta, gc_rows, tb=DN_BLOCK, hb=DN_HEADS_PER_STEP):
    b, _, s, _ = q.shape
    nb = s // tb
    nd = 2 * HEADS
    fwd = pl.BlockSpec((1, hb, tb, HEAD_DIM), lambda bi, h, c: (bi, h, c, 0))
    bwd = pl.BlockSpec((1, hb, tb, HEAD_DIM), lambda bi, h, c: (bi, h, nb - 1 - c, 0))
    vec_f = pl.BlockSpec((1, tb, nd), lambda bi, h, c: (bi, c, 0))
    vec_b = pl.BlockSpec((1, tb, nd), lambda bi, h, c: (bi, nb - 1 - c, 0))
    row_f = pl.BlockSpec((1, nd, tb), lambda bi, h, c: (bi, 0, c))
    row_b = pl.BlockSpec((1, nd, tb), lambda bi, h, c: (bi, 0, nb - 1 - c))
    os_ = jax.ShapeDtypeStruct((b, HEADS, s, HEAD_DIM), F32)
    return pl.pallas_call(
        _dn_kernel,
        grid=(b, HEADS // hb, nb),
        in_specs=[fwd, fwd, fwd, bwd, bwd, bwd, vec_f, vec_b, vec_f, vec_b, row_f, row_b],
        out_specs=[fwd, bwd],
        out_shape=[os_, os_],
        scratch_shapes=[pltpu.VMEM((2, hb, HEAD_DIM, HEAD_DIM), F32)],
        compiler_params=_params("parallel", "parallel", "arbitrary"),
        name="deltanet_scan",
    )(q, k, v, q, k, v, gc, gc, beta, beta, gc_rows, gc_rows)


def _dn_out_kernel(of_ref, ob_ref, z_ref, nw_ref, o_ref):
    nw = nw_ref[...]
    for h in range(HEADS):
        o = _rms(of_ref[0, h] + ob_ref[0, h], nw)
        z = z_ref[0, :, h * HEAD_DIM:(h + 1) * HEAD_DIM].astype(F32)
        o_ref[0, :, h * HEAD_DIM:(h + 1) * HEAD_DIM] = (o * (z * _sigmoid(z))).astype(o_ref.dtype)


def deltanet_out(o_f, o_b, proj, z_col0, out_norm, ts=ROW_TILE):
    b, _, s, _ = o_f.shape
    head_spec = pl.BlockSpec((1, HEADS, ts, HEAD_DIM), lambda bi, i: (bi, 0, i, 0))
    zb = z_col0 // A_V_W
    return pl.pallas_call(
        _dn_out_kernel,
        grid=(b, s // ts),
        in_specs=[head_spec, head_spec,
                  pl.BlockSpec((1, ts, A_V_W), lambda bi, i: (bi, i, zb)),
                  pl.BlockSpec((1, HEAD_DIM), lambda bi, i: (0, 0))],
        out_specs=pl.BlockSpec((1, ts, A_V_W), lambda bi, i: (bi, i, 0)),
        out_shape=jax.ShapeDtypeStruct((b, s, A_V_W), BF16),
        compiler_params=_params("parallel", "parallel"),
        name="deltanet_out",
    )(o_f, o_b, proj, out_norm.reshape(1, HEAD_DIM))


def _rope_tables(s):
    half = ROPE_DIMS // 2
    inv = ROPE_THETA ** (-jnp.arange(0, ROPE_DIMS, 2, dtype=F32) / ROPE_DIMS)
    ang = jnp.arange(s, dtype=F32)[:, None] * inv[None, :]
    cos, sin = jnp.cos(ang), jnp.sin(ang)
    rest = HEAD_DIM - ROPE_DIMS
    c = jnp.concatenate([cos, cos, jnp.ones((s, rest), F32)], axis=1)
    s1 = jnp.concatenate([jnp.zeros((s, half), F32), sin, jnp.zeros((s, rest), F32)], axis=1)
    s2 = jnp.concatenate([-sin, jnp.zeros((s, half + rest), F32)], axis=1)
    return c, s1, s2


def _trunk(x_a, x_b, p_a, p_b, ab_w_in, ab_conv_w, ab_a_log, ab_dt_bias, ab_out_norm, ab_rpb, ab_w_out,
           c_w_in, c_lambda, c_subln, c_w_out, norms, ffn_w_in, ffn_conv_w, ffn_conv_b,
           ffn_w_out, ple_w_proj, ple_w_gate):
    b_a, s, d = x_a.shape
    b = b_a + x_b.shape[0]
    t, t_a = b * s, b_a * s
    depth = norms.shape[0]
    h = jnp.concatenate([x_a, x_b], axis=0).reshape(t, d)
    rope = _rope_tables(s)
    o1 = A_QKV_W
    o2 = o1 + A_V_W
    o3 = o2 + 4 * HEADS
    ab_main = _column_tiles(jnp.concatenate([ab_w_in[:, :, :o2], ab_w_in[:, :, o3:]], axis=2), COL_TILE)
    ab_gate = _column_tiles(jnp.pad(ab_w_in[:, :, o2:o3], ((0, 0), (0, 0), (0, LANES - 4 * HEADS))), LANES)
    c_in = _column_tiles(c_w_in, COL_TILE)
    ffn_in = _column_tiles(ffn_w_in, FF_TILE)
    ab_out, c_out, ffn_out = ab_w_out.astype(BF16), c_w_out.astype(BF16), ffn_w_out.astype(BF16)
    ple_gate, ple_proj = ple_w_gate.astype(BF16), ple_w_proj.astype(BF16)
    for layer in range(depth):
        j = layer // 2
        if layer % 2 == 0:
            proj = norm_matmul(h, norms[layer, 0], ab_main, j, BF16).reshape(b, s, -1)
            gates = norm_matmul(h, norms[layer, 0], ab_gate, j, F32).reshape(b, s, LANES)
            qa, ka, va, gc, beta = deltanet_prep(proj, gates, ab_conv_w[j], ab_a_log[j], ab_dt_bias[j])
            o_f, o_b = deltanet_scan(qa, ka, va, gc, beta, jnp.transpose(gc, (0, 2, 1)))
            o_a = deltanet_out(o_f, o_b, proj, o1, ab_out_norm[j])
            o_nb = neighbourhood_attention(proj, o2, _na_bias(ab_rpb[j]))
            h = proj_norm_res([o_a.reshape(t, -1), o_nb.reshape(t, -1)], ab_out, j, norms[layer, 1], h)
        else:
            proj = norm_matmul(h, norms[layer, 0], c_in, j, BF16, rope=rope,
                               rope_tiles=2 * C_QK_W // COL_TILE, seq_len=s).reshape(b, s, -1)
            o_c = diff_attention(proj, c_lambda[j], c_subln[j], layer)
            h = proj_norm_res([o_c.reshape(t, -1)], c_out, j, norms[layer, 1], h)
        h = conv_ffn(h, norms[layer, 2], ffn_in, ffn_out, layer, ffn_conv_w[layer], ffn_conv_b[layer],
                     norms[layer, 3], s)
        ple_args = (p_a[layer].reshape(t_a, -1), p_b[layer].reshape(t - t_a, -1), ple_gate, ple_proj, layer)
        if layer + 1 < depth:
            h = ple(h, *ple_args)
    y_a = ple(h, *ple_args, row0=0, rows=t_a)
    y_b = ple(h, *ple_args, row0=t_a, rows=t - t_a)
    return y_a.reshape(b_a, s, d), y_b.reshape(b - b_a, s, d)


def kernel(x_prompt, x_sample, p_prompt, p_sample, ab_w_in, ab_conv_w, ab_a_log, ab_dt_bias, ab_out_norm,
           ab_rpb, ab_w_out, c_w_in, c_lambda, c_subln, c_w_out, norms, ffn_w_in, ffn_conv_w, ffn_conv_b,
           ffn_w_out, ple_w_proj, ple_w_gate):
    return _trunk(x_prompt, x_sample, p_prompt, p_sample, ab_w_in, ab_conv_w, ab_a_log, ab_dt_bias,
                  ab_out_norm, ab_rpb, ab_w_out, c_w_in, c_lambda, c_subln, c_w_out, norms, ffn_w_in,
                  ffn_conv_w, ffn_conv_b, ffn_w_out, ple_w_proj, ple_w_gate)
```

```python
import functools
import math

import jax
import jax.numpy as jnp
import numpy as np
from jax import lax
from jax.experimental import pallas as pl
from jax.experimental.pallas import tpu as pltpu

F32 = jnp.float32
BF16 = jnp.bfloat16

D_MODEL = 2048
PLE_DIM = 256
GRID_W = 64
RMS_EPS = 1e-6
HEADS = 8
HEAD_DIM = 128
A_CONV_W = 5
NA_ROWS = 8
NA_COLS = 16
ROPE_THETA = 500000.0
ROPE_DIMS = HEAD_DIM // 4
D_FF = 8192
FFN_CONV_W = 3
A_QKV_W = 3 * HEADS * HEAD_DIM
A_V_W = HEADS * HEAD_DIM
B_W = HEADS * HEAD_DIM
C_QK_W = HEADS * 2 * HEAD_DIM

LANES = 128
SUBLANES_BF16 = 16
VMEM_LIMIT = 56 * 1024 * 1024
NEG_BIG = -1e30

ROW_TILE = 512
MM_ROW_TILE = 1024
COL_TILE = 512
FF_TILE = 512
FFN_ROW_SPLIT = 4
DN_BLOCK = 256
DN_BASE = 64
DN_GROUP = 128
DN_HEADS_PER_STEP = 4
ATT_Q_TILE = 512
ATT_SUB_ROWS = 256
NA_Q_ROWS = 8
NA_SUB_Q_ROWS = 4
NA_SUB_K_ROWS = 12


def _params(*sem):
    return pltpu.CompilerParams(dimension_semantics=sem, vmem_limit_bytes=VMEM_LIMIT)


def _rms(x, w):
    return x * lax.rsqrt(jnp.mean(x * x, axis=-1, keepdims=True) + RMS_EPS) * w


def _dot(a, b):
    return jnp.dot(a, b, preferred_element_type=F32)


def _dot_nt(a, b):
    return lax.dot_general(a, b, (((1,), (1,)), ((), ())), preferred_element_type=F32)


def _dot_tn(a, b):
    return lax.dot_general(a, b, (((0,), (0,)), ((), ())), preferred_element_type=F32)


def _sigmoid(x):
    return 1.0 / (1.0 + jnp.exp(-x))


def _norm_matmul_kernel(*refs, rope_tiles):
    if rope_tiles:
        x_ref, nw_ref, w_ref, rc_ref, rs1_ref, rs2_ref, o_ref, xn_ref = refs
    else:
        x_ref, nw_ref, w_ref, o_ref, xn_ref = refs
    j = pl.program_id(1)

    @pl.when(j == 0)
    def _():
        xn_ref[...] = _rms(x_ref[...], nw_ref[...]).astype(BF16)

    y = _dot(xn_ref[...], w_ref[0, 0])
    if not rope_tiles:
        o_ref[...] = y.astype(o_ref.dtype)
        return

    @pl.when(j < rope_tiles)
    def _():
        n = y.shape[1]
        reps = n // LANES
        c = jnp.concatenate([rc_ref[...]] * reps, axis=1)
        s1 = jnp.concatenate([rs1_ref[...]] * reps, axis=1)
        s2 = jnp.concatenate([rs2_ref[...]] * reps, axis=1)
        half = ROPE_DIMS // 2
        r = y * c + pltpu.roll(y, half, 1) * s1 + pltpu.roll(y, n - half, 1) * s2
        o_ref[...] = r.astype(o_ref.dtype)

    @pl.when(j >= rope_tiles)
    def _():
        o_ref[...] = y.astype(o_ref.dtype)


def _column_tiles(w, tn):
    nl, k, n = w.shape
    return jnp.transpose(w.reshape(nl, k, n // tn, tn), (0, 2, 1, 3)).astype(BF16)


def norm_matmul(x, nw, w_tiles, layer, out_dtype, rope=None, rope_tiles=0, seq_len=None, tm=MM_ROW_TILE):
    t, k = x.shape
    _, n_tiles, _, tn = w_tiles.shape
    n = n_tiles * tn
    tm = min(tm, t)
    assert t % tm == 0
    in_specs = [
        pl.BlockSpec((tm, k), lambda i, j: (i, 0)),
        pl.BlockSpec((1, k), lambda i, j: (0, 0)),
        pl.BlockSpec((1, 1, k, tn), lambda i, j: (layer, j, 0, 0)),
    ]
    args = [x, nw.reshape(1, k), w_tiles]
    if rope_tiles:
        per_seq = seq_len // tm
        for tab in rope:
            in_specs.append(pl.BlockSpec((tm, LANES), lambda i, j: (i % per_seq, 0)))
            args.append(tab)
    return pl.pallas_call(
        functools.partial(_norm_matmul_kernel, rope_tiles=rope_tiles),
        grid=(t // tm, n // tn),
        in_specs=in_specs,
        out_specs=pl.BlockSpec((tm, tn), lambda i, j: (i, j)),
        out_shape=jax.ShapeDtypeStruct((t, n), out_dtype),
        scratch_shapes=[pltpu.VMEM((tm, k), BF16)],
        compiler_params=_params("parallel", "arbitrary"),
        name="norm_matmul",
    )(*args)


def _proj_norm_res_kernel(*refs, n_in):
    a_refs = refs[:n_in]
    w_refs = refs[n_in:2 * n_in]
    nw_ref, h_ref, o_ref = refs[2 * n_in:]
    y = _dot(a_refs[0][...], w_refs[0][0])
    for a_ref, w_ref in zip(a_refs[1:], w_refs[1:]):
        y = y + _dot(a_ref[...], w_ref[0])
    o_ref[...] = h_ref[...] + _rms(y, nw_ref[...])


def proj_norm_res(acts, w, layer, nw, h, tm=ROW_TILE):
    t, n = h.shape
    n_in = len(acts)
    kw = acts[0].shape[1]
    assert all(a.shape[1] == kw for a in acts) and w.shape[1] == n_in * kw
    in_specs = [pl.BlockSpec((tm, kw), lambda i: (i, 0)) for _ in acts]
    in_specs += [pl.BlockSpec((1, kw, n), functools.partial(lambda piece, i: (layer, piece, 0), piece))
                 for piece in range(n_in)]
    in_specs += [pl.BlockSpec((1, n), lambda i: (0, 0)), pl.BlockSpec((tm, n), lambda i: (i, 0))]
    ws = [w] * n_in
    return pl.pallas_call(
        functools.partial(_proj_norm_res_kernel, n_in=n_in),
        grid=(t // tm,),
        in_specs=in_specs,
        out_specs=pl.BlockSpec((tm, n), lambda i: (i, 0)),
        out_shape=jax.ShapeDtypeStruct((t, n), F32),
        compiler_params=_params("parallel"),
        name="proj_norm_res",
    )(*acts, *ws, nw.reshape(1, n), h)


def _ple_kernel(h_ref, pa_ref, pb_ref, wg_ref, wp_ref, o_ref, *, first_block, a_blocks):
    h = h_ref[...]
    gate = _sigmoid(_dot(h.astype(BF16), wg_ref[0]))
    p = jnp.where(first_block + pl.program_id(0) < a_blocks, pa_ref[...], pb_ref[...])
    o_ref[...] = h + gate * _dot(p.astype(BF16), wp_ref[0])


def ple(h, p_a, p_b, wg, wp, layer, row0=0, rows=None, tm=ROW_TILE):
    n = h.shape[1]
    rows = h.shape[0] if rows is None else rows
    assert row0 % tm == 0 and rows % tm == 0 and p_a.shape[0] % tm == 0 and p_b.shape[0] % tm == 0
    i0 = row0 // tm
    na, nb = p_a.shape[0] // tm, p_b.shape[0] // tm
    return pl.pallas_call(
        functools.partial(_ple_kernel, first_block=i0, a_blocks=na),
        grid=(rows // tm,),
        in_specs=[
            pl.BlockSpec((tm, n), lambda i: (i0 + i, 0)),
            pl.BlockSpec((tm, p_a.shape[1]), lambda i: (jnp.minimum(i0 + i, na - 1), 0)),
            pl.BlockSpec((tm, p_b.shape[1]), lambda i: (jnp.clip(i0 + i - na, 0, nb - 1), 0)),
            pl.BlockSpec((1,) + wg.shape[1:], lambda i: (layer, 0, 0)),
            pl.BlockSpec((1,) + wp.shape[1:], lambda i: (layer, 0, 0)),
        ],
        out_specs=pl.BlockSpec((tm, n), lambda i: (i, 0)),
        out_shape=jax.ShapeDtypeStruct((rows, n), F32),
        compiler_params=_params("parallel"),
        name="ple",
    )(h, p_a, p_b, wg, wp)


def _ffn_kernel(x_ref, xp_ref, xx_ref, nw_ref, cw_ref, cb_ref, nw2_ref, win_hbm, wo_hbm, o_ref,
                xn_ref, hid_ref, acc_ref, win_buf, wo_buf, win_sem, wo_sem, *, tm, tf, nf, per_seq, layer):
    pos = pl.program_id(0) % per_seq

    def win_copy(chunk, slot, part):
        return pltpu.make_async_copy(win_hbm.at[layer, chunk + part * nf], win_buf.at[slot, part],
                                     win_sem.at[slot, part])

    def wo_copy(chunk, slot):
        return pltpu.make_async_copy(wo_hbm.at[layer, pl.ds(chunk * tf, tf)], wo_buf.at[slot], wo_sem.at[slot])

    def hidden(slot):
        xe = xn_ref[...]
        for part in range(2):
            hid = _dot(xe, win_buf[slot, part])
            hid_ref[slot, part, 8:, :] = hid
            hid_ref[slot, part, 0:8, :] = hid[tm + 8:, :]

    def project(chunk, slot):
        rows = tm // FFN_ROW_SPLIT
        for r0 in range(0, tm, rows):
            def conv(part):
                cw = cw_ref[chunk + part * nf]
                return (hid_ref[slot, part, pl.ds(7 + r0, rows), :] * cw[0:1]
                        + hid_ref[slot, part, pl.ds(8 + r0, rows), :] * cw[1:2]
                        + hid_ref[slot, part, pl.ds(9 + r0, rows), :] * cw[2:3] + cb_ref[chunk + part * nf])

            g, u = conv(0), conv(1)
            c0 = math.sqrt(2.0 / math.pi)
            gelu = 0.5 * g * (1.0 + jnp.tanh(c0 * (g + 0.044715 * (g * g * g))))
            acc_ref[r0:r0 + rows, :] += _dot((gelu * u).astype(BF16), wo_buf[slot])

    for part in range(2):
        win_copy(0, 0, part).start()
        win_copy(1, 1, part).start()
    wo_copy(0, 0).start()

    nw = nw_ref[...]
    prev = jnp.where(pos == 0, 0.0, _rms(xp_ref[...], nw))
    nxt = jnp.where(pos == per_seq - 1, 0.0, _rms(xx_ref[...], nw))
    xn_ref[0:tm, :] = _rms(x_ref[...], nw).astype(BF16)
    xn_ref[tm:, :] = jnp.concatenate([nxt, prev], axis=0).astype(BF16)
    acc_ref[...] = jnp.zeros_like(acc_ref)

    for part in range(2):
        win_copy(0, 0, part).wait()
    hidden(0)

    def body(c, carry):
        cur = lax.rem(c, 2)
        nxt_slot = 1 - cur

        @pl.when(c + 2 < nf)
        def _():
            for part in range(2):
                win_copy(c + 2, cur, part).start()

        wo_copy(c + 1, nxt_slot).start()
        for part in range(2):
            win_copy(c + 1, nxt_slot, part).wait()
        wo_copy(c, cur).wait()
        hidden(nxt_slot)
        project(c, cur)
        return carry

    lax.fori_loop(0, nf - 1, body, 0)
    last = (nf - 1) % 2
    wo_copy(nf - 1, last).wait()
    project(nf - 1, last)
    o_ref[...] = x_ref[...] + _rms(acc_ref[...], nw2_ref[...])


def conv_ffn(h, nw_in, w_in, w_out, layer, conv_w, conv_b, nw_out, seq_len, tm=ROW_TILE):
    t, d = h.shape
    f = w_out.shape[1]
    tf = w_in.shape[3]
    nf = f // tf
    assert nf >= 2
    per_seq = seq_len // tm
    r8 = tm // 8
    last8 = t // 8 - 1
    cw = jnp.transpose(conv_w.reshape(FFN_CONV_W, 2 * nf, tf), (1, 0, 2))
    cb = conv_b.reshape(2 * nf, 1, tf)
    hid_rows = 8 + tm + SUBLANES_BF16
    return pl.pallas_call(
        functools.partial(_ffn_kernel, tm=tm, tf=tf, nf=nf, per_seq=per_seq, layer=layer),
        grid=(t // tm,),
        in_specs=[
            pl.BlockSpec((tm, d), lambda i: (i, 0)),
            pl.BlockSpec((8, d), lambda i: (jnp.maximum(i * r8 - 1, 0), 0)),
            pl.BlockSpec((8, d), lambda i: (jnp.minimum((i + 1) * r8, last8), 0)),
            pl.BlockSpec((1, d), lambda i: (0, 0)),
            pl.BlockSpec(cw.shape, lambda i: (0, 0, 0)),
            pl.BlockSpec(cb.shape, lambda i: (0, 0, 0)),
            pl.BlockSpec((1, d), lambda i: (0, 0)),
            pl.BlockSpec(memory_space=pl.ANY),
            pl.BlockSpec(memory_space=pl.ANY),
        ],
        out_specs=pl.BlockSpec((tm, d), lambda i: (i, 0)),
        out_shape=jax.ShapeDtypeStruct((t, d), F32),
        scratch_shapes=[
            pltpu.VMEM((tm + SUBLANES_BF16, d), BF16),
            pltpu.VMEM((2, 2, hid_rows, tf), F32),
            pltpu.VMEM((tm, d), F32),
            pltpu.VMEM((2, 2, d, tf), BF16),
            pltpu.VMEM((2, tf, d), BF16),
            pltpu.SemaphoreType.DMA((2, 2)),
            pltpu.SemaphoreType.DMA((2,)),
        ],
        compiler_params=_params("parallel"),
        name="conv_ffn",
    )(h, h, h, nw_in.reshape(1, d), cw, cb, nw_out.reshape(1, d), w_in, w_out)


def _diff_attn_kernel(lp_ref, sub_ref, q_ref, k_ref, v_ref, o_ref, *, lambda_init):
    lp = lp_ref[...]
    lam = (jnp.exp(jnp.sum(lp[0:1] * lp[1:2], axis=-1, keepdims=True))
           - jnp.exp(jnp.sum(lp[2:3] * lp[3:4], axis=-1, keepdims=True)) + lambda_init)
    sub = sub_ref[...]
    n_sub = q_ref.shape[1] // ATT_SUB_ROWS

    def scores(i):
        q = q_ref[0, i * ATT_SUB_ROWS:(i + 1) * ATT_SUB_ROWS, :].astype(F32)
        q = (q * (HEAD_DIM ** -0.5 * math.log2(math.e))).astype(BF16)
        return [_dot_nt(q[:, lo:lo + HEAD_DIM], k_ref[0, :, lo:lo + HEAD_DIM]) for lo in (0, HEAD_DIM)]

    def weights(s12):
        es = [jnp.exp2(s - jnp.max(s, axis=-1, keepdims=True)) for s in s12]
        l1, l2 = [jnp.sum(e, axis=-1, keepdims=True) for e in es]
        return (es[0] - es[1] * (lam * l1 / l2)).astype(BF16), 1.0 / l1

    def finish(i, a_inv):
        a, inv_l1 = a_inv
        o = _dot(a, v_ref[0]) * inv_l1
        o_ref[0, i * ATT_SUB_ROWS:(i + 1) * ATT_SUB_ROWS, :] = (_rms(o, sub) * (1.0 - lambda_init)).astype(o_ref.dtype)

    s_next = scores(0)
    a_prev = None
    for i in range(n_sub):
        s_cur = s_next
        if i + 1 < n_sub:
            s_next = scores(i + 1)
        a_cur = weights(s_cur)
        if a_prev is not None:
            finish(i - 1, a_prev)
        a_prev = a_cur
    finish(n_sub - 1, a_prev)


def diff_attention(proj, lam_params, subln, layer, tq=ATT_Q_TILE):
    b, s, _ = proj.shape
    w = 2 * HEAD_DIM
    lambda_init = 0.8 - 0.6 * math.exp(-0.3 * layer)
    return pl.pallas_call(
        functools.partial(_diff_attn_kernel, lambda_init=lambda_init),
        grid=(b, HEADS, s // tq),
        in_specs=[
            pl.BlockSpec((4, HEAD_DIM), lambda bi, h, qi: (0, 0)),
            pl.BlockSpec((1, w), lambda bi, h, qi: (0, 0)),
            pl.BlockSpec((1, tq, w), lambda bi, h, qi: (bi, qi, h)),
            pl.BlockSpec((1, s, w), lambda bi, h, qi: (bi, 0, HEADS + h)),
            pl.BlockSpec((1, s, w), lambda bi, h, qi: (bi, 0, 2 * HEADS + h)),
        ],
        out_specs=pl.BlockSpec((1, tq, w), lambda bi, h, qi: (bi, qi, h)),
        out_shape=jax.ShapeDtypeStruct((b, s, C_QK_W), BF16),
        compiler_params=_params("parallel", "parallel", "arbitrary"),
        name="diff_attention",
    )(lam_params, subln.reshape(1, w), proj, proj, proj)


def _na_key_start_row(first_query_row):
    return (first_query_row - NA_ROWS // 2, 0, GRID_W - NA_SUB_K_ROWS)


def _na_bias(rpb):
    rows = GRID_W
    n_sub = NA_Q_ROWS // NA_SUB_Q_ROWS
    sel_r = np.zeros((3, n_sub, NA_SUB_Q_ROWS, NA_SUB_K_ROWS, 2 * NA_ROWS - 1), np.float32)
    for case, rb in enumerate((0, 1, rows // NA_Q_ROWS - 1)):
        for p in range(n_sub):
            r0 = rb * NA_Q_ROWS + p * NA_SUB_Q_ROWS
            base = int(np.clip(*_na_key_start_row(r0)))
            for ql in range(NA_SUB_Q_ROWS):
                r = r0 + ql
                row_start = int(np.clip(r - NA_ROWS // 2, 0, rows - NA_ROWS))
                assert base <= row_start and row_start + NA_ROWS <= base + NA_SUB_K_ROWS
                for kl in range(NA_SUB_K_ROWS):
                    kr = base + kl
                    if row_start <= kr < row_start + NA_ROWS:
                        sel_r[case, p, ql, kl, kr - r + NA_ROWS - 1] = 1.0
    c = np.arange(GRID_W)
    col_start = np.clip(c - NA_COLS // 2, 0, GRID_W - NA_COLS)
    col_ok = (c[None, :] >= col_start[:, None]) & (c[None, :] < col_start[:, None] + NA_COLS)
    dc = np.clip(c[None, :] - c[:, None] + NA_COLS - 1, 0, 2 * NA_COLS - 2)
    sel_c = (np.arange(2 * NA_COLS - 1)[:, None, None] == dc[None]) & col_ok[None]
    valid = (sel_r.sum(-1) > 0)[:, :, :, None, :, None] & col_ok[None, None, None, :, None, :]
    t1 = jnp.einsum("hab,cpqka->hcpqkb", rpb, sel_r, precision=lax.Precision.HIGHEST)
    t2 = jnp.einsum("hcpqkb,bxy->hcpqxky", t1, sel_c.astype(np.float32), precision=lax.Precision.HIGHEST)
    bias = jnp.where(valid[None], t2 * math.log2(math.e), -jnp.inf)
    return bias.reshape(rpb.shape[0], 3, n_sub, NA_SUB_Q_ROWS * GRID_W, NA_SUB_K_ROWS * GRID_W)


def _na_kernel(q_ref, k_ref, v_ref, b_ref, o_ref):
    rb = pl.program_id(2)
    nq = NA_SUB_Q_ROWS * GRID_W
    nk = NA_SUB_K_ROWS * GRID_W
    n_sub = NA_Q_ROWS // NA_SUB_Q_ROWS

    def key_start(p):
        row, lo, hi = _na_key_start_row(rb * NA_Q_ROWS + p * NA_SUB_Q_ROWS)
        return pl.multiple_of(jnp.clip(row, lo, hi) * GRID_W, NA_SUB_Q_ROWS * GRID_W)

    def scores(p):
        q = (q_ref[0, p * nq:(p + 1) * nq, :].astype(F32) * (HEAD_DIM ** -0.5 * math.log2(math.e))).astype(BF16)
        return _dot_nt(q, k_ref[0, pl.ds(key_start(p), nk), :]) + b_ref[0, 0, p]

    def weights(s):
        e = jnp.exp2(s - jnp.max(s, axis=-1, keepdims=True))
        return e.astype(BF16), 1.0 / jnp.sum(e, axis=-1, keepdims=True)

    def finish(p, e_inv):
        e, inv_l = e_inv
        o = _dot(e, v_ref[0, pl.ds(key_start(p), nk), :]) * inv_l
        o_ref[0, p * nq:(p + 1) * nq, :] = o.astype(o_ref.dtype)

    s_next = scores(0)
    e_prev = None
    for p in range(n_sub):
        s_cur = s_next
        if p + 1 < n_sub:
            s_next = scores(p + 1)
        e_cur = weights(s_cur)
        if e_prev is not None:
            finish(p - 1, e_prev)
        e_prev = e_cur
    finish(n_sub - 1, e_prev)


def neighbourhood_attention(proj, col0, bias):
    b, s, _ = proj.shape
    nq = NA_Q_ROWS * GRID_W
    nblk = s // nq
    c0 = col0 // HEAD_DIM

    def bias_map(bi, h, rb):
        return (h, jnp.where(rb == 0, 0, jnp.where(rb == nblk - 1, 2, 1)), 0, 0, 0)

    return pl.pallas_call(
        _na_kernel,
        grid=(b, HEADS, nblk),
        in_specs=[
            pl.BlockSpec((1, nq, HEAD_DIM), lambda bi, h, rb: (bi, rb, c0 + h)),
            pl.BlockSpec((1, s, HEAD_DIM), lambda bi, h, rb: (bi, 0, c0 + HEADS + h)),
            pl.BlockSpec((1, s, HEAD_DIM), lambda bi, h, rb: (bi, 0, c0 + 2 * HEADS + h)),
            pl.BlockSpec((1, 1) + bias.shape[2:], bias_map),
        ],
        out_specs=pl.BlockSpec((1, nq, HEAD_DIM), lambda bi, h, rb: (bi, rb, h)),
        out_shape=jax.ShapeDtypeStruct((b, s, B_W), BF16),
        compiler_params=_params("parallel", "parallel", "arbitrary"),
        name="neighbourhood_attention",
    )(proj, proj, proj, bias)


HALO = SUBLANES_BF16


def _split3(x):
    hi = x.astype(BF16)
    r1 = x - hi.astype(F32)
    mid = r1.astype(BF16)
    lo = (r1 - mid.astype(F32)).astype(BF16)
    return hi, mid, lo


def _dn_prep_kernel(x_ref, xp_ref, xx_ref, cw_ref, gt_ref, alog_ref, dtb_ref,
                    q_ref, k_ref, v_ref, gc_ref, beta_ref, xs_ref):
    i = pl.program_id(1)
    tb = x_ref.shape[1]
    pad = A_CONV_W // 2
    xs_ref[0:HALO, :] = jnp.where(i == 0, 0.0, xp_ref[0].astype(F32))
    xs_ref[HALO:HALO + tb, :] = x_ref[0].astype(F32)
    xs_ref[HALO + tb:, :] = jnp.where(i == pl.num_programs(1) - 1, 0.0, xx_ref[0].astype(F32))
    cw = cw_ref[...]
    y = xs_ref[pl.ds(HALO - pad, tb), :] * cw[0:1]
    for t in range(1, A_CONV_W):
        y = y + xs_ref[pl.ds(HALO - pad + t, tb), :] * cw[t:t + 1]
    y = y * _sigmoid(y)
    for h in range(HEADS):
        qh = y[:, h * HEAD_DIM:(h + 1) * HEAD_DIM]
        kh = y[:, (HEADS + h) * HEAD_DIM:(HEADS + h + 1) * HEAD_DIM]
        qn = qh * (lax.rsqrt(jnp.sum(qh * qh, axis=-1, keepdims=True) + 1e-6) * (HEAD_DIM ** -0.5))
        kn = kh * lax.rsqrt(jnp.sum(kh * kh, axis=-1, keepdims=True) + 1e-6)
        q_ref[0, h] = qn.astype(q_ref.dtype)
        k_ref[0, h] = kn.astype(k_ref.dtype)
        v_ref[0, h] = y[:, (2 * HEADS + h) * HEAD_DIM:(2 * HEADS + h + 1) * HEAD_DIM].astype(v_ref.dtype)

    nd = 2 * HEADS
    gates = gt_ref[0]
    z = gates[:, 0:nd] + dtb_ref[...]
    softplus = jnp.maximum(z, 0.0) + jnp.log(1.0 + jnp.exp(-jnp.abs(z)))
    g = -jnp.exp(alog_ref[...]) * softplus
    beta_ref[0] = _sigmoid(gates[:, nd:2 * nd])
    r = lax.broadcasted_iota(jnp.int32, (tb, tb), 0)
    c = lax.broadcasted_iota(jnp.int32, (tb, tb), 1)
    lower = jnp.where(r >= c, 1.0, 0.0).astype(BF16)
    upper = jnp.where(r <= c, 1.0, 0.0).astype(BF16)
    parts = _split3(g)
    fwd = _dot(lower, parts[0]) + _dot(lower, parts[1]) + _dot(lower, parts[2])
    bwd = _dot(upper, parts[0]) + _dot(upper, parts[1]) + _dot(upper, parts[2])
    col = lax.broadcasted_iota(jnp.int32, (tb, nd), 1)
    gc_ref[0] = jnp.where(col < HEADS, fwd, bwd)


def deltanet_prep(proj, gates, conv_w, a_log, dt_bias, tb=DN_BLOCK):
    b, s, _ = proj.shape
    nb = s // tb
    rh = tb // HALO
    nd = 2 * HEADS
    hs = jax.ShapeDtypeStruct((b, HEADS, s, HEAD_DIM), BF16)
    head_spec = pl.BlockSpec((1, HEADS, tb, HEAD_DIM), lambda bi, i: (bi, 0, i, 0))
    vec_spec = pl.BlockSpec((1, tb, nd), lambda bi, i: (bi, i, 0))
    return pl.pallas_call(
        _dn_prep_kernel,
        grid=(b, nb),
        in_specs=[
            pl.BlockSpec((1, tb, A_QKV_W), lambda bi, i: (bi, i, 0)),
            pl.BlockSpec((1, HALO, A_QKV_W), lambda bi, i: (bi, jnp.maximum(i * rh - 1, 0), 0)),
            pl.BlockSpec((1, HALO, A_QKV_W), lambda bi, i: (bi, jnp.minimum((i + 1) * rh, s // HALO - 1), 0)),
            pl.BlockSpec((A_CONV_W, A_QKV_W), lambda bi, i: (0, 0)),
            pl.BlockSpec((1, tb, LANES), lambda bi, i: (bi, i, 0)),
            pl.BlockSpec((1, nd), lambda bi, i: (0, 0)),
            pl.BlockSpec((1, nd), lambda bi, i: (0, 0)),
        ],
        out_specs=[head_spec, head_spec, head_spec, vec_spec, vec_spec],
        out_shape=[hs, hs, hs, jax.ShapeDtypeStruct((b, s, nd), F32), jax.ShapeDtypeStruct((b, s, nd), F32)],
        scratch_shapes=[pltpu.VMEM((tb + 2 * HALO, A_QKV_W), F32)],
        compiler_params=_params("parallel", "arbitrary"),
        name="deltanet_prep",
    )(proj, proj, proj, conv_w, gates, a_log.reshape(1, nd), dt_bias.reshape(1, nd))


def _split2(x):
    hi = x.astype(BF16)
    return hi, (x - hi.astype(F32)).astype(BF16)


def _unit_triangular_solve(ms, rhss, same_block):
    n = ms[0].shape[0]
    grp = DN_GROUP
    ng = n // grp
    r = lax.broadcasted_iota(jnp.int32, (grp, grp), 0)
    c = lax.broadcasted_iota(jnp.int32, (grp, grp), 1)
    base_mask = (r // DN_BASE) == (c // DN_BASE)
    row = lax.broadcasted_iota(jnp.int32, (DN_BASE, grp), 0)
    lane = lax.broadcasted_iota(jnp.int32, (DN_BASE, grp), 1)
    left = lane < DN_BASE

    def pair_lhs(hi, lo):
        hi_f, lo_f = hi.astype(F32), lo.astype(F32)
        swapped = pltpu.roll(hi_f, DN_BASE, 1)
        top = jnp.concatenate([jnp.where(left, hi_f, swapped), jnp.where(left, lo_f, 0.0)], axis=1)
        bottom = jnp.concatenate([jnp.where(left, swapped, hi_f), jnp.where(left, 0.0, lo_f)], axis=1)
        return jnp.concatenate([top, bottom], axis=0).astype(BF16)

    def pair_product(lhs, rhs_hi, rhs_lo):
        res = _dot(lhs, jnp.concatenate([rhs_hi, rhs_lo, rhs_hi, rhs_hi], axis=0))
        return jnp.where(left, res[:DN_BASE], res[DN_BASE:])

    pieces = [jnp.where(base_mask, m[g * grp:(g + 1) * grp, g * grp:(g + 1) * grp], 0.0)
              for m in ms for g in range(ng)]
    pws = [piece[:DN_BASE] + piece[DN_BASE:] for piece in pieces]
    ts = [jnp.where(lane % DN_BASE == row, 1.0, 0.0) + pw for pw in pws]
    splits = [_split2(pw) for pw in pws]
    lhss = [pair_lhs(hi, lo) for hi, lo in splits]
    size = 2
    while size < DN_BASE:
        pws = [pair_product(lhs, hi, lo) for lhs, (hi, lo) in zip(lhss, splits)]
        splits = [_split2(pw) for pw in pws]
        lhss = [pair_lhs(hi, lo) for hi, lo in splits]
        ts = [t + pair_product(lhs, *_split2(t)) for lhs, t in zip(lhss, ts)]
        size *= 2
    t_bases = [jnp.concatenate([jnp.where(left, t, 0.0), jnp.where(left, 0.0, t)], axis=0).astype(BF16)
               for t in ts]
    offs = {}
    blk = 2 * DN_BASE
    while blk <= n:
        offs[blk] = [jnp.where(same_block(blk), jnp.where(same_block(blk // 2), 0.0, m), 0.0).astype(BF16)
                     for m in ms]
        blk *= 2

    def apply(blk, ys):
        if blk == DN_BASE:
            ybs = [y.astype(BF16) for y in ys]
            return [jnp.concatenate([_dot(t_bases[i * ng + g], yb[g * grp:(g + 1) * grp]) for g in range(ng)],
                                    axis=0) for i, yb in enumerate(ybs)]
        zs = apply(blk // 2, ys)
        corr = apply(blk // 2, [_dot(off, z.astype(BF16)) for off, z in zip(offs[blk], zs)])
        return [z + cr for z, cr in zip(zs, corr)]

    return apply(n, rhss)


def _dn_block(chains):
    n = chains[0][0].shape[0]
    r = lax.broadcasted_iota(jnp.int32, (n, n), 0)
    c = lax.broadcasted_iota(jnp.int32, (n, n), 1)

    def same_block(size):
        return (r // size) == (c // size)

    ms, rhss, pre = [], [], []
    for q, k, v, gcol, grow, beta, state_ref, forward in chains:
        d = (r - c) if forward else (c - r)
        kf = k.astype(F32)
        kb = kf * beta
        decay = jnp.exp(jnp.where(d >= 0, gcol - grow, NEG_BIG))
        ms.append(jnp.where(d > 0, -(_dot_nt(kb.astype(BF16), k) * decay), 0.0))
        intra = (_dot_nt(q, k) * decay).astype(BF16)
        eg = jnp.exp(gcol)
        rhss.append(jnp.concatenate([v.astype(F32) * beta, kb * eg], axis=1))
        g_last = grow[:, n - 1:n] if forward else grow[:, 0:1]
        pre.append((intra, (q.astype(F32) * eg).astype(BF16), (kf * jnp.exp(g_last - gcol)).astype(BF16),
                    jnp.exp(g_last)))
    sols = _unit_triangular_solve(ms, rhss, same_block)
    outs = []
    for chain, sol, (intra, q_dec, k_dec, blk_decay) in zip(chains, sols, pre):
        state_ref = chain[6]
        state = state_ref[...]
        sb = state.astype(BF16)
        v_new = sol[:, :HEAD_DIM] - _dot(sol[:, HEAD_DIM:].astype(BF16), sb)
        vb = v_new.astype(BF16)
        outs.append(_dot(q_dec, sb) + _dot(intra, vb))
        state_ref[...] = state * blk_decay + _dot_tn(k_dec, vb)
    return outs


def _dn_kernel(qf_ref, kf_ref, vf_ref, qb_ref, kb_ref, vb_ref, gcf_ref, gcb_ref, btf_ref, btb_ref,
               grf_ref, grb_ref, of_ref, ob_ref, state_ref):
    hb = qf_ref.shape[1]

    @pl.when(pl.program_id(2) == 0)
    def _():
        state_ref[...] = jnp.zeros_like(state_ref)

    lane = lax.broadcasted_iota(jnp.int32, gcf_ref.shape[1:], 1)

    def pick(ref, idx):
        return jnp.sum(jnp.where(lane == idx, ref[0], 0.0), axis=-1, keepdims=True)

    chains = []
    for hl in range(hb):
        h = pl.program_id(1) * hb + hl
        chains.append((qf_ref[0, hl], kf_ref[0, hl], vf_ref[0, hl], pick(gcf_ref, h),
                       grf_ref[0, pl.ds(h, 1), :], pick(btf_ref, h), state_ref.at[0, hl], True))
        chains.append((qb_ref[0, hl], kb_ref[0, hl], vb_ref[0, hl], pick(gcb_ref, HEADS + h),
                       grb_ref[0, pl.ds(HEADS + h, 1), :], pick(btb_ref, HEADS + h), state_ref.at[1, hl], False))
    outs = _dn_block(chains)
    for hl in range(hb):
        of_ref[0, hl] = outs[2 * hl]
        ob_ref[0, hl] = outs[2 * hl + 1]


def deltanet_scan(q, k, v, gc, beta, gc_rows, tb=DN_BLOCK, hb=DN_HEADS_PER_STEP):
    b, _, s, _ = q.shape
    nb = s // tb
    nd = 2 * HEADS
    fwd = pl.BlockSpec((1, hb, tb, HEAD_DIM), lambda bi, h, c: (bi, h, c, 0))
    bwd = pl.BlockSpec((1, hb, tb, HEAD_DIM), lambda bi, h, c: (bi, h, nb - 1 - c, 0))
    vec_f = pl.BlockSpec((1, tb, nd), lambda bi, h, c: (bi, c, 0))
    vec_b = pl.BlockSpec((1, tb, nd), lambda bi, h, c: (bi, nb - 1 - c, 0))
    row_f = pl.BlockSpec((1, nd, tb), lambda bi, h, c: (bi, 0, c))
    row_b = pl.BlockSpec((1, nd, tb), lambda bi, h, c: (bi, 0, nb - 1 - c))
    os_ = jax.ShapeDtypeStruct((b, HEADS, s, HEAD_DIM), F32)
    return pl.pallas_call(
        _dn_kernel,
        grid=(b, HEADS // hb, nb),
        in_specs=[fwd, fwd, fwd, bwd, bwd, bwd, vec_f, vec_b, vec_f, vec_b, row_f, row_b],
        out_specs=[fwd, bwd],
        out_shape=[os_, os_],
        scratch_shapes=[pltpu.VMEM((2, hb, HEAD_DIM, HEAD_DIM), F32)],
        compiler_params=_params("parallel", "parallel", "arbitrary"),
        name="deltanet_scan",
    )(q, k, v, q, k, v, gc, gc, beta, beta, gc_rows, gc_rows)


def _dn_out_kernel(of_ref, ob_ref, z_ref, nw_ref, o_ref):
    nw = nw_ref[...]
    for h in range(HEADS):
        o = _rms(of_ref[0, h] + ob_ref[0, h], nw)
        z = z_ref[0, :, h * HEAD_DIM:(h + 1) * HEAD_DIM].astype(F32)
        o_ref[0, :, h * HEAD_DIM:(h + 1) * HEAD_DIM] = (o * (z * _sigmoid(z))).astype(o_ref.dtype)


def deltanet_out(o_f, o_b, proj, z_col0, out_norm, ts=ROW_TILE):
    b, _, s, _ = o_f.shape
    head_spec = pl.BlockSpec((1, HEADS, ts, HEAD_DIM), lambda bi, i: (bi, 0, i, 0))
    zb = z_col0 // A_V_W
    return pl.pallas_call(
        _dn_out_kernel,
        grid=(b, s // ts),
        in_specs=[head_spec, head_spec,
                  pl.BlockSpec((1, ts, A_V_W), lambda bi, i: (bi, i, zb)),
                  pl.BlockSpec((1, HEAD_DIM), lambda bi, i: (0, 0))],
        out_specs=pl.BlockSpec((1, ts, A_V_W), lambda bi, i: (bi, i, 0)),
        out_shape=jax.ShapeDtypeStruct((b, s, A_V_W), BF16),
        compiler_params=_params("parallel", "parallel"),
        name="deltanet_out",
    )(o_f, o_b, proj, out_norm.reshape(1, HEAD_DIM))


def _rope_tables(s):
    half = ROPE_DIMS // 2
    inv = ROPE_THETA ** (-jnp.arange(0, ROPE_DIMS, 2, dtype=F32) / ROPE_DIMS)
    ang = jnp.arange(s, dtype=F32)[:, None] * inv[None, :]
    cos, sin = jnp.cos(ang), jnp.sin(ang)
    rest = HEAD_DIM - ROPE_DIMS
    c = jnp.concatenate([cos, cos, jnp.ones((s, rest), F32)], axis=1)
    s1 = jnp.concatenate([jnp.zeros((s, half), F32), sin, jnp.zeros((s, rest), F32)], axis=1)
    s2 = jnp.concatenate([-sin, jnp.zeros((s, half + rest), F32)], axis=1)
    return c, s1, s2


def _trunk(x_a, x_b, p_a, p_b, ab_w_in, ab_conv_w, ab_a_log, ab_dt_bias, ab_out_norm, ab_rpb, ab_w_out,
           c_w_in, c_lambda, c_subln, c_w_out, norms, ffn_w_in, ffn_conv_w, ffn_conv_b,
           ffn_w_out, ple_w_proj, ple_w_gate):
    b_a, s, d = x_a.shape
    b = b_a + x_b.shape[0]
    t, t_a = b * s, b_a * s
    depth = norms.shape[0]
    h = jnp.concatenate([x_a, x_b], axis=0).reshape(t, d)
    rope = _rope_tables(s)
    o1 = A_QKV_W
    o2 = o1 + A_V_W
    o3 = o2 + 4 * HEADS
    ab_main = _column_tiles(jnp.concatenate([ab_w_in[:, :, :o2], ab_w_in[:, :, o3:]], axis=2), COL_TILE)
    ab_gate = _column_tiles(jnp.pad(ab_w_in[:, :, o2:o3], ((0, 0), (0, 0), (0, LANES - 4 * HEADS))), LANES)
    c_in = _column_tiles(c_w_in, COL_TILE)
    ffn_in = _column_tiles(ffn_w_in, FF_TILE)
    ab_out, c_out, ffn_out = ab_w_out.astype(BF16), c_w_out.astype(BF16), ffn_w_out.astype(BF16)
    ple_gate, ple_proj = ple_w_gate.astype(BF16), ple_w_proj.astype(BF16)
    for layer in range(depth):
        j = layer // 2
        if layer % 2 == 0:
            proj = norm_matmul(h, norms[layer, 0], ab_main, j, BF16).reshape(b, s, -1)
            gates = norm_matmul(h, norms[layer, 0], ab_gate, j, F32).reshape(b, s, LANES)
            qa, ka, va, gc, beta = deltanet_prep(proj, gates, ab_conv_w[j], ab_a_log[j], ab_dt_bias[j])
            o_f, o_b = deltanet_scan(qa, ka, va, gc, beta, jnp.transpose(gc, (0, 2, 1)))
            o_a = deltanet_out(o_f, o_b, proj, o1, ab_out_norm[j])
            o_nb = neighbourhood_attention(proj, o2, _na_bias(ab_rpb[j]))
            h = proj_norm_res([o_a.reshape(t, -1), o_nb.reshape(t, -1)], ab_out, j, norms[layer, 1], h)
        else:
            proj = norm_matmul(h, norms[layer, 0], c_in, j, BF16, rope=rope,
                               rope_tiles=2 * C_QK_W // COL_TILE, seq_len=s).reshape(b, s, -1)
            o_c = diff_attention(proj, c_lambda[j], c_subln[j], layer)
            h = proj_norm_res([o_c.reshape(t, -1)], c_out, j, norms[layer, 1], h)
        h = conv_ffn(h, norms[layer, 2], ffn_in, ffn_out, layer, ffn_conv_w[layer], ffn_conv_b[layer],
                     norms[layer, 3], s)
        ple_args = (p_a[layer].reshape(t_a, -1), p_b[layer].reshape(t - t_a, -1), ple_gate, ple_proj, layer)
        if layer + 1 < depth:
            h = ple(h, *ple_args)
    y_a = ple(h, *ple_args, row0=0, rows=t_a)
    y_b = ple(h, *ple_args, row0=t_a, rows=t - t_a)
    return y_a.reshape(b_a, s, d), y_b.reshape(b - b_a, s, d)


def kernel(x_prompt, x_sample, p_prompt, p_sample, ab_w_in, ab_conv_w, ab_a_log, ab_dt_bias, ab_out_norm,
           ab_rpb, ab_w_out, c_w_in, c_lambda, c_subln, c_w_out, norms, ffn_w_in, ffn_conv_w, ffn_conv_b,
           ffn_w_out, ple_w_proj, ple_w_gate):
    return _trunk(x_prompt, x_sample, p_prompt, p_sample, ab_w_in, ab_conv_w, ab_a_log, ab_dt_bias,
                  ab_out_norm, ab_rpb, ab_w_out, c_w_in, c_lambda, c_subln, c_w_out, norms, ffn_w_in,
                  ffn_conv_w, ffn_conv_b, ffn_w_out, ple_w_proj, ple_w_gate)
```

```python
import functools
import math

import jax
import jax.numpy as jnp
import numpy as np
from jax import lax
from jax.experimental import pallas as pl
from jax.experimental.pallas import tpu as pltpu

F32 = jnp.float32
BF16 = jnp.bfloat16

D_MODEL = 2048
PLE_DIM = 256
GRID_W = 64
RMS_EPS = 1e-6
HEADS = 8
HEAD_DIM = 128
A_CONV_W = 5
NA_ROWS = 8
NA_COLS = 16
ROPE_THETA = 500000.0
ROPE_DIMS = HEAD_DIM // 4
D_FF = 8192
FFN_CONV_W = 3
A_QKV_W = 3 * HEADS * HEAD_DIM
A_V_W = HEADS * HEAD_DIM
B_W = HEADS * HEAD_DIM
C_QK_W = HEADS * 2 * HEAD_DIM

LANES = 128
SUBLANES_BF16 = 16
VMEM_LIMIT = 56 * 1024 * 1024
NEG_BIG = -1e30

ROW_TILE = 512
MM_ROW_TILE = 1024
COL_TILE = 1024
FF_TILE = 512
DN_BLOCK = 256
DN_BASE = 64
DN_GROUP = 128
DN_HEADS_PER_STEP = 4
ATT_Q_TILE = 512
ATT_SUB_ROWS = 256
NA_Q_ROWS = 8
NA_HEADS_PER_STEP = 4
NA_SUB_Q_ROWS = 4
NA_SUB_K_ROWS = 12


def _params(*sem):
    return pltpu.CompilerParams(dimension_semantics=sem, vmem_limit_bytes=VMEM_LIMIT)


def _rms(x, w):
    return x * lax.rsqrt(jnp.mean(x * x, axis=-1, keepdims=True) + RMS_EPS) * w


def _dot(a, b):
    return jnp.dot(a, b, preferred_element_type=F32)


def _dot_nt(a, b):
    return lax.dot_general(a, b, (((1,), (1,)), ((), ())), preferred_element_type=F32)


def _dot_tn(a, b):
    return lax.dot_general(a, b, (((0,), (0,)), ((), ())), preferred_element_type=F32)


def _sigmoid(x):
    return 1.0 / (1.0 + jnp.exp(-x))


def _norm_matmul_kernel(*refs, rope_tiles):
    if rope_tiles:
        x_ref, nw_ref, w_ref, rc_ref, rs1_ref, rs2_ref, o_ref, xn_ref = refs
    else:
        x_ref, nw_ref, w_ref, o_ref, xn_ref = refs
    j = pl.program_id(1)

    @pl.when(j == 0)
    def _():
        xn_ref[...] = _rms(x_ref[...], nw_ref[...]).astype(BF16)

    y = _dot(xn_ref[...], w_ref[0, 0])
    if not rope_tiles:
        o_ref[...] = y.astype(o_ref.dtype)
        return

    @pl.when(j < rope_tiles)
    def _():
        n = y.shape[1]
        reps = n // LANES
        c = jnp.concatenate([rc_ref[...]] * reps, axis=1)
        s1 = jnp.concatenate([rs1_ref[...]] * reps, axis=1)
        s2 = jnp.concatenate([rs2_ref[...]] * reps, axis=1)
        half = ROPE_DIMS // 2
        r = y * c + pltpu.roll(y, half, 1) * s1 + pltpu.roll(y, n - half, 1) * s2
        o_ref[...] = r.astype(o_ref.dtype)

    @pl.when(j >= rope_tiles)
    def _():
        o_ref[...] = y.astype(o_ref.dtype)


def _column_tiles(w, tn):
    nl, k, n = w.shape
    return jnp.transpose(w.reshape(nl, k, n // tn, tn), (0, 2, 1, 3)).astype(BF16)


def norm_matmul(x, nw, w_tiles, layer, out_dtype, rope=None, rope_tiles=0, seq_len=None, tm=MM_ROW_TILE):
    t, k = x.shape
    _, n_tiles, _, tn = w_tiles.shape
    n = n_tiles * tn
    tm = min(tm, t)
    assert t % tm == 0
    in_specs = [
        pl.BlockSpec((tm, k), lambda i, j: (i, 0)),
        pl.BlockSpec((1, k), lambda i, j: (0, 0)),
        pl.BlockSpec((1, 1, k, tn), lambda i, j: (layer, j, 0, 0)),
    ]
    args = [x, nw.reshape(1, k), w_tiles]
    if rope_tiles:
        per_seq = seq_len // tm
        for tab in rope:
            in_specs.append(pl.BlockSpec((tm, LANES), lambda i, j: (i % per_seq, 0)))
            args.append(tab)
    return pl.pallas_call(
        functools.partial(_norm_matmul_kernel, rope_tiles=rope_tiles),
        grid=(t // tm, n // tn),
        in_specs=in_specs,
        out_specs=pl.BlockSpec((tm, tn), lambda i, j: (i, j)),
        out_shape=jax.ShapeDtypeStruct((t, n), out_dtype),
        scratch_shapes=[pltpu.VMEM((tm, k), BF16)],
        compiler_params=_params("parallel", "arbitrary"),
        name="norm_matmul",
    )(*args)


def _proj_norm_res_kernel(*refs, n_in):
    a_refs = refs[:n_in]
    w_refs = refs[n_in:2 * n_in]
    nw_ref, h_ref, o_ref = refs[2 * n_in:]
    y = _dot(a_refs[0][...], w_refs[0][0])
    for a_ref, w_ref in zip(a_refs[1:], w_refs[1:]):
        y = y + _dot(a_ref[...], w_ref[0])
    o_ref[...] = h_ref[...] + _rms(y, nw_ref[...])


def proj_norm_res(acts, w, layer, nw, h, tm=ROW_TILE):
    t, n = h.shape
    n_in = len(acts)
    kw = acts[0].shape[1]
    assert all(a.shape[1] == kw for a in acts) and w.shape[1] == n_in * kw
    in_specs = [pl.BlockSpec((tm, kw), lambda i: (i, 0)) for _ in acts]
    in_specs += [pl.BlockSpec((1, kw, n), functools.partial(lambda piece, i: (layer, piece, 0), piece))
                 for piece in range(n_in)]
    in_specs += [pl.BlockSpec((1, n), lambda i: (0, 0)), pl.BlockSpec((tm, n), lambda i: (i, 0))]
    ws = [w] * n_in
    return pl.pallas_call(
        functools.partial(_proj_norm_res_kernel, n_in=n_in),
        grid=(t // tm,),
        in_specs=in_specs,
        out_specs=pl.BlockSpec((tm, n), lambda i: (i, 0)),
        out_shape=jax.ShapeDtypeStruct((t, n), F32),
        compiler_params=_params("parallel"),
        name="proj_norm_res",
    )(*acts, *ws, nw.reshape(1, n), h)


def _ple_kernel(h_ref, pa_ref, pb_ref, wg_ref, wp_ref, o_ref, *, first_block, a_blocks):
    h = h_ref[...]
    gate = _sigmoid(_dot(h.astype(BF16), wg_ref[0]))
    p = jnp.where(first_block + pl.program_id(0) < a_blocks, pa_ref[0], pb_ref[0])
    o_ref[...] = h + gate * _dot(p.astype(BF16), wp_ref[0])


def ple(h, p_a, p_b, wg, wp, layer, row0=0, rows=None, tm=ROW_TILE):
    n = h.shape[1]
    rows = h.shape[0] if rows is None else rows
    assert row0 % tm == 0 and rows % tm == 0 and p_a.shape[1] % tm == 0 and p_b.shape[1] % tm == 0
    i0 = row0 // tm
    na, nb = p_a.shape[1] // tm, p_b.shape[1] // tm
    return pl.pallas_call(
        functools.partial(_ple_kernel, first_block=i0, a_blocks=na),
        grid=(rows // tm,),
        in_specs=[
            pl.BlockSpec((tm, n), lambda i: (i0 + i, 0)),
            pl.BlockSpec((1, tm, p_a.shape[2]), lambda i: (layer, jnp.minimum(i0 + i, na - 1), 0)),
            pl.BlockSpec((1, tm, p_b.shape[2]), lambda i: (layer, jnp.clip(i0 + i - na, 0, nb - 1), 0)),
            pl.BlockSpec((1,) + wg.shape[1:], lambda i: (layer, 0, 0)),
            pl.BlockSpec((1,) + wp.shape[1:], lambda i: (layer, 0, 0)),
        ],
        out_specs=pl.BlockSpec((tm, n), lambda i: (i, 0)),
        out_shape=jax.ShapeDtypeStruct((rows, n), F32),
        compiler_params=_params("parallel"),
        name="ple",
    )(h, p_a, p_b, wg, wp)


def _ffn_kernel(x_ref, xp_ref, xx_ref, nw_ref, wg_ref, wu_ref, cwg_ref, cwu_ref, cbg_ref, cbu_ref,
                wo_ref, nw2_ref, o_ref, xn_ref, hg_ref, hu_ref, acc_ref, *, tm, per_seq):
    i = pl.program_id(0)
    j = pl.program_id(1)

    @pl.when(j == 0)
    def _():
        nw = nw_ref[...]
        pos = i % per_seq
        prev = jnp.where(pos == 0, 0.0, _rms(xp_ref[...], nw))
        nxt = jnp.where(pos == per_seq - 1, 0.0, _rms(xx_ref[...], nw))
        xn_ref[0:tm, :] = _rms(x_ref[...], nw).astype(BF16)
        xn_ref[tm:, :] = jnp.concatenate([nxt, prev], axis=0).astype(BF16)
        acc_ref[...] = jnp.zeros_like(acc_ref)

    xe = xn_ref[...]

    def conv(w_ref, h_ref, cw_ref, cb_ref):
        hid = _dot(xe, w_ref[0, 0])
        h_ref[8:, :] = hid
        h_ref[0:8, :] = hid[tm + 8:, :]
        cw = cw_ref[...]
        return (h_ref[pl.ds(7, tm), :] * cw[0:1] + h_ref[pl.ds(8, tm), :] * cw[1:2]
                + h_ref[pl.ds(9, tm), :] * cw[2:3] + cb_ref[...])

    g = conv(wg_ref, hg_ref, cwg_ref, cbg_ref)
    u = conv(wu_ref, hu_ref, cwu_ref, cbu_ref)
    c0 = math.sqrt(2.0 / math.pi)
    gelu = 0.5 * g * (1.0 + jnp.tanh(c0 * (g + 0.044715 * (g * g * g))))
    acc_ref[...] += _dot((gelu * u).astype(BF16), wo_ref[0])

    @pl.when(j == pl.num_programs(1) - 1)
    def _():
        o_ref[...] = x_ref[...] + _rms(acc_ref[...], nw2_ref[...])


def conv_ffn(h, nw_in, w_in, w_out, layer, conv_w, conv_b, nw_out, seq_len, tm=ROW_TILE):
    t, d = h.shape
    f = w_out.shape[1]
    tf = w_in.shape[3]
    nf = f // tf
    per_seq = seq_len // tm
    r8 = tm // 8
    last8 = t // 8 - 1
    cb = conv_b.reshape(1, 2 * f)
    return pl.pallas_call(
        functools.partial(_ffn_kernel, tm=tm, per_seq=per_seq),
        grid=(t // tm, nf),
        in_specs=[
            pl.BlockSpec((tm, d), lambda i, j: (i, 0)),
            pl.BlockSpec((8, d), lambda i, j: (jnp.maximum(i * r8 - 1, 0), 0)),
            pl.BlockSpec((8, d), lambda i, j: (jnp.minimum((i + 1) * r8, last8), 0)),
            pl.BlockSpec((1, d), lambda i, j: (0, 0)),
            pl.BlockSpec((1, 1, d, tf), lambda i, j: (layer, j, 0, 0)),
            pl.BlockSpec((1, 1, d, tf), lambda i, j: (layer, j + nf, 0, 0)),
            pl.BlockSpec((FFN_CONV_W, tf), lambda i, j: (0, j)),
            pl.BlockSpec((FFN_CONV_W, tf), lambda i, j: (0, j + nf)),
            pl.BlockSpec((1, tf), lambda i, j: (0, j)),
            pl.BlockSpec((1, tf), lambda i, j: (0, j + nf)),
            pl.BlockSpec((1, tf, d), lambda i, j: (layer, j, 0)),
            pl.BlockSpec((1, d), lambda i, j: (0, 0)),
        ],
        out_specs=pl.BlockSpec((tm, d), lambda i, j: (i, 0)),
        out_shape=jax.ShapeDtypeStruct((t, d), F32),
        scratch_shapes=[
            pltpu.VMEM((tm + SUBLANES_BF16, d), BF16),
            pltpu.VMEM((8 + tm + SUBLANES_BF16, tf), F32),
            pltpu.VMEM((8 + tm + SUBLANES_BF16, tf), F32),
            pltpu.VMEM((tm, d), F32),
        ],
        compiler_params=_params("parallel", "arbitrary"),
        name="conv_ffn",
    )(h, h, h, nw_in.reshape(1, d), w_in, w_in, conv_w, conv_w, cb, cb, w_out, nw_out.reshape(1, d))


def _diff_attn_kernel(lp_ref, sub_ref, q_ref, k_ref, v_ref, o_ref, *, lambda_init):
    lp = lp_ref[...]
    lam = (jnp.exp(jnp.sum(lp[0:1] * lp[1:2], axis=-1, keepdims=True))
           - jnp.exp(jnp.sum(lp[2:3] * lp[3:4], axis=-1, keepdims=True)) + lambda_init)
    sub = sub_ref[...]
    n_sub = q_ref.shape[1] // ATT_SUB_ROWS

    def scores(i):
        q = q_ref[0, i * ATT_SUB_ROWS:(i + 1) * ATT_SUB_ROWS, :].astype(F32)
        q = (q * (HEAD_DIM ** -0.5 * math.log2(math.e))).astype(BF16)
        return [_dot_nt(q[:, lo:lo + HEAD_DIM], k_ref[0, :, lo:lo + HEAD_DIM]) for lo in (0, HEAD_DIM)]

    def weights(s12):
        es = [jnp.exp2(s - jnp.max(s, axis=-1, keepdims=True)) for s in s12]
        l1, l2 = [jnp.sum(e, axis=-1, keepdims=True) for e in es]
        return (es[0] - es[1] * (lam * l1 / l2)).astype(BF16), 1.0 / l1

    def finish(i, a_inv):
        a, inv_l1 = a_inv
        o = _dot(a, v_ref[0]) * inv_l1
        o_ref[0, i * ATT_SUB_ROWS:(i + 1) * ATT_SUB_ROWS, :] = (_rms(o, sub) * (1.0 - lambda_init)).astype(o_ref.dtype)

    s_next = scores(0)
    a_prev = None
    for i in range(n_sub):
        s_cur = s_next
        if i + 1 < n_sub:
            s_next = scores(i + 1)
        a_cur = weights(s_cur)
        if a_prev is not None:
            finish(i - 1, a_prev)
        a_prev = a_cur
    finish(n_sub - 1, a_prev)


def diff_attention(proj, lam_params, subln, layer, tq=ATT_Q_TILE):
    b, s, _ = proj.shape
    w = 2 * HEAD_DIM
    lambda_init = 0.8 - 0.6 * math.exp(-0.3 * layer)
    return pl.pallas_call(
        functools.partial(_diff_attn_kernel, lambda_init=lambda_init),
        grid=(b, HEADS, s // tq),
        in_specs=[
            pl.BlockSpec((4, HEAD_DIM), lambda bi, h, qi: (0, 0)),
            pl.BlockSpec((1, w), lambda bi, h, qi: (0, 0)),
            pl.BlockSpec((1, tq, w), lambda bi, h, qi: (bi, qi, h)),
            pl.BlockSpec((1, s, w), lambda bi, h, qi: (bi, 0, HEADS + h)),
            pl.BlockSpec((1, s, w), lambda bi, h, qi: (bi, 0, 2 * HEADS + h)),
        ],
        out_specs=pl.BlockSpec((1, tq, w), lambda bi, h, qi: (bi, qi, h)),
        out_shape=jax.ShapeDtypeStruct((b, s, C_QK_W), BF16),
        compiler_params=_params("parallel", "parallel", "arbitrary"),
        name="diff_attention",
    )(lam_params, subln.reshape(1, w), proj, proj, proj)


def _na_key_start_row(first_query_row):
    return (first_query_row - NA_ROWS // 2, 0, GRID_W - NA_SUB_K_ROWS)


def _na_bias(rpb):
    rows = GRID_W
    n_sub = NA_Q_ROWS // NA_SUB_Q_ROWS
    sel_r = np.zeros((3, n_sub, NA_SUB_Q_ROWS, NA_SUB_K_ROWS, 2 * NA_ROWS - 1), np.float32)
    for case, rb in enumerate((0, 1, rows // NA_Q_ROWS - 1)):
        for p in range(n_sub):
            r0 = rb * NA_Q_ROWS + p * NA_SUB_Q_ROWS
            base = int(np.clip(*_na_key_start_row(r0)))
            for ql in range(NA_SUB_Q_ROWS):
                r = r0 + ql
                row_start = int(np.clip(r - NA_ROWS // 2, 0, rows - NA_ROWS))
                assert base <= row_start and row_start + NA_ROWS <= base + NA_SUB_K_ROWS
                for kl in range(NA_SUB_K_ROWS):
                    kr = base + kl
                    if row_start <= kr < row_start + NA_ROWS:
                        sel_r[case, p, ql, kl, kr - r + NA_ROWS - 1] = 1.0
    c = np.arange(GRID_W)
    col_start = np.clip(c - NA_COLS // 2, 0, GRID_W - NA_COLS)
    col_ok = (c[None, :] >= col_start[:, None]) & (c[None, :] < col_start[:, None] + NA_COLS)
    dc = np.clip(c[None, :] - c[:, None] + NA_COLS - 1, 0, 2 * NA_COLS - 2)
    sel_c = (np.arange(2 * NA_COLS - 1)[:, None, None] == dc[None]) & col_ok[None]
    valid = (sel_r.sum(-1) > 0)[:, :, :, None, :, None] & col_ok[None, None, None, :, None, :]
    t1 = jnp.einsum("hab,cpqka->hcpqkb", rpb, sel_r, precision=lax.Precision.HIGHEST)
    t2 = jnp.einsum("hcpqkb,bxy->hcpqxky", t1, sel_c.astype(np.float32), precision=lax.Precision.HIGHEST)
    bias = jnp.where(valid[None], t2 * math.log2(math.e), -jnp.inf)
    return bias.reshape(rpb.shape[0], 3, n_sub, NA_SUB_Q_ROWS * GRID_W, NA_SUB_K_ROWS * GRID_W)


def _na_kernel(q_ref, k_ref, v_ref, b_ref, o_ref):
    rb = pl.program_id(2)
    nq = NA_SUB_Q_ROWS * GRID_W
    nk = NA_SUB_K_ROWS * GRID_W
    n_sub = NA_Q_ROWS // NA_SUB_Q_ROWS

    def key_start(p):
        row, lo, hi = _na_key_start_row(rb * NA_Q_ROWS + p * NA_SUB_Q_ROWS)
        return pl.multiple_of(jnp.clip(row, lo, hi) * GRID_W, NA_SUB_Q_ROWS * GRID_W)

    def scores(item):
        hl, p = item
        lanes = slice(hl * HEAD_DIM, (hl + 1) * HEAD_DIM)
        q = (q_ref[0, p * nq:(p + 1) * nq, lanes].astype(F32) * (HEAD_DIM ** -0.5 * math.log2(math.e))).astype(BF16)
        return _dot_nt(q, k_ref[0, pl.ds(key_start(p), nk), lanes]) + b_ref[hl, 0, p]

    def weights(s):
        e = jnp.exp2(s - jnp.max(s, axis=-1, keepdims=True))
        return e.astype(BF16), 1.0 / jnp.sum(e, axis=-1, keepdims=True)

    def finish(item, e_inv):
        hl, p = item
        lanes = slice(hl * HEAD_DIM, (hl + 1) * HEAD_DIM)
        e, inv_l = e_inv
        o = _dot(e, v_ref[0, pl.ds(key_start(p), nk), lanes]) * inv_l
        o_ref[0, p * nq:(p + 1) * nq, lanes] = o.astype(o_ref.dtype)

    items = [(hl, p) for hl in range(NA_HEADS_PER_STEP) for p in range(n_sub)]
    s_next = scores(items[0])
    e_prev = None
    for n, item in enumerate(items):
        s_cur = s_next
        if n + 1 < len(items):
            s_next = scores(items[n + 1])
        e_cur = weights(s_cur)
        if e_prev is not None:
            finish(items[n - 1], e_prev)
        e_prev = e_cur
    finish(items[-1], e_prev)


def neighbourhood_attention(proj, col0, bias):
    b, s, _ = proj.shape
    nq = NA_Q_ROWS * GRID_W
    nblk = s // nq
    hs = NA_HEADS_PER_STEP
    w = hs * HEAD_DIM
    c0 = col0 // w
    groups = HEADS // hs

    def bias_map(bi, h, rb):
        return (h, jnp.where(rb == 0, 0, jnp.where(rb == nblk - 1, 2, 1)), 0, 0, 0)

    return pl.pallas_call(
        _na_kernel,
        grid=(b, groups, nblk),
        in_specs=[
            pl.BlockSpec((1, nq, w), lambda bi, h, rb: (bi, rb, c0 + h)),
            pl.BlockSpec((1, s, w), lambda bi, h, rb: (bi, 0, c0 + groups + h)),
            pl.BlockSpec((1, s, w), lambda bi, h, rb: (bi, 0, c0 + 2 * groups + h)),
            pl.BlockSpec((hs, 1) + bias.shape[2:], bias_map),
        ],
        out_specs=pl.BlockSpec((1, nq, w), lambda bi, h, rb: (bi, rb, h)),
        out_shape=jax.ShapeDtypeStruct((b, s, B_W), BF16),
        compiler_params=_params("parallel", "parallel", "arbitrary"),
        name="neighbourhood_attention",
    )(proj, proj, proj, bias)


HALO = SUBLANES_BF16


def _split3(x):
    hi = x.astype(BF16)
    r1 = x - hi.astype(F32)
    mid = r1.astype(BF16)
    lo = (r1 - mid.astype(F32)).astype(BF16)
    return hi, mid, lo


def _dn_prep_kernel(x_ref, xp_ref, xx_ref, cw_ref, gt_ref, alog_ref, dtb_ref,
                    q_ref, k_ref, v_ref, gc_ref, beta_ref, xs_ref):
    i = pl.program_id(1)
    tb = x_ref.shape[1]
    pad = A_CONV_W // 2
    xs_ref[0:HALO, :] = jnp.where(i == 0, 0.0, xp_ref[0].astype(F32))
    xs_ref[HALO:HALO + tb, :] = x_ref[0].astype(F32)
    xs_ref[HALO + tb:, :] = jnp.where(i == pl.num_programs(1) - 1, 0.0, xx_ref[0].astype(F32))
    cw = cw_ref[...]
    y = xs_ref[pl.ds(HALO - pad, tb), :] * cw[0:1]
    for t in range(1, A_CONV_W):
        y = y + xs_ref[pl.ds(HALO - pad + t, tb), :] * cw[t:t + 1]
    y = y * _sigmoid(y)
    for h in range(HEADS):
        qh = y[:, h * HEAD_DIM:(h + 1) * HEAD_DIM]
        kh = y[:, (HEADS + h) * HEAD_DIM:(HEADS + h + 1) * HEAD_DIM]
        qn = qh * (lax.rsqrt(jnp.sum(qh * qh, axis=-1, keepdims=True) + 1e-6) * (HEAD_DIM ** -0.5))
        kn = kh * lax.rsqrt(jnp.sum(kh * kh, axis=-1, keepdims=True) + 1e-6)
        q_ref[0, h] = qn.astype(q_ref.dtype)
        k_ref[0, h] = kn.astype(k_ref.dtype)
        v_ref[0, h] = y[:, (2 * HEADS + h) * HEAD_DIM:(2 * HEADS + h + 1) * HEAD_DIM].astype(v_ref.dtype)

    nd = 2 * HEADS
    gates = gt_ref[0]
    z = gates[:, 0:nd] + dtb_ref[...]
    softplus = jnp.maximum(z, 0.0) + jnp.log(1.0 + jnp.exp(-jnp.abs(z)))
    g = -jnp.exp(alog_ref[...]) * softplus
    beta_ref[0] = _sigmoid(gates[:, nd:2 * nd])
    r = lax.broadcasted_iota(jnp.int32, (tb, tb), 0)
    c = lax.broadcasted_iota(jnp.int32, (tb, tb), 1)
    lower = jnp.where(r >= c, 1.0, 0.0).astype(BF16)
    upper = jnp.where(r <= c, 1.0, 0.0).astype(BF16)
    parts = _split3(g)
    fwd = _dot(lower, parts[0]) + _dot(lower, parts[1]) + _dot(lower, parts[2])
    bwd = _dot(upper, parts[0]) + _dot(upper, parts[1]) + _dot(upper, parts[2])
    col = lax.broadcasted_iota(jnp.int32, (tb, nd), 1)
    gc_ref[0] = jnp.where(col < HEADS, fwd, bwd)


def deltanet_prep(proj, gates, conv_w, a_log, dt_bias, tb=DN_BLOCK):
    b, s, _ = proj.shape
    nb = s // tb
    rh = tb // HALO
    nd = 2 * HEADS
    hs = jax.ShapeDtypeStruct((b, HEADS, s, HEAD_DIM), BF16)
    head_spec = pl.BlockSpec((1, HEADS, tb, HEAD_DIM), lambda bi, i: (bi, 0, i, 0))
    vec_spec = pl.BlockSpec((1, tb, nd), lambda bi, i: (bi, i, 0))
    return pl.pallas_call(
        _dn_prep_kernel,
        grid=(b, nb),
        in_specs=[
            pl.BlockSpec((1, tb, A_QKV_W), lambda bi, i: (bi, i, 0)),
            pl.BlockSpec((1, HALO, A_QKV_W), lambda bi, i: (bi, jnp.maximum(i * rh - 1, 0), 0)),
            pl.BlockSpec((1, HALO, A_QKV_W), lambda bi, i: (bi, jnp.minimum((i + 1) * rh, s // HALO - 1), 0)),
            pl.BlockSpec((A_CONV_W, A_QKV_W), lambda bi, i: (0, 0)),
            pl.BlockSpec((1, tb, LANES), lambda bi, i: (bi, i, 0)),
            pl.BlockSpec((1, nd), lambda bi, i: (0, 0)),
            pl.BlockSpec((1, nd), lambda bi, i: (0, 0)),
        ],
        out_specs=[head_spec, head_spec, head_spec, vec_spec, vec_spec],
        out_shape=[hs, hs, hs, jax.ShapeDtypeStruct((b, s, nd), F32), jax.ShapeDtypeStruct((b, s, nd), F32)],
        scratch_shapes=[pltpu.VMEM((tb + 2 * HALO, A_QKV_W), F32)],
        compiler_params=_params("parallel", "arbitrary"),
        name="deltanet_prep",
    )(proj, proj, proj, conv_w, gates, a_log.reshape(1, nd), dt_bias.reshape(1, nd))


def _split2(x):
    hi = x.astype(BF16)
    return hi, (x - hi.astype(F32)).astype(BF16)


def _unit_triangular_solve(ms, rhss, same_block):
    n = ms[0].shape[0]
    grp = DN_GROUP
    ng = n // grp
    r = lax.broadcasted_iota(jnp.int32, (grp, grp), 0)
    c = lax.broadcasted_iota(jnp.int32, (grp, grp), 1)
    base_mask = (r // DN_BASE) == (c // DN_BASE)
    row = lax.broadcasted_iota(jnp.int32, (DN_BASE, grp), 0)
    lane = lax.broadcasted_iota(jnp.int32, (DN_BASE, grp), 1)
    left = lane < DN_BASE

    def pair_lhs(hi, lo):
        hi_f, lo_f = hi.astype(F32), lo.astype(F32)
        swapped = pltpu.roll(hi_f, DN_BASE, 1)
        top = jnp.concatenate([jnp.where(left, hi_f, swapped), jnp.where(left, lo_f, 0.0)], axis=1)
        bottom = jnp.concatenate([jnp.where(left, swapped, hi_f), jnp.where(left, 0.0, lo_f)], axis=1)
        return jnp.concatenate([top, bottom], axis=0).astype(BF16)

    def pair_product(lhs, rhs_hi, rhs_lo):
        res = _dot(lhs, jnp.concatenate([rhs_hi, rhs_lo, rhs_hi, rhs_hi], axis=0))
        return jnp.where(left, res[:DN_BASE], res[DN_BASE:])

    pieces = [jnp.where(base_mask, m[g * grp:(g + 1) * grp, g * grp:(g + 1) * grp], 0.0)
              for m in ms for g in range(ng)]
    pws = [piece[:DN_BASE] + piece[DN_BASE:] for piece in pieces]
    ts = [jnp.where(lane % DN_BASE == row, 1.0, 0.0) + pw for pw in pws]
    splits = [_split2(pw) for pw in pws]
    lhss = [pair_lhs(hi, lo) for hi, lo in splits]
    size = 2
    while size < DN_BASE:
        pws = [pair_product(lhs, hi, lo) for lhs, (hi, lo) in zip(lhss, splits)]
        splits = [_split2(pw) for pw in pws]
        lhss = [pair_lhs(hi, lo) for hi, lo in splits]
        ts = [t + pair_product(lhs, *_split2(t)) for lhs, t in zip(lhss, ts)]
        size *= 2
    t_bases = [jnp.concatenate([jnp.where(left, t, 0.0), jnp.where(left, 0.0, t)], axis=0).astype(BF16)
               for t in ts]
    offs = {}
    blk = 2 * DN_BASE
    while blk <= n:
        offs[blk] = [jnp.where(same_block(blk), jnp.where(same_block(blk // 2), 0.0, m), 0.0).astype(BF16)
                     for m in ms]
        blk *= 2

    def apply(blk, ys):
        if blk == DN_BASE:
            ybs = [y.astype(BF16) for y in ys]
            return [jnp.concatenate([_dot(t_bases[i * ng + g], yb[g * grp:(g + 1) * grp]) for g in range(ng)],
                                    axis=0) for i, yb in enumerate(ybs)]
        zs = apply(blk // 2, ys)
        corr = apply(blk // 2, [_dot(off, z.astype(BF16)) for off, z in zip(offs[blk], zs)])
        return [z + cr for z, cr in zip(zs, corr)]

    return apply(n, rhss)


def _dn_block(chains):
    n = chains[0][0].shape[0]
    r = lax.broadcasted_iota(jnp.int32, (n, n), 0)
    c = lax.broadcasted_iota(jnp.int32, (n, n), 1)

    def same_block(size):
        return (r // size) == (c // size)

    ms, rhss, pre = [], [], []
    for q, k, v, gcol, grow, beta, state_ref, forward in chains:
        d = (r - c) if forward else (c - r)
        kf = k.astype(F32)
        kb = kf * beta
        decay = jnp.exp(jnp.where(d >= 0, gcol - grow, NEG_BIG))
        ms.append(jnp.where(d > 0, -(_dot_nt(kb.astype(BF16), k) * decay), 0.0))
        intra = (_dot_nt(q, k) * decay).astype(BF16)
        eg = jnp.exp(gcol)
        rhss.append(jnp.concatenate([v.astype(F32) * beta, kb * eg], axis=1))
        g_last = grow[:, n - 1:n] if forward else grow[:, 0:1]
        pre.append((intra, (q.astype(F32) * eg).astype(BF16), (kf * jnp.exp(g_last - gcol)).astype(BF16),
                    jnp.exp(g_last)))
    sols = _unit_triangular_solve(ms, rhss, same_block)
    outs = []
    for chain, sol, (intra, q_dec, k_dec, blk_decay) in zip(chains, sols, pre):
        state_ref = chain[6]
        state = state_ref[...]
        sb = state.astype(BF16)
        v_new = sol[:, :HEAD_DIM] - _dot(sol[:, HEAD_DIM:].astype(BF16), sb)
        vb = v_new.astype(BF16)
        outs.append(_dot(q_dec, sb) + _dot(intra, vb))
        state_ref[...] = state * blk_decay + _dot_tn(k_dec, vb)
    return outs


def _dn_kernel(qf_ref, kf_ref, vf_ref, qb_ref, kb_ref, vb_ref, gcf_ref, gcb_ref, btf_ref, btb_ref,
               grf_ref, grb_ref, of_ref, ob_ref, state_ref):
    hb = qf_ref.shape[1]

    @pl.when(pl.program_id(2) == 0)
    def _():
        state_ref[...] = jnp.zeros_like(state_ref)

    lane = lax.broadcasted_iota(jnp.int32, gcf_ref.shape[1:], 1)

    def pick(ref, idx):
        return jnp.sum(jnp.where(lane == idx, ref[0], 0.0), axis=-1, keepdims=True)

    chains = []
    for hl in range(hb):
        h = pl.program_id(1) * hb + hl
        chains.append((qf_ref[0, hl], kf_ref[0, hl], vf_ref[0, hl], pick(gcf_ref, h),
                       grf_ref[0, pl.ds(h, 1), :], pick(btf_ref, h), state_ref.at[0, hl], True))
        chains.append((qb_ref[0, hl], kb_ref[0, hl], vb_ref[0, hl], pick(gcb_ref, HEADS + h),
                       grb_ref[0, pl.ds(HEADS + h, 1), :], pick(btb_ref, HEADS + h), state_ref.at[1, hl], False))
    outs = _dn_block(chains)
    for hl in range(hb):
        of_ref[0, hl] = outs[2 * hl]
        ob_ref[0, hl] = outs[2 * hl + 1]


def deltanet_scan(q, k, v, gc, beta, gc_rows, tb=DN_BLOCK, hb=DN_HEADS_PER_STEP):
    b, _, s, _ = q.shape
    nb = s // tb
    nd = 2 * HEADS
    fwd = pl.BlockSpec((1, hb, tb, HEAD_DIM), lambda bi, h, c: (bi, h, c, 0))
    bwd = pl.BlockSpec((1, hb, tb, HEAD_DIM), lambda bi, h, c: (bi, h, nb - 1 - c, 0))
    vec_f = pl.BlockSpec((1, tb, nd), lambda bi, h, c: (bi, c, 0))
    vec_b = pl.BlockSpec((1, tb, nd), lambda bi, h, c: (bi, nb - 1 - c, 0))
    row_f = pl.BlockSpec((1, nd, tb), lambda bi, h, c: (bi, 0, c))
    row_b = pl.BlockSpec((1, nd, tb), lambda bi, h, c: (bi, 0, nb - 1 - c))
    os_ = jax.ShapeDtypeStruct((b, HEADS, s, HEAD_DIM), F32)
    return pl.pallas_call(
        _dn_kernel,
        grid=(b, HEADS // hb, nb),
        in_specs=[fwd, fwd, fwd, bwd, bwd, bwd, vec_f, vec_b, vec_f, vec_b, row_f, row_b],
        out_specs=[fwd, bwd],
        out_shape=[os_, os_],
        scratch_shapes=[pltpu.VMEM((2, hb, HEAD_DIM, HEAD_DIM), F32)],
        compiler_params=_params("parallel", "parallel", "arbitrary"),
        name="deltanet_scan",
    )(q, k, v, q, k, v, gc, gc, beta, beta, gc_rows, gc_rows)


def _dn_out_kernel(of_ref, ob_ref, z_ref, nw_ref, o_ref):
    nw = nw_ref[...]
    for h in range(HEADS):
        o = _rms(of_ref[0, h] + ob_ref[0, h], nw)
        z = z_ref[0, :, h * HEAD_DIM:(h + 1) * HEAD_DIM].astype(F32)
        o_ref[0, :, h * HEAD_DIM:(h + 1) * HEAD_DIM] = (o * (z * _sigmoid(z))).astype(o_ref.dtype)


def deltanet_out(o_f, o_b, proj, z_col0, out_norm, ts=ROW_TILE):
    b, _, s, _ = o_f.shape
    head_spec = pl.BlockSpec((1, HEADS, ts, HEAD_DIM), lambda bi, i: (bi, 0, i, 0))
    zb = z_col0 // A_V_W
    return pl.pallas_call(
        _dn_out_kernel,
        grid=(b, s // ts),
        in_specs=[head_spec, head_spec,
                  pl.BlockSpec((1, ts, A_V_W), lambda bi, i: (bi, i, zb)),
                  pl.BlockSpec((1, HEAD_DIM), lambda bi, i: (0, 0))],
        out_specs=pl.BlockSpec((1, ts, A_V_W), lambda bi, i: (bi, i, 0)),
        out_shape=jax.ShapeDtypeStruct((b, s, A_V_W), BF16),
        compiler_params=_params("parallel", "parallel"),
        name="deltanet_out",
    )(o_f, o_b, proj, out_norm.reshape(1, HEAD_DIM))


def _rope_tables(s):
    half = ROPE_DIMS // 2
    inv = ROPE_THETA ** (-jnp.arange(0, ROPE_DIMS, 2, dtype=F32) / ROPE_DIMS)
    ang = jnp.arange(s, dtype=F32)[:, None] * inv[None, :]
    cos, sin = jnp.cos(ang), jnp.sin(ang)
    rest = HEAD_DIM - ROPE_DIMS
    c = jnp.concatenate([cos, cos, jnp.ones((s, rest), F32)], axis=1)
    s1 = jnp.concatenate([jnp.zeros((s, half), F32), sin, jnp.zeros((s, rest), F32)], axis=1)
    s2 = jnp.concatenate([-sin, jnp.zeros((s, half + rest), F32)], axis=1)
    return c, s1, s2


def _trunk(x_a, x_b, p_a, p_b, ab_w_in, ab_conv_w, ab_a_log, ab_dt_bias, ab_out_norm, ab_rpb, ab_w_out,
           c_w_in, c_lambda, c_subln, c_w_out, norms, ffn_w_in, ffn_conv_w, ffn_conv_b,
           ffn_w_out, ple_w_proj, ple_w_gate):
    b_a, s, d = x_a.shape
    b = b_a + x_b.shape[0]
    t, t_a = b * s, b_a * s
    depth = norms.shape[0]
    depth_p = p_a.shape[0]
    h = jnp.concatenate([x_a, x_b], axis=0).reshape(t, d)
    rope = _rope_tables(s)
    o1 = A_QKV_W
    o2 = o1 + A_V_W
    o3 = o2 + 4 * HEADS
    ab_main = _column_tiles(jnp.concatenate([ab_w_in[:, :, :o2], ab_w_in[:, :, o3:]], axis=2), COL_TILE)
    ab_gate = _column_tiles(jnp.pad(ab_w_in[:, :, o2:o3], ((0, 0), (0, 0), (0, LANES - 4 * HEADS))), LANES)
    c_in = _column_tiles(c_w_in, COL_TILE)
    ffn_in = _column_tiles(ffn_w_in, FF_TILE)
    ab_out, c_out, ffn_out = ab_w_out.astype(BF16), c_w_out.astype(BF16), ffn_w_out.astype(BF16)
    ple_gate, ple_proj = ple_w_gate.astype(BF16), ple_w_proj.astype(BF16)
    for layer in range(depth):
        j = layer // 2
        if layer % 2 == 0:
            proj = norm_matmul(h, norms[layer, 0], ab_main, j, BF16).reshape(b, s, -1)
            gates = norm_matmul(h, norms[layer, 0], ab_gate, j, F32).reshape(b, s, LANES)
            qa, ka, va, gc, beta = deltanet_prep(proj, gates, ab_conv_w[j], ab_a_log[j], ab_dt_bias[j])
            o_f, o_b = deltanet_scan(qa, ka, va, gc, beta, jnp.transpose(gc, (0, 2, 1)))
            o_a = deltanet_out(o_f, o_b, proj, o1, ab_out_norm[j])
            o_nb = neighbourhood_attention(proj, o2, _na_bias(ab_rpb[j]))
            h = proj_norm_res([o_a.reshape(t, -1), o_nb.reshape(t, -1)], ab_out, j, norms[layer, 1], h)
        else:
            proj = norm_matmul(h, norms[layer, 0], c_in, j, BF16, rope=rope,
                               rope_tiles=2 * C_QK_W // COL_TILE, seq_len=s).reshape(b, s, -1)
            o_c = diff_attention(proj, c_lambda[j], c_subln[j], layer)
            h = proj_norm_res([o_c.reshape(t, -1)], c_out, j, norms[layer, 1], h)
        h = conv_ffn(h, norms[layer, 2], ffn_in, ffn_out, layer, ffn_conv_w[layer], ffn_conv_b[layer],
                     norms[layer, 3], s)
        ple_args = (p_a.reshape(depth_p, t_a, -1), p_b.reshape(depth_p, t - t_a, -1), ple_gate, ple_proj, layer)
        if layer + 1 < depth:
            h = ple(h, *ple_args)
    y_a = ple(h, *ple_args, row0=0, rows=t_a)
    y_b = ple(h, *ple_args, row0=t_a, rows=t - t_a)
    return y_a.reshape(b_a, s, d), y_b.reshape(b - b_a, s, d)


def kernel(x_prompt, x_sample, p_prompt, p_sample, ab_w_in, ab_conv_w, ab_a_log, ab_dt_bias, ab_out_norm,
           ab_rpb, ab_w_out, c_w_in, c_lambda, c_subln, c_w_out, norms, ffn_w_in, ffn_conv_w, ffn_conv_b,
           ffn_w_out, ple_w_proj, ple_w_gate):
    return _trunk(x_prompt, x_sample, p_prompt, p_sample, ab_w_in, ab_conv_w, ab_a_log, ab_dt_bias,
                  ab_out_norm, ab_rpb, ab_w_out, c_w_in, c_lambda, c_subln, c_w_out, norms, ffn_w_in,
                  ffn_conv_w, ffn_conv_b, ffn_w_out, ple_w_proj, ple_w_gate)
```

```python
import functools
import math

import jax
import jax.numpy as jnp
import numpy as np
from jax import lax
from jax.experimental import pallas as pl
from jax.experimental.pallas import tpu as pltpu

F32 = jnp.float32
BF16 = jnp.bfloat16

D_MODEL = 2048
PLE_DIM = 256
GRID_W = 64
RMS_EPS = 1e-6
HEADS = 8
HEAD_DIM = 128
A_CONV_W = 5
NA_ROWS = 8
NA_COLS = 16
ROPE_THETA = 500000.0
ROPE_DIMS = HEAD_DIM // 4
D_FF = 8192
FFN_CONV_W = 3
A_QKV_W = 3 * HEADS * HEAD_DIM
A_V_W = HEADS * HEAD_DIM
B_W = HEADS * HEAD_DIM
C_QK_W = HEADS * 2 * HEAD_DIM

LANES = 128
SUBLANES_BF16 = 16
VMEM_LIMIT = 56 * 1024 * 1024
NEG_BIG = -1e30

ROW_TILE = 512
MM_ROW_TILE = 1024
COL_TILE = 1024
FF_TILE = 512
FFN_ROW_TILE = 1024
DN_BLOCK = 256
DN_BASE = 64
DN_GROUP = 128
DN_HEADS_PER_STEP = 4
ATT_Q_TILE = 512
ATT_SUB_ROWS = 256
NA_Q_ROWS = 8
NA_HEADS_PER_STEP = 4
NA_SUB_Q_ROWS = 4
NA_SUB_K_ROWS = 12


def _params(*sem):
    return pltpu.CompilerParams(dimension_semantics=sem, vmem_limit_bytes=VMEM_LIMIT)


def _rms(x, w):
    return x * lax.rsqrt(jnp.mean(x * x, axis=-1, keepdims=True) + RMS_EPS) * w


def _dot(a, b):
    return jnp.dot(a, b, preferred_element_type=F32)


def _dot_nt(a, b):
    return lax.dot_general(a, b, (((1,), (1,)), ((), ())), preferred_element_type=F32)


def _dot_tn(a, b):
    return lax.dot_general(a, b, (((0,), (0,)), ((), ())), preferred_element_type=F32)


def _sigmoid(x):
    return 1.0 / (1.0 + jnp.exp(-x))


def _norm_matmul_kernel(*refs, rope_tiles):
    if rope_tiles:
        x_ref, nw_ref, w_ref, rc_ref, rs1_ref, rs2_ref, o_ref, xn_ref = refs
    else:
        x_ref, nw_ref, w_ref, o_ref, xn_ref = refs
    j = pl.program_id(1)

    @pl.when(j == 0)
    def _():
        xn_ref[...] = _rms(x_ref[...], nw_ref[...]).astype(BF16)

    y = _dot(xn_ref[...], w_ref[0, 0])
    if not rope_tiles:
        o_ref[...] = y.astype(o_ref.dtype)
        return

    @pl.when(j < rope_tiles)
    def _():
        n = y.shape[1]
        reps = n // LANES
        c = jnp.concatenate([rc_ref[...]] * reps, axis=1)
        s1 = jnp.concatenate([rs1_ref[...]] * reps, axis=1)
        s2 = jnp.concatenate([rs2_ref[...]] * reps, axis=1)
        half = ROPE_DIMS // 2
        r = y * c + pltpu.roll(y, half, 1) * s1 + pltpu.roll(y, n - half, 1) * s2
        o_ref[...] = r.astype(o_ref.dtype)

    @pl.when(j >= rope_tiles)
    def _():
        o_ref[...] = y.astype(o_ref.dtype)


def _column_tiles(w, tn):
    nl, k, n = w.shape
    return jnp.transpose(w.reshape(nl, k, n // tn, tn), (0, 2, 1, 3)).astype(BF16)


def norm_matmul(x, nw, w_tiles, layer, out_dtype, rope=None, rope_tiles=0, seq_len=None, tm=MM_ROW_TILE):
    t, k = x.shape
    _, n_tiles, _, tn = w_tiles.shape
    n = n_tiles * tn
    tm = min(tm, t)
    assert t % tm == 0
    in_specs = [
        pl.BlockSpec((tm, k), lambda i, j: (i, 0)),
        pl.BlockSpec((1, k), lambda i, j: (0, 0)),
        pl.BlockSpec((1, 1, k, tn), lambda i, j: (layer, j, 0, 0)),
    ]
    args = [x, nw.reshape(1, k), w_tiles]
    if rope_tiles:
        per_seq = seq_len // tm
        for tab in rope:
            in_specs.append(pl.BlockSpec((tm, LANES), lambda i, j: (i % per_seq, 0)))
            args.append(tab)
    return pl.pallas_call(
        functools.partial(_norm_matmul_kernel, rope_tiles=rope_tiles),
        grid=(t // tm, n // tn),
        in_specs=in_specs,
        out_specs=pl.BlockSpec((tm, tn), lambda i, j: (i, j)),
        out_shape=jax.ShapeDtypeStruct((t, n), out_dtype),
        scratch_shapes=[pltpu.VMEM((tm, k), BF16)],
        compiler_params=_params("parallel", "arbitrary"),
        name="norm_matmul",
    )(*args)


def _proj_norm_res_kernel(*refs, n_in):
    a_refs = refs[:n_in]
    w_refs = refs[n_in:2 * n_in]
    nw_ref, nw_next_ref, h_ref, o_ref, xn_ref = refs[2 * n_in:]
    y = _dot(a_refs[0][...], w_refs[0][0])
    for a_ref, w_ref in zip(a_refs[1:], w_refs[1:]):
        y = y + _dot(a_ref[...], w_ref[0])
    h_new = h_ref[...] + _rms(y, nw_ref[...])
    o_ref[...] = h_new
    xn_ref[...] = _rms(h_new, nw_next_ref[...]).astype(BF16)


def proj_norm_res(acts, w, layer, nw, nw_next, h, tm=ROW_TILE):
    t, n = h.shape
    n_in = len(acts)
    kw = acts[0].shape[1]
    assert all(a.shape[1] == kw for a in acts) and w.shape[1] == n_in * kw
    in_specs = [pl.BlockSpec((tm, kw), lambda i: (i, 0)) for _ in acts]
    in_specs += [pl.BlockSpec((1, kw, n), functools.partial(lambda piece, i: (layer, piece, 0), piece))
                 for piece in range(n_in)]
    in_specs += [pl.BlockSpec((1, n), lambda i: (0, 0)), pl.BlockSpec((1, n), lambda i: (0, 0)),
                 pl.BlockSpec((tm, n), lambda i: (i, 0))]
    ws = [w] * n_in
    row_spec = pl.BlockSpec((tm, n), lambda i: (i, 0))
    return pl.pallas_call(
        functools.partial(_proj_norm_res_kernel, n_in=n_in),
        grid=(t // tm,),
        in_specs=in_specs,
        out_specs=[row_spec, row_spec],
        out_shape=[jax.ShapeDtypeStruct((t, n), F32), jax.ShapeDtypeStruct((t, n), BF16)],
        compiler_params=_params("parallel"),
        name="proj_norm_res",
    )(*acts, *ws, nw.reshape(1, n), nw_next.reshape(1, n), h)


def _ple_kernel(h_ref, f_ref, pa_ref, pb_ref, wg_ref, wp_ref, o_ref, *, first_block, a_blocks):
    h = h_ref[...] + f_ref[...]
    gate = _sigmoid(_dot(h.astype(BF16), wg_ref[0]))
    p = jnp.where(first_block + pl.program_id(0) < a_blocks, pa_ref[0], pb_ref[0])
    o_ref[...] = h + gate * _dot(p.astype(BF16), wp_ref[0])


def ple(h, f, p_a, p_b, wg, wp, layer, row0=0, rows=None, tm=ROW_TILE):
    n = h.shape[1]
    rows = h.shape[0] if rows is None else rows
    assert row0 % tm == 0 and rows % tm == 0 and p_a.shape[1] % tm == 0 and p_b.shape[1] % tm == 0
    i0 = row0 // tm
    na, nb = p_a.shape[1] // tm, p_b.shape[1] // tm
    return pl.pallas_call(
        functools.partial(_ple_kernel, first_block=i0, a_blocks=na),
        grid=(rows // tm,),
        in_specs=[
            pl.BlockSpec((tm, n), lambda i: (i0 + i, 0)),
            pl.BlockSpec((tm, n), lambda i: (i0 + i, 0)),
            pl.BlockSpec((1, tm, p_a.shape[2]), lambda i: (layer, jnp.minimum(i0 + i, na - 1), 0)),
            pl.BlockSpec((1, tm, p_b.shape[2]), lambda i: (layer, jnp.clip(i0 + i - na, 0, nb - 1), 0)),
            pl.BlockSpec((1,) + wg.shape[1:], lambda i: (layer, 0, 0)),
            pl.BlockSpec((1,) + wp.shape[1:], lambda i: (layer, 0, 0)),
        ],
        out_specs=pl.BlockSpec((tm, n), lambda i: (i, 0)),
        out_shape=jax.ShapeDtypeStruct((rows, n), F32),
        compiler_params=_params("parallel"),
        name="ple",
    )(h, f, p_a, p_b, wg, wp)


def _ffn_kernel(x_ref, xp_ref, xx_ref, wg_ref, wu_ref, cwg_ref, cwu_ref, cbg_ref, cbu_ref,
                wo_ref, nw2_ref, o_ref, xn_ref, hg_ref, hu_ref, *, tm, per_seq):
    i = pl.program_id(0)
    j = pl.program_id(1)

    @pl.when(j == 0)
    def _():
        pos = i % per_seq
        half = SUBLANES_BF16 // 2
        prev = jnp.where(pos == 0, 0.0, xp_ref[...].astype(F32))[half:]
        nxt = jnp.where(pos == per_seq - 1, 0.0, xx_ref[...].astype(F32))[:half]
        xn_ref[0:tm, :] = x_ref[...]
        xn_ref[tm:, :] = jnp.concatenate([nxt, prev], axis=0).astype(BF16)
        o_ref[...] = jnp.zeros_like(o_ref)

    xe = xn_ref[...]

    def conv(w_ref, h_ref, cw_ref, cb_ref):
        hid = _dot(xe, w_ref[0, 0])
        h_ref[8:, :] = hid
        h_ref[0:8, :] = hid[tm + 8:, :]
        cw = cw_ref[...]
        return (h_ref[pl.ds(7, tm), :] * cw[0:1] + h_ref[pl.ds(8, tm), :] * cw[1:2]
                + h_ref[pl.ds(9, tm), :] * cw[2:3] + cb_ref[...])

    g = conv(wg_ref, hg_ref, cwg_ref, cbg_ref)
    u = conv(wu_ref, hu_ref, cwu_ref, cbu_ref)
    c0 = math.sqrt(2.0 / math.pi)
    gelu = 0.5 * g * (1.0 + jnp.tanh(c0 * (g + 0.044715 * (g * g * g))))
    o_ref[...] += _dot((gelu * u).astype(BF16), wo_ref[0])

    @pl.when(j == pl.num_programs(1) - 1)
    def _():
        o_ref[...] = _rms(o_ref[...], nw2_ref[...])


def conv_ffn(xn, w_in, w_out, layer, conv_w, conv_b, nw_out, seq_len, tm=FFN_ROW_TILE):
    t, d = xn.shape
    f = w_out.shape[1]
    tf = w_in.shape[3]
    nf = f // tf
    per_seq = seq_len // tm
    rg = tm // SUBLANES_BF16
    last_group = t // SUBLANES_BF16 - 1
    cb = conv_b.reshape(1, 2 * f)
    return pl.pallas_call(
        functools.partial(_ffn_kernel, tm=tm, per_seq=per_seq),
        grid=(t // tm, nf),
        in_specs=[
            pl.BlockSpec((tm, d), lambda i, j: (i, 0)),
            pl.BlockSpec((SUBLANES_BF16, d), lambda i, j: (jnp.maximum(i * rg - 1, 0), 0)),
            pl.BlockSpec((SUBLANES_BF16, d), lambda i, j: (jnp.minimum((i + 1) * rg, last_group), 0)),
            pl.BlockSpec((1, 1, d, tf), lambda i, j: (layer, j, 0, 0)),
            pl.BlockSpec((1, 1, d, tf), lambda i, j: (layer, j + nf, 0, 0)),
            pl.BlockSpec((FFN_CONV_W, tf), lambda i, j: (0, j)),
            pl.BlockSpec((FFN_CONV_W, tf), lambda i, j: (0, j + nf)),
            pl.BlockSpec((1, tf), lambda i, j: (0, j)),
            pl.BlockSpec((1, tf), lambda i, j: (0, j + nf)),
            pl.BlockSpec((1, tf, d), lambda i, j: (layer, j, 0)),
            pl.BlockSpec((1, d), lambda i, j: (0, 0)),
        ],
        out_specs=pl.BlockSpec((tm, d), lambda i, j: (i, 0)),
        out_shape=jax.ShapeDtypeStruct((t, d), F32),
        scratch_shapes=[
            pltpu.VMEM((tm + SUBLANES_BF16, d), BF16),
            pltpu.VMEM((8 + tm + SUBLANES_BF16, tf), F32),
            pltpu.VMEM((8 + tm + SUBLANES_BF16, tf), F32),
        ],
        compiler_params=_params("parallel", "arbitrary"),
        name="conv_ffn",
    )(xn, xn, xn, w_in, w_in, conv_w, conv_w, cb, cb, w_out, nw_out.reshape(1, d))


def _diff_attn_kernel(lp_ref, sub_ref, q_ref, k_ref, v_ref, o_ref, *, lambda_init):
    lp = lp_ref[...]
    lam = (jnp.exp(jnp.sum(lp[0:1] * lp[1:2], axis=-1, keepdims=True))
           - jnp.exp(jnp.sum(lp[2:3] * lp[3:4], axis=-1, keepdims=True)) + lambda_init)
    sub = sub_ref[...]
    n_sub = q_ref.shape[1] // ATT_SUB_ROWS

    def scores(i):
        q = q_ref[0, i * ATT_SUB_ROWS:(i + 1) * ATT_SUB_ROWS, :].astype(F32)
        q = (q * (HEAD_DIM ** -0.5 * math.log2(math.e))).astype(BF16)
        return [_dot_nt(q[:, lo:lo + HEAD_DIM], k_ref[0, :, lo:lo + HEAD_DIM]) for lo in (0, HEAD_DIM)]

    def weights(s12):
        es = [jnp.exp2(s - jnp.max(s, axis=-1, keepdims=True)) for s in s12]
        l1, l2 = [jnp.sum(e, axis=-1, keepdims=True) for e in es]
        return (es[0] - es[1] * (lam * l1 / l2)).astype(BF16), 1.0 / l1

    def finish(i, a_inv):
        a, inv_l1 = a_inv
        o = _dot(a, v_ref[0]) * inv_l1
        o_ref[0, i * ATT_SUB_ROWS:(i + 1) * ATT_SUB_ROWS, :] = (_rms(o, sub) * (1.0 - lambda_init)).astype(o_ref.dtype)

    s_next = scores(0)
    a_prev = None
    for i in range(n_sub):
        s_cur = s_next
        if i + 1 < n_sub:
            s_next = scores(i + 1)
        a_cur = weights(s_cur)
        if a_prev is not None:
            finish(i - 1, a_prev)
        a_prev = a_cur
    finish(n_sub - 1, a_prev)


def diff_attention(proj, lam_params, subln, layer, tq=ATT_Q_TILE):
    b, s, _ = proj.shape
    w = 2 * HEAD_DIM
    lambda_init = 0.8 - 0.6 * math.exp(-0.3 * layer)
    return pl.pallas_call(
        functools.partial(_diff_attn_kernel, lambda_init=lambda_init),
        grid=(b, HEADS, s // tq),
        in_specs=[
            pl.BlockSpec((4, HEAD_DIM), lambda bi, h, qi: (0, 0)),
            pl.BlockSpec((1, w), lambda bi, h, qi: (0, 0)),
            pl.BlockSpec((1, tq, w), lambda bi, h, qi: (bi, qi, h)),
            pl.BlockSpec((1, s, w), lambda bi, h, qi: (bi, 0, HEADS + h)),
            pl.BlockSpec((1, s, w), lambda bi, h, qi: (bi, 0, 2 * HEADS + h)),
        ],
        out_specs=pl.BlockSpec((1, tq, w), lambda bi, h, qi: (bi, qi, h)),
        out_shape=jax.ShapeDtypeStruct((b, s, C_QK_W), BF16),
        compiler_params=_params("parallel", "parallel", "arbitrary"),
        name="diff_attention",
    )(lam_params, subln.reshape(1, w), proj, proj, proj)


def _na_key_start_row(first_query_row):
    return (first_query_row - NA_ROWS // 2, 0, GRID_W - NA_SUB_K_ROWS)


def _na_bias(rpb):
    rows = GRID_W
    n_sub = NA_Q_ROWS // NA_SUB_Q_ROWS
    sel_r = np.zeros((3, n_sub, NA_SUB_Q_ROWS, NA_SUB_K_ROWS, 2 * NA_ROWS - 1), np.float32)
    for case, rb in enumerate((0, 1, rows // NA_Q_ROWS - 1)):
        for p in range(n_sub):
            r0 = rb * NA_Q_ROWS + p * NA_SUB_Q_ROWS
            base = int(np.clip(*_na_key_start_row(r0)))
            for ql in range(NA_SUB_Q_ROWS):
                r = r0 + ql
                row_start = int(np.clip(r - NA_ROWS // 2, 0, rows - NA_ROWS))
                assert base <= row_start and row_start + NA_ROWS <= base + NA_SUB_K_ROWS
                for kl in range(NA_SUB_K_ROWS):
                    kr = base + kl
                    if row_start <= kr < row_start + NA_ROWS:
                        sel_r[case, p, ql, kl, kr - r + NA_ROWS - 1] = 1.0
    c = np.arange(GRID_W)
    col_start = np.clip(c - NA_COLS // 2, 0, GRID_W - NA_COLS)
    col_ok = (c[None, :] >= col_start[:, None]) & (c[None, :] < col_start[:, None] + NA_COLS)
    dc = np.clip(c[None, :] - c[:, None] + NA_COLS - 1, 0, 2 * NA_COLS - 2)
    sel_c = (np.arange(2 * NA_COLS - 1)[:, None, None] == dc[None]) & col_ok[None]
    valid = (sel_r.sum(-1) > 0)[:, :, :, None, :, None] & col_ok[None, None, None, :, None, :]
    t1 = jnp.einsum("hab,cpqka->hcpqkb", rpb, sel_r, precision=lax.Precision.HIGHEST)
    t2 = jnp.einsum("hcpqkb,bxy->hcpqxky", t1, sel_c.astype(np.float32), precision=lax.Precision.HIGHEST)
    bias = jnp.where(valid[None], t2 * math.log2(math.e), -jnp.inf)
    return bias.reshape(rpb.shape[0], 3, n_sub, NA_SUB_Q_ROWS * GRID_W, NA_SUB_K_ROWS * GRID_W)


def _na_kernel(q_ref, k_ref, v_ref, b_ref, o_ref):
    rb = pl.program_id(2)
    nq = NA_SUB_Q_ROWS * GRID_W
    nk = NA_SUB_K_ROWS * GRID_W
    n_sub = NA_Q_ROWS // NA_SUB_Q_ROWS

    def key_start(p):
        row, lo, hi = _na_key_start_row(rb * NA_Q_ROWS + p * NA_SUB_Q_ROWS)
        return pl.multiple_of(jnp.clip(row, lo, hi) * GRID_W, NA_SUB_Q_ROWS * GRID_W)

    def scores(item):
        hl, p = item
        lanes = slice(hl * HEAD_DIM, (hl + 1) * HEAD_DIM)
        q = (q_ref[0, p * nq:(p + 1) * nq, lanes].astype(F32) * (HEAD_DIM ** -0.5 * math.log2(math.e))).astype(BF16)
        return _dot_nt(q, k_ref[0, pl.ds(key_start(p), nk), lanes]) + b_ref[hl, 0, p]

    def weights(s):
        e = jnp.exp2(s - jnp.max(s, axis=-1, keepdims=True))
        return e.astype(BF16), 1.0 / jnp.sum(e, axis=-1, keepdims=True)

    def finish(item, e_inv):
        hl, p = item
        lanes = slice(hl * HEAD_DIM, (hl + 1) * HEAD_DIM)
        e, inv_l = e_inv
        o = _dot(e, v_ref[0, pl.ds(key_start(p), nk), lanes]) * inv_l
        o_ref[0, p * nq:(p + 1) * nq, lanes] = o.astype(o_ref.dtype)

    items = [(hl, p) for hl in range(NA_HEADS_PER_STEP) for p in range(n_sub)]
    s_next = scores(items[0])
    e_prev = None
    for n, item in enumerate(items):
        s_cur = s_next
        if n + 1 < len(items):
            s_next = scores(items[n + 1])
        e_cur = weights(s_cur)
        if e_prev is not None:
            finish(items[n - 1], e_prev)
        e_prev = e_cur
    finish(items[-1], e_prev)


def neighbourhood_attention(proj, col0, bias):
    b, s, _ = proj.shape
    nq = NA_Q_ROWS * GRID_W
    nblk = s // nq
    hs = NA_HEADS_PER_STEP
    w = hs * HEAD_DIM
    c0 = col0 // w
    groups = HEADS // hs

    def bias_map(bi, h, rb):
        return (h, jnp.where(rb == 0, 0, jnp.where(rb == nblk - 1, 2, 1)), 0, 0, 0)

    return pl.pallas_call(
        _na_kernel,
        grid=(b, groups, nblk),
        in_specs=[
            pl.BlockSpec((1, nq, w), lambda bi, h, rb: (bi, rb, c0 + h)),
            pl.BlockSpec((1, s, w), lambda bi, h, rb: (bi, 0, c0 + groups + h)),
            pl.BlockSpec((1, s, w), lambda bi, h, rb: (bi, 0, c0 + 2 * groups + h)),
            pl.BlockSpec((hs, 1) + bias.shape[2:], bias_map),
        ],
        out_specs=pl.BlockSpec((1, nq, w), lambda bi, h, rb: (bi, rb, h)),
        out_shape=jax.ShapeDtypeStruct((b, s, B_W), BF16),
        compiler_params=_params("parallel", "parallel", "arbitrary"),
        name="neighbourhood_attention",
    )(proj, proj, proj, bias)


HALO = SUBLANES_BF16


def _split3(x):
    hi = x.astype(BF16)
    r1 = x - hi.astype(F32)
    mid = r1.astype(BF16)
    lo = (r1 - mid.astype(F32)).astype(BF16)
    return hi, mid, lo


def _dn_prep_kernel(x_ref, xp_ref, xx_ref, cw_ref, gt_ref, alog_ref, dtb_ref,
                    q_ref, k_ref, v_ref, gc_ref, beta_ref, xs_ref):
    i = pl.program_id(1)
    tb = x_ref.shape[1]
    pad = A_CONV_W // 2
    xs_ref[0:HALO, :] = jnp.where(i == 0, 0.0, xp_ref[0].astype(F32))
    xs_ref[HALO:HALO + tb, :] = x_ref[0].astype(F32)
    xs_ref[HALO + tb:, :] = jnp.where(i == pl.num_programs(1) - 1, 0.0, xx_ref[0].astype(F32))
    cw = cw_ref[...]
    y = xs_ref[pl.ds(HALO - pad, tb), :] * cw[0:1]
    for t in range(1, A_CONV_W):
        y = y + xs_ref[pl.ds(HALO - pad + t, tb), :] * cw[t:t + 1]
    y = y * _sigmoid(y)
    for h in range(HEADS):
        qh = y[:, h * HEAD_DIM:(h + 1) * HEAD_DIM]
        kh = y[:, (HEADS + h) * HEAD_DIM:(HEADS + h + 1) * HEAD_DIM]
        qn = qh * (lax.rsqrt(jnp.sum(qh * qh, axis=-1, keepdims=True) + 1e-6) * (HEAD_DIM ** -0.5))
        kn = kh * lax.rsqrt(jnp.sum(kh * kh, axis=-1, keepdims=True) + 1e-6)
        q_ref[0, h] = qn.astype(q_ref.dtype)
        k_ref[0, h] = kn.astype(k_ref.dtype)
        v_ref[0, h] = y[:, (2 * HEADS + h) * HEAD_DIM:(2 * HEADS + h + 1) * HEAD_DIM].astype(v_ref.dtype)

    nd = 2 * HEADS
    gates = gt_ref[0]
    z = gates[:, 0:nd] + dtb_ref[...]
    softplus = jnp.maximum(z, 0.0) + jnp.log(1.0 + jnp.exp(-jnp.abs(z)))
    g = -jnp.exp(alog_ref[...]) * softplus
    beta_ref[0] = _sigmoid(gates[:, nd:2 * nd])
    r = lax.broadcasted_iota(jnp.int32, (tb, tb), 0)
    c = lax.broadcasted_iota(jnp.int32, (tb, tb), 1)
    lower = jnp.where(r >= c, 1.0, 0.0).astype(BF16)
    upper = jnp.where(r <= c, 1.0, 0.0).astype(BF16)
    parts = _split3(g)
    fwd = _dot(lower, parts[0]) + _dot(lower, parts[1]) + _dot(lower, parts[2])
    bwd = _dot(upper, parts[0]) + _dot(upper, parts[1]) + _dot(upper, parts[2])
    col = lax.broadcasted_iota(jnp.int32, (tb, nd), 1)
    gc_ref[0] = jnp.where(col < HEADS, fwd, bwd)


def deltanet_prep(proj, gates, conv_w, a_log, dt_bias, tb=DN_BLOCK):
    b, s, _ = proj.shape
    nb = s // tb
    rh = tb // HALO
    nd = 2 * HEADS
    hs = jax.ShapeDtypeStruct((b, HEADS, s, HEAD_DIM), BF16)
    head_spec = pl.BlockSpec((1, HEADS, tb, HEAD_DIM), lambda bi, i: (bi, 0, i, 0))
    vec_spec = pl.BlockSpec((1, tb, nd), lambda bi, i: (bi, i, 0))
    return pl.pallas_call(
        _dn_prep_kernel,
        grid=(b, nb),
        in_specs=[
            pl.BlockSpec((1, tb, A_QKV_W), lambda bi, i: (bi, i, 0)),
            pl.BlockSpec((1, HALO, A_QKV_W), lambda bi, i: (bi, jnp.maximum(i * rh - 1, 0), 0)),
            pl.BlockSpec((1, HALO, A_QKV_W), lambda bi, i: (bi, jnp.minimum((i + 1) * rh, s // HALO - 1), 0)),
            pl.BlockSpec((A_CONV_W, A_QKV_W), lambda bi, i: (0, 0)),
            pl.BlockSpec((1, tb, LANES), lambda bi, i: (bi, i, 0)),
            pl.BlockSpec((1, nd), lambda bi, i: (0, 0)),
            pl.BlockSpec((1, nd), lambda bi, i: (0, 0)),
        ],
        out_specs=[head_spec, head_spec, head_spec, vec_spec, vec_spec],
        out_shape=[hs, hs, hs, jax.ShapeDtypeStruct((b, s, nd), F32), jax.ShapeDtypeStruct((b, s, nd), F32)],
        scratch_shapes=[pltpu.VMEM((tb + 2 * HALO, A_QKV_W), F32)],
        compiler_params=_params("parallel", "arbitrary"),
        name="deltanet_prep",
    )(proj, proj, proj, conv_w, gates, a_log.reshape(1, nd), dt_bias.reshape(1, nd))


def _split2(x):
    hi = x.astype(BF16)
    return hi, (x - hi.astype(F32)).astype(BF16)


def _unit_triangular_solve(ms, rhss, same_block):
    n = ms[0].shape[0]
    grp = DN_GROUP
    ng = n // grp
    r = lax.broadcasted_iota(jnp.int32, (grp, grp), 0)
    c = lax.broadcasted_iota(jnp.int32, (grp, grp), 1)
    base_mask = (r // DN_BASE) == (c // DN_BASE)
    row = lax.broadcasted_iota(jnp.int32, (DN_BASE, grp), 0)
    lane = lax.broadcasted_iota(jnp.int32, (DN_BASE, grp), 1)
    left = lane < DN_BASE

    def pair_lhs(hi, lo):
        hi_f, lo_f = hi.astype(F32), lo.astype(F32)
        swapped = pltpu.roll(hi_f, DN_BASE, 1)
        top = jnp.concatenate([jnp.where(left, hi_f, swapped), jnp.where(left, lo_f, 0.0)], axis=1)
        bottom = jnp.concatenate([jnp.where(left, swapped, hi_f), jnp.where(left, 0.0, lo_f)], axis=1)
        return jnp.concatenate([top, bottom], axis=0).astype(BF16)

    def pair_product(lhs, rhs_hi, rhs_lo):
        res = _dot(lhs, jnp.concatenate([rhs_hi, rhs_lo, rhs_hi, rhs_hi], axis=0))
        return jnp.where(left, res[:DN_BASE], res[DN_BASE:])

    pieces = [jnp.where(base_mask, m[g * grp:(g + 1) * grp, g * grp:(g + 1) * grp], 0.0)
              for m in ms for g in range(ng)]
    pws = [piece[:DN_BASE] + piece[DN_BASE:] for piece in pieces]
    ts = [jnp.where(lane % DN_BASE == row, 1.0, 0.0) + pw for pw in pws]
    splits = [_split2(pw) for pw in pws]
    lhss = [pair_lhs(hi, lo) for hi, lo in splits]
    size = 2
    while size < DN_BASE:
        pws = [pair_product(lhs, hi, lo) for lhs, (hi, lo) in zip(lhss, splits)]
        splits = [_split2(pw) for pw in pws]
        lhss = [pair_lhs(hi, lo) for hi, lo in splits]
        ts = [t + pair_product(lhs, *_split2(t)) for lhs, t in zip(lhss, ts)]
        size *= 2
    t_bases = [jnp.concatenate([jnp.where(left, t, 0.0), jnp.where(left, 0.0, t)], axis=0).astype(BF16)
               for t in ts]
    offs = {}
    blk = 2 * DN_BASE
    while blk <= n:
        offs[blk] = [jnp.where(same_block(blk), jnp.where(same_block(blk // 2), 0.0, m), 0.0).astype(BF16)
                     for m in ms]
        blk *= 2

    def apply(blk, ys):
        if blk == DN_BASE:
            ybs = [y.astype(BF16) for y in ys]
            return [jnp.concatenate([_dot(t_bases[i * ng + g], yb[g * grp:(g + 1) * grp]) for g in range(ng)],
                                    axis=0) for i, yb in enumerate(ybs)]
        zs = apply(blk // 2, ys)
        corr = apply(blk // 2, [_dot(off, z.astype(BF16)) for off, z in zip(offs[blk], zs)])
        return [z + cr for z, cr in zip(zs, corr)]

    return apply(n, rhss)


def _dn_block(chains):
    n = chains[0][0].shape[0]
    r = lax.broadcasted_iota(jnp.int32, (n, n), 0)
    c = lax.broadcasted_iota(jnp.int32, (n, n), 1)

    def same_block(size):
        return (r // size) == (c // size)

    ms, rhss, pre = [], [], []
    for q, k, v, gcol, grow, beta, state_ref, forward in chains:
        d = (r - c) if forward else (c - r)
        kf = k.astype(F32)
        kb = kf * beta
        decay = jnp.exp(jnp.where(d >= 0, gcol - grow, NEG_BIG))
        ms.append(jnp.where(d > 0, -(_dot_nt(kb.astype(BF16), k) * decay), 0.0))
        intra = (_dot_nt(q, k) * decay).astype(BF16)
        eg = jnp.exp(gcol)
        rhss.append(jnp.concatenate([v.astype(F32) * beta, kb * eg], axis=1))
        g_last = grow[:, n - 1:n] if forward else grow[:, 0:1]
        pre.append((intra, (q.astype(F32) * eg).astype(BF16), (kf * jnp.exp(g_last - gcol)).astype(BF16),
                    jnp.exp(g_last)))
    sols = _unit_triangular_solve(ms, rhss, same_block)
    outs = []
    for chain, sol, (intra, q_dec, k_dec, blk_decay) in zip(chains, sols, pre):
        state_ref = chain[6]
        state = state_ref[...]
        sb = state.astype(BF16)
        v_new = sol[:, :HEAD_DIM] - _dot(sol[:, HEAD_DIM:].astype(BF16), sb)
        vb = v_new.astype(BF16)
        outs.append(_dot(q_dec, sb) + _dot(intra, vb))
        state_ref[...] = state * blk_decay + _dot_tn(k_dec, vb)
    return outs


def _dn_kernel(qf_ref, kf_ref, vf_ref, qb_ref, kb_ref, vb_ref, gcf_ref, gcb_ref, btf_ref, btb_ref,
               grf_ref, grb_ref, of_ref, ob_ref, state_ref):
    hb = qf_ref.shape[1]

    @pl.when(pl.program_id(2) == 0)
    def _():
        state_ref[...] = jnp.zeros_like(state_ref)

    lane = lax.broadcasted_iota(jnp.int32, gcf_ref.shape[1:], 1)

    def pick(ref, idx):
        return jnp.sum(jnp.where(lane == idx, ref[0], 0.0), axis=-1, keepdims=True)

    chains = []
    for hl in range(hb):
        h = pl.program_id(1) * hb + hl
        chains.append((qf_ref[0, hl], kf_ref[0, hl], vf_ref[0, hl], pick(gcf_ref, h),
                       grf_ref[0, pl.ds(h, 1), :], pick(btf_ref, h), state_ref.at[0, hl], True))
        chains.append((qb_ref[0, hl], kb_ref[0, hl], vb_ref[0, hl], pick(gcb_ref, HEADS + h),
                       grb_ref[0, pl.ds(HEADS + h, 1), :], pick(btb_ref, HEADS + h), state_ref.at[1, hl], False))
    outs = _dn_block(chains)
    for hl in range(hb):
        of_ref[0, hl] = outs[2 * hl]
        ob_ref[0, hl] = outs[2 * hl + 1]


def deltanet_scan(q, k, v, gc, beta, gc_rows, tb=DN_BLOCK, hb=DN_HEADS_PER_STEP):
    b, _, s, _ = q.shape
    nb = s // tb
    nd = 2 * HEADS
    fwd = pl.BlockSpec((1, hb, tb, HEAD_DIM), lambda bi, h, c: (bi, h, c, 0))
    bwd = pl.BlockSpec((1, hb, tb, HEAD_DIM), lambda bi, h, c: (bi, h, nb - 1 - c, 0))
    vec_f = pl.BlockSpec((1, tb, nd), lambda bi, h, c: (bi, c, 0))
    vec_b = pl.BlockSpec((1, tb, nd), lambda bi, h, c: (bi, nb - 1 - c, 0))
    row_f = pl.BlockSpec((1, nd, tb), lambda bi, h, c: (bi, 0, c))
    row_b = pl.BlockSpec((1, nd, tb), lambda bi, h, c: (bi, 0, nb - 1 - c))
    os_ = jax.ShapeDtypeStruct((b, HEADS, s, HEAD_DIM), F32)
    return pl.pallas_call(
        _dn_kernel,
        grid=(b, HEADS // hb, nb),
        in_specs=[fwd, fwd, fwd, bwd, bwd, bwd, vec_f, vec_b, vec_f, vec_b, row_f, row_b],
        out_specs=[fwd, bwd],
        out_shape=[os_, os_],
        scratch_shapes=[pltpu.VMEM((2, hb, HEAD_DIM, HEAD_DIM), F32)],
        compiler_params=_params("parallel", "parallel", "arbitrary"),
        name="deltanet_scan",
    )(q, k, v, q, k, v, gc, gc, beta, beta, gc_rows, gc_rows)


def _dn_out_kernel(of_ref, ob_ref, z_ref, nw_ref, o_ref):
    nw = nw_ref[...]
    for h in range(HEADS):
        o = _rms(of_ref[0, h] + ob_ref[0, h], nw)
        z = z_ref[0, :, h * HEAD_DIM:(h + 1) * HEAD_DIM].astype(F32)
        o_ref[0, :, h * HEAD_DIM:(h + 1) * HEAD_DIM] = (o * (z * _sigmoid(z))).astype(o_ref.dtype)


def deltanet_out(o_f, o_b, proj, z_col0, out_norm, ts=ROW_TILE):
    b, _, s, _ = o_f.shape
    head_spec = pl.BlockSpec((1, HEADS, ts, HEAD_DIM), lambda bi, i: (bi, 0, i, 0))
    zb = z_col0 // A_V_W
    return pl.pallas_call(
        _dn_out_kernel,
        grid=(b, s // ts),
        in_specs=[head_spec, head_spec,
                  pl.BlockSpec((1, ts, A_V_W), lambda bi, i: (bi, i, zb)),
                  pl.BlockSpec((1, HEAD_DIM), lambda bi, i: (0, 0))],
        out_specs=pl.BlockSpec((1, ts, A_V_W), lambda bi, i: (bi, i, 0)),
        out_shape=jax.ShapeDtypeStruct((b, s, A_V_W), BF16),
        compiler_params=_params("parallel", "parallel"),
        name="deltanet_out",
    )(o_f, o_b, proj, out_norm.reshape(1, HEAD_DIM))


def _rope_tables(s):
    half = ROPE_DIMS // 2
    inv = ROPE_THETA ** (-jnp.arange(0, ROPE_DIMS, 2, dtype=F32) / ROPE_DIMS)
    ang = jnp.arange(s, dtype=F32)[:, None] * inv[None, :]
    cos, sin = jnp.cos(ang), jnp.sin(ang)
    rest = HEAD_DIM - ROPE_DIMS
    c = jnp.concatenate([cos, cos, jnp.ones((s, rest), F32)], axis=1)
    s1 = jnp.concatenate([jnp.zeros((s, half), F32), sin, jnp.zeros((s, rest), F32)], axis=1)
    s2 = jnp.concatenate([-sin, jnp.zeros((s, half + rest), F32)], axis=1)
    return c, s1, s2


def _trunk(x_a, x_b, p_a, p_b, ab_w_in, ab_conv_w, ab_a_log, ab_dt_bias, ab_out_norm, ab_rpb, ab_w_out,
           c_w_in, c_lambda, c_subln, c_w_out, norms, ffn_w_in, ffn_conv_w, ffn_conv_b,
           ffn_w_out, ple_w_proj, ple_w_gate):
    b_a, s, d = x_a.shape
    b = b_a + x_b.shape[0]
    t, t_a = b * s, b_a * s
    depth = norms.shape[0]
    depth_p = p_a.shape[0]
    h = jnp.concatenate([x_a, x_b], axis=0).reshape(t, d)
    rope = _rope_tables(s)
    o1 = A_QKV_W
    o2 = o1 + A_V_W
    o3 = o2 + 4 * HEADS
    ab_main = _column_tiles(jnp.concatenate([ab_w_in[:, :, :o2], ab_w_in[:, :, o3:]], axis=2), COL_TILE)
    ab_gate = _column_tiles(jnp.pad(ab_w_in[:, :, o2:o3], ((0, 0), (0, 0), (0, LANES - 4 * HEADS))), LANES)
    c_in = _column_tiles(c_w_in, COL_TILE)
    ffn_in = _column_tiles(ffn_w_in, FF_TILE)
    ab_out, c_out, ffn_out = ab_w_out.astype(BF16), c_w_out.astype(BF16), ffn_w_out.astype(BF16)
    ple_gate, ple_proj = ple_w_gate.astype(BF16), ple_w_proj.astype(BF16)
    for layer in range(depth):
        j = layer // 2
        if layer % 2 == 0:
            proj = norm_matmul(h, norms[layer, 0], ab_main, j, BF16).reshape(b, s, -1)
            gates = norm_matmul(h, norms[layer, 0], ab_gate, j, F32).reshape(b, s, LANES)
            qa, ka, va, gc, beta = deltanet_prep(proj, gates, ab_conv_w[j], ab_a_log[j], ab_dt_bias[j])
            o_f, o_b = deltanet_scan(qa, ka, va, gc, beta, jnp.transpose(gc, (0, 2, 1)))
            o_a = deltanet_out(o_f, o_b, proj, o1, ab_out_norm[j])
            o_nb = neighbourhood_attention(proj, o2, _na_bias(ab_rpb[j]))
            h, xn = proj_norm_res([o_a.reshape(t, -1), o_nb.reshape(t, -1)], ab_out, j, norms[layer, 1],
                                  norms[layer, 2], h)
        else:
            proj = norm_matmul(h, norms[layer, 0], c_in, j, BF16, rope=rope,
                               rope_tiles=2 * C_QK_W // COL_TILE, seq_len=s).reshape(b, s, -1)
            o_c = diff_attention(proj, c_lambda[j], c_subln[j], layer)
            h, xn = proj_norm_res([o_c.reshape(t, -1)], c_out, j, norms[layer, 1], norms[layer, 2], h)
        f = conv_ffn(xn, ffn_in, ffn_out, layer, ffn_conv_w[layer], ffn_conv_b[layer], norms[layer, 3], s)
        ple_args = (f, p_a.reshape(depth_p, t_a, -1), p_b.reshape(depth_p, t - t_a, -1), ple_gate, ple_proj, layer)
        if layer + 1 < depth:
            h = ple(h, *ple_args)
    y_a = ple(h, *ple_args, row0=0, rows=t_a)
    y_b = ple(h, *ple_args, row0=t_a, rows=t - t_a)
    return y_a.reshape(b_a, s, d), y_b.reshape(b - b_a, s, d)


def kernel(x_prompt, x_sample, p_prompt, p_sample, ab_w_in, ab_conv_w, ab_a_log, ab_dt_bias, ab_out_norm,
           ab_rpb, ab_w_out, c_w_in, c_lambda, c_subln, c_w_out, norms, ffn_w_in, ffn_conv_w, ffn_conv_b,
           ffn_w_out, ple_w_proj, ple_w_gate):
    return _trunk(x_prompt, x_sample, p_prompt, p_sample, ab_w_in, ab_conv_w, ab_a_log, ab_dt_bias,
                  ab_out_norm, ab_rpb, ab_w_out, c_w_in, c_lambda, c_subln, c_w_out, norms, ffn_w_in,
                  ffn_conv_w, ffn_conv_b, ffn_w_out, ple_w_proj, ple_w_gate)
```

```python
import functools
import math

import jax
import jax.numpy as jnp
import numpy as np
from jax import lax
from jax.experimental import pallas as pl
from jax.experimental.pallas import tpu as pltpu

F32 = jnp.float32
BF16 = jnp.bfloat16

D_MODEL = 2048
PLE_DIM = 256
GRID_W = 64
RMS_EPS = 1e-6
HEADS = 8
HEAD_DIM = 128
A_CONV_W = 5
NA_ROWS = 8
NA_COLS = 16
ROPE_THETA = 500000.0
ROPE_DIMS = HEAD_DIM // 4
D_FF = 8192
FFN_CONV_W = 3
A_QKV_W = 3 * HEADS * HEAD_DIM
A_V_W = HEADS * HEAD_DIM
B_W = HEADS * HEAD_DIM
C_QK_W = HEADS * 2 * HEAD_DIM

LANES = 128
SUBLANES_BF16 = 16
VMEM_LIMIT = 56 * 1024 * 1024
NEG_BIG = -1e30

ROW_TILE = 512
MM_ROW_TILE = 1024
COL_TILE = 1024
FF_TILE = 512
FFN_ROW_TILE = 1024
FFN_ROW_PIECE = 256
DN_BLOCK = 256
DN_BASE = 64
DN_GROUP = 128
DN_HEADS_PER_STEP = 4
ATT_Q_TILE = 512
ATT_SUB_ROWS = 256
NA_Q_ROWS = 8
NA_HEADS_PER_STEP = 4
NA_SUB_Q_ROWS = 4
NA_SUB_K_ROWS = 12


def _params(*sem):
    return pltpu.CompilerParams(dimension_semantics=sem, vmem_limit_bytes=VMEM_LIMIT)


def _rms(x, w):
    return x * lax.rsqrt(jnp.mean(x * x, axis=-1, keepdims=True) + RMS_EPS) * w


def _dot(a, b):
    return jnp.dot(a, b, preferred_element_type=F32)


def _dot_nt(a, b):
    return lax.dot_general(a, b, (((1,), (1,)), ((), ())), preferred_element_type=F32)


def _dot_tn(a, b):
    return lax.dot_general(a, b, (((0,), (0,)), ((), ())), preferred_element_type=F32)


def _sigmoid(x):
    return 1.0 / (1.0 + jnp.exp(-x))


def _norm_matmul_kernel(*refs, rope_tiles):
    if rope_tiles:
        x_ref, nw_ref, w_ref, rc_ref, rs1_ref, rs2_ref, o_ref, xn_ref = refs
    else:
        x_ref, nw_ref, w_ref, o_ref, xn_ref = refs
    j = pl.program_id(1)

    @pl.when(j == 0)
    def _():
        xn_ref[...] = _rms(x_ref[...], nw_ref[...]).astype(BF16)

    y = _dot(xn_ref[...], w_ref[0, 0])
    if not rope_tiles:
        o_ref[...] = y.astype(o_ref.dtype)
        return

    @pl.when(j < rope_tiles)
    def _():
        n = y.shape[1]
        reps = n // LANES
        c = jnp.concatenate([rc_ref[...]] * reps, axis=1)
        s1 = jnp.concatenate([rs1_ref[...]] * reps, axis=1)
        s2 = jnp.concatenate([rs2_ref[...]] * reps, axis=1)
        half = ROPE_DIMS // 2
        r = y * c + pltpu.roll(y, half, 1) * s1 + pltpu.roll(y, n - half, 1) * s2
        o_ref[...] = r.astype(o_ref.dtype)

    @pl.when(j >= rope_tiles)
    def _():
        o_ref[...] = y.astype(o_ref.dtype)


def _column_tiles(w, tn):
    nl, k, n = w.shape
    return jnp.transpose(w.reshape(nl, k, n // tn, tn), (0, 2, 1, 3)).astype(BF16)


def norm_matmul(x, nw, w_tiles, layer, out_dtype, rope=None, rope_tiles=0, seq_len=None, tm=MM_ROW_TILE):
    t, k = x.shape
    _, n_tiles, _, tn = w_tiles.shape
    n = n_tiles * tn
    tm = min(tm, t)
    assert t % tm == 0
    in_specs = [
        pl.BlockSpec((tm, k), lambda i, j: (i, 0)),
        pl.BlockSpec((1, k), lambda i, j: (0, 0)),
        pl.BlockSpec((1, 1, k, tn), lambda i, j: (layer, j, 0, 0)),
    ]
    args = [x, nw.reshape(1, k), w_tiles]
    if rope_tiles:
        per_seq = seq_len // tm
        for tab in rope:
            in_specs.append(pl.BlockSpec((tm, LANES), lambda i, j: (i % per_seq, 0)))
            args.append(tab)
    return pl.pallas_call(
        functools.partial(_norm_matmul_kernel, rope_tiles=rope_tiles),
        grid=(t // tm, n // tn),
        in_specs=in_specs,
        out_specs=pl.BlockSpec((tm, tn), lambda i, j: (i, j)),
        out_shape=jax.ShapeDtypeStruct((t, n), out_dtype),
        scratch_shapes=[pltpu.VMEM((tm, k), BF16)],
        compiler_params=_params("parallel", "arbitrary"),
        name="norm_matmul",
    )(*args)


def _proj_norm_res_kernel(*refs, n_in):
    a_refs = refs[:n_in]
    w_refs = refs[n_in:2 * n_in]
    nw_ref, nw_next_ref, h_ref, o_ref, xn_ref = refs[2 * n_in:]
    y = _dot(a_refs[0][...], w_refs[0][0])
    for a_ref, w_ref in zip(a_refs[1:], w_refs[1:]):
        y = y + _dot(a_ref[...], w_ref[0])
    h_new = h_ref[...] + _rms(y, nw_ref[...])
    o_ref[...] = h_new
    xn_ref[...] = _rms(h_new, nw_next_ref[...]).astype(BF16)


def proj_norm_res(acts, w, layer, nw, nw_next, h, tm=ROW_TILE):
    t, n = h.shape
    n_in = len(acts)
    kw = acts[0].shape[1]
    assert all(a.shape[1] == kw for a in acts) and w.shape[1] == n_in * kw
    in_specs = [pl.BlockSpec((tm, kw), lambda i: (i, 0)) for _ in acts]
    in_specs += [pl.BlockSpec((1, kw, n), functools.partial(lambda piece, i: (layer, piece, 0), piece))
                 for piece in range(n_in)]
    in_specs += [pl.BlockSpec((1, n), lambda i: (0, 0)), pl.BlockSpec((1, n), lambda i: (0, 0)),
                 pl.BlockSpec((tm, n), lambda i: (i, 0))]
    ws = [w] * n_in
    row_spec = pl.BlockSpec((tm, n), lambda i: (i, 0))
    return pl.pallas_call(
        functools.partial(_proj_norm_res_kernel, n_in=n_in),
        grid=(t // tm,),
        in_specs=in_specs,
        out_specs=[row_spec, row_spec],
        out_shape=[jax.ShapeDtypeStruct((t, n), F32), jax.ShapeDtypeStruct((t, n), BF16)],
        compiler_params=_params("parallel"),
        name="proj_norm_res",
    )(*acts, *ws, nw.reshape(1, n), nw_next.reshape(1, n), h)


def _ple_kernel(h_ref, f_ref, pa_ref, pb_ref, wg_ref, wp_ref, o_ref, *, first_block, a_blocks):
    h = h_ref[...] + f_ref[...]
    gate = _sigmoid(_dot(h.astype(BF16), wg_ref[0]))
    p = jnp.where(first_block + pl.program_id(0) < a_blocks, pa_ref[0], pb_ref[0])
    o_ref[...] = h + gate * _dot(p.astype(BF16), wp_ref[0])


def ple(h, f, p_a, p_b, wg, wp, layer, row0=0, rows=None, tm=ROW_TILE):
    n = h.shape[1]
    rows = h.shape[0] if rows is None else rows
    assert row0 % tm == 0 and rows % tm == 0 and p_a.shape[1] % tm == 0 and p_b.shape[1] % tm == 0
    i0 = row0 // tm
    na, nb = p_a.shape[1] // tm, p_b.shape[1] // tm
    return pl.pallas_call(
        functools.partial(_ple_kernel, first_block=i0, a_blocks=na),
        grid=(rows // tm,),
        in_specs=[
            pl.BlockSpec((tm, n), lambda i: (i0 + i, 0)),
            pl.BlockSpec((tm, n), lambda i: (i0 + i, 0)),
            pl.BlockSpec((1, tm, p_a.shape[2]), lambda i: (layer, jnp.minimum(i0 + i, na - 1), 0)),
            pl.BlockSpec((1, tm, p_b.shape[2]), lambda i: (layer, jnp.clip(i0 + i - na, 0, nb - 1), 0)),
            pl.BlockSpec((1,) + wg.shape[1:], lambda i: (layer, 0, 0)),
            pl.BlockSpec((1,) + wp.shape[1:], lambda i: (layer, 0, 0)),
        ],
        out_specs=pl.BlockSpec((tm, n), lambda i: (i, 0)),
        out_shape=jax.ShapeDtypeStruct((rows, n), F32),
        compiler_params=_params("parallel"),
        name="ple",
    )(h, f, p_a, p_b, wg, wp)


def _ffn_kernel(x_ref, xp_ref, xx_ref, wg_ref, wu_ref, cwg_ref, cwu_ref, cbg_ref, cbu_ref,
                wo_ref, nw2_ref, o_ref, xn_ref, hg_ref, hu_ref, *, tm, per_seq):
    i = pl.program_id(0)
    j = pl.program_id(1)

    @pl.when(j == 0)
    def _():
        pos = i % per_seq
        half = SUBLANES_BF16 // 2
        prev = jnp.where(pos == 0, 0.0, xp_ref[...].astype(F32))[half:]
        nxt = jnp.where(pos == per_seq - 1, 0.0, xx_ref[...].astype(F32))[:half]
        xn_ref[0:tm, :] = x_ref[...]
        xn_ref[tm:, :] = jnp.concatenate([nxt, prev], axis=0).astype(BF16)
        o_ref[...] = jnp.zeros_like(o_ref)

    xe = xn_ref[...]

    for w_ref, h_ref in ((wg_ref, hg_ref), (wu_ref, hu_ref)):
        hid = _dot(xe, w_ref[0, 0])
        h_ref[8:, :] = hid
        h_ref[0:8, :] = hid[tm + 8:, :]

    def conv(h_ref, cw_ref, cb_ref, r0, rows):
        cw = cw_ref[...]
        return (h_ref[pl.ds(7 + r0, rows), :] * cw[0:1] + h_ref[pl.ds(8 + r0, rows), :] * cw[1:2]
                + h_ref[pl.ds(9 + r0, rows), :] * cw[2:3] + cb_ref[...])

    rows = min(FFN_ROW_PIECE, tm)
    for r0 in range(0, tm, rows):
        g = conv(hg_ref, cwg_ref, cbg_ref, r0, rows)
        u = conv(hu_ref, cwu_ref, cbu_ref, r0, rows)
        c0 = math.sqrt(2.0 / math.pi)
        gelu = 0.5 * g * (1.0 + jnp.tanh(c0 * (g + 0.044715 * (g * g * g))))
        o_ref[r0:r0 + rows, :] += _dot((gelu * u).astype(BF16), wo_ref[0])

    @pl.when(j == pl.num_programs(1) - 1)
    def _():
        o_ref[...] = _rms(o_ref[...], nw2_ref[...])


def conv_ffn(xn, w_in, w_out, layer, conv_w, conv_b, nw_out, seq_len, tm=FFN_ROW_TILE):
    t, d = xn.shape
    f = w_out.shape[1]
    tf = w_in.shape[3]
    nf = f // tf
    per_seq = seq_len // tm
    rg = tm // SUBLANES_BF16
    last_group = t // SUBLANES_BF16 - 1
    cb = conv_b.reshape(1, 2 * f)
    return pl.pallas_call(
        functools.partial(_ffn_kernel, tm=tm, per_seq=per_seq),
        grid=(t // tm, nf),
        in_specs=[
            pl.BlockSpec((tm, d), lambda i, j: (i, 0)),
            pl.BlockSpec((SUBLANES_BF16, d), lambda i, j: (jnp.maximum(i * rg - 1, 0), 0)),
            pl.BlockSpec((SUBLANES_BF16, d), lambda i, j: (jnp.minimum((i + 1) * rg, last_group), 0)),
            pl.BlockSpec((1, 1, d, tf), lambda i, j: (layer, j, 0, 0)),
            pl.BlockSpec((1, 1, d, tf), lambda i, j: (layer, j + nf, 0, 0)),
            pl.BlockSpec((FFN_CONV_W, tf), lambda i, j: (0, j)),
            pl.BlockSpec((FFN_CONV_W, tf), lambda i, j: (0, j + nf)),
            pl.BlockSpec((1, tf), lambda i, j: (0, j)),
            pl.BlockSpec((1, tf), lambda i, j: (0, j + nf)),
            pl.BlockSpec((1, tf, d), lambda i, j: (layer, j, 0)),
            pl.BlockSpec((1, d), lambda i, j: (0, 0)),
        ],
        out_specs=pl.BlockSpec((tm, d), lambda i, j: (i, 0)),
        out_shape=jax.ShapeDtypeStruct((t, d), F32),
        scratch_shapes=[
            pltpu.VMEM((tm + SUBLANES_BF16, d), BF16),
            pltpu.VMEM((8 + tm + SUBLANES_BF16, tf), F32),
            pltpu.VMEM((8 + tm + SUBLANES_BF16, tf), F32),
        ],
        compiler_params=_params("parallel", "arbitrary"),
        name="conv_ffn",
    )(xn, xn, xn, w_in, w_in, conv_w, conv_w, cb, cb, w_out, nw_out.reshape(1, d))


def _diff_attn_kernel(lp_ref, sub_ref, q_ref, k_ref, v_ref, o_ref, *, lambda_init):
    lp = lp_ref[...]
    lam = (jnp.exp(jnp.sum(lp[0:1] * lp[1:2], axis=-1, keepdims=True))
           - jnp.exp(jnp.sum(lp[2:3] * lp[3:4], axis=-1, keepdims=True)) + lambda_init)
    sub = sub_ref[...]
    n_sub = q_ref.shape[1] // ATT_SUB_ROWS

    def scores(i):
        q = q_ref[0, i * ATT_SUB_ROWS:(i + 1) * ATT_SUB_ROWS, :].astype(F32)
        q = (q * (HEAD_DIM ** -0.5 * math.log2(math.e))).astype(BF16)
        return [_dot_nt(q[:, lo:lo + HEAD_DIM], k_ref[0, :, lo:lo + HEAD_DIM]) for lo in (0, HEAD_DIM)]

    def weights(s12):
        es = [jnp.exp2(s - jnp.max(s, axis=-1, keepdims=True)) for s in s12]
        l1, l2 = [jnp.sum(e, axis=-1, keepdims=True) for e in es]
        return (es[0] - es[1] * (lam * l1 / l2)).astype(BF16), 1.0 / l1

    def finish(i, a_inv):
        a, inv_l1 = a_inv
        o = _dot(a, v_ref[0]) * inv_l1
        o_ref[0, i * ATT_SUB_ROWS:(i + 1) * ATT_SUB_ROWS, :] = (_rms(o, sub) * (1.0 - lambda_init)).astype(o_ref.dtype)

    s_next = scores(0)
    a_prev = None
    for i in range(n_sub):
        s_cur = s_next
        if i + 1 < n_sub:
            s_next = scores(i + 1)
        a_cur = weights(s_cur)
        if a_prev is not None:
            finish(i - 1, a_prev)
        a_prev = a_cur
    finish(n_sub - 1, a_prev)


def diff_attention(proj, lam_params, subln, layer, tq=ATT_Q_TILE):
    b, s, _ = proj.shape
    w = 2 * HEAD_DIM
    lambda_init = 0.8 - 0.6 * math.exp(-0.3 * layer)
    return pl.pallas_call(
        functools.partial(_diff_attn_kernel, lambda_init=lambda_init),
        grid=(b, HEADS, s // tq),
        in_specs=[
            pl.BlockSpec((4, HEAD_DIM), lambda bi, h, qi: (0, 0)),
            pl.BlockSpec((1, w), lambda bi, h, qi: (0, 0)),
            pl.BlockSpec((1, tq, w), lambda bi, h, qi: (bi, qi, h)),
            pl.BlockSpec((1, s, w), lambda bi, h, qi: (bi, 0, HEADS + h)),
            pl.BlockSpec((1, s, w), lambda bi, h, qi: (bi, 0, 2 * HEADS + h)),
        ],
        out_specs=pl.BlockSpec((1, tq, w), lambda bi, h, qi: (bi, qi, h)),
        out_shape=jax.ShapeDtypeStruct((b, s, C_QK_W), BF16),
        compiler_params=_params("parallel", "parallel", "arbitrary"),
        name="diff_attention",
    )(lam_params, subln.reshape(1, w), proj, proj, proj)


def _na_key_start_row(first_query_row):
    return (first_query_row - NA_ROWS // 2, 0, GRID_W - NA_SUB_K_ROWS)


def _na_bias(rpb):
    rows = GRID_W
    n_sub = NA_Q_ROWS // NA_SUB_Q_ROWS
    n_dr = 2 * NA_ROWS - 1
    c = np.arange(GRID_W)
    col_start = np.clip(c - NA_COLS // 2, 0, GRID_W - NA_COLS)
    col_ok = (c[None, :] >= col_start[:, None]) & (c[None, :] < col_start[:, None] + NA_COLS)
    dc = np.clip(c[None, :] - c[:, None] + NA_COLS - 1, 0, 2 * NA_COLS - 2)
    sel_c = (np.arange(2 * NA_COLS - 1)[:, None, None] == dc[None]).astype(np.float32)
    tables = jnp.einsum("hab,bxy->haxy", rpb, sel_c, precision=lax.Precision.HIGHEST) * math.log2(math.e)
    tables = jnp.where(col_ok[None, None], tables, -jnp.inf)
    masked = jnp.full(tables.shape[:1] + tables.shape[2:], -jnp.inf, tables.dtype)
    blocks = []
    for rb in (0, 1, rows // NA_Q_ROWS - 1):
        for p in range(n_sub):
            r0 = rb * NA_Q_ROWS + p * NA_SUB_Q_ROWS
            base = int(np.clip(*_na_key_start_row(r0)))
            q_rows = []
            for ql in range(NA_SUB_Q_ROWS):
                r = r0 + ql
                row_start = int(np.clip(r - NA_ROWS // 2, 0, rows - NA_ROWS))
                assert base <= row_start and row_start + NA_ROWS <= base + NA_SUB_K_ROWS
                tiles = []
                for kl in range(NA_SUB_K_ROWS):
                    kr = base + kl
                    inside = row_start <= kr < row_start + NA_ROWS
                    assert not inside or 0 <= kr - r + NA_ROWS - 1 < n_dr
                    tiles.append(tables[:, kr - r + NA_ROWS - 1] if inside else masked)
                q_rows.append(jnp.concatenate(tiles, axis=-1))
            blocks.append(jnp.concatenate(q_rows, axis=-2))
    bias = jnp.stack(blocks, axis=1)
    return bias.reshape(rpb.shape[0], 3, n_sub, NA_SUB_Q_ROWS * GRID_W, NA_SUB_K_ROWS * GRID_W)


def _na_kernel(q_ref, k_ref, v_ref, b_ref, o_ref):
    rb = pl.program_id(2)
    nq = NA_SUB_Q_ROWS * GRID_W
    nk = NA_SUB_K_ROWS * GRID_W
    n_sub = NA_Q_ROWS // NA_SUB_Q_ROWS

    def key_start(p):
        row, lo, hi = _na_key_start_row(rb * NA_Q_ROWS + p * NA_SUB_Q_ROWS)
        return pl.multiple_of(jnp.clip(row, lo, hi) * GRID_W, NA_SUB_Q_ROWS * GRID_W)

    def scores(item):
        hl, p = item
        lanes = slice(hl * HEAD_DIM, (hl + 1) * HEAD_DIM)
        q = (q_ref[0, p * nq:(p + 1) * nq, lanes].astype(F32) * (HEAD_DIM ** -0.5 * math.log2(math.e))).astype(BF16)
        return _dot_nt(q, k_ref[0, pl.ds(key_start(p), nk), lanes]) + b_ref[hl, 0, p]

    def weights(s):
        e = jnp.exp2(s - jnp.max(s, axis=-1, keepdims=True))
        return e.astype(BF16), 1.0 / jnp.sum(e, axis=-1, keepdims=True)

    def finish(item, e_inv):
        hl, p = item
        lanes = slice(hl * HEAD_DIM, (hl + 1) * HEAD_DIM)
        e, inv_l = e_inv
        o = _dot(e, v_ref[0, pl.ds(key_start(p), nk), lanes]) * inv_l
        o_ref[0, p * nq:(p + 1) * nq, lanes] = o.astype(o_ref.dtype)

    items = [(hl, p) for hl in range(NA_HEADS_PER_STEP) for p in range(n_sub)]
    s_next = scores(items[0])
    e_prev = None
    for n, item in enumerate(items):
        s_cur = s_next
        if n + 1 < len(items):
            s_next = scores(items[n + 1])
        e_cur = weights(s_cur)
        if e_prev is not None:
            finish(items[n - 1], e_prev)
        e_prev = e_cur
    finish(items[-1], e_prev)


def neighbourhood_attention(proj, col0, bias):
    b, s, _ = proj.shape
    nq = NA_Q_ROWS * GRID_W
    nblk = s // nq
    hs = NA_HEADS_PER_STEP
    w = hs * HEAD_DIM
    c0 = col0 // w
    groups = HEADS // hs

    def bias_map(bi, h, rb):
        return (h, jnp.where(rb == 0, 0, jnp.where(rb == nblk - 1, 2, 1)), 0, 0, 0)

    return pl.pallas_call(
        _na_kernel,
        grid=(b, groups, nblk),
        in_specs=[
            pl.BlockSpec((1, nq, w), lambda bi, h, rb: (bi, rb, c0 + h)),
            pl.BlockSpec((1, s, w), lambda bi, h, rb: (bi, 0, c0 + groups + h)),
            pl.BlockSpec((1, s, w), lambda bi, h, rb: (bi, 0, c0 + 2 * groups + h)),
            pl.BlockSpec((hs, 1) + bias.shape[2:], bias_map),
        ],
        out_specs=pl.BlockSpec((1, nq, w), lambda bi, h, rb: (bi, rb, h)),
        out_shape=jax.ShapeDtypeStruct((b, s, B_W), BF16),
        compiler_params=_params("parallel", "parallel", "arbitrary"),
        name="neighbourhood_attention",
    )(proj, proj, proj, bias)


HALO = SUBLANES_BF16


def _split3(x):
    hi = x.astype(BF16)
    r1 = x - hi.astype(F32)
    mid = r1.astype(BF16)
    lo = (r1 - mid.astype(F32)).astype(BF16)
    return hi, mid, lo


def _dn_prep_kernel(x_ref, xp_ref, xx_ref, cw_ref, gt_ref, alog_ref, dtb_ref,
                    q_ref, k_ref, v_ref, gc_ref, beta_ref, xs_ref):
    i = pl.program_id(1)
    tb = x_ref.shape[1]
    pad = A_CONV_W // 2
    xs_ref[0:HALO, :] = jnp.where(i == 0, 0.0, xp_ref[0].astype(F32))
    xs_ref[HALO:HALO + tb, :] = x_ref[0].astype(F32)
    xs_ref[HALO + tb:, :] = jnp.where(i == pl.num_programs(1) - 1, 0.0, xx_ref[0].astype(F32))
    cw = cw_ref[...]
    y = xs_ref[pl.ds(HALO - pad, tb), :] * cw[0:1]
    for t in range(1, A_CONV_W):
        y = y + xs_ref[pl.ds(HALO - pad + t, tb), :] * cw[t:t + 1]
    y = y * _sigmoid(y)
    for h in range(HEADS):
        qh = y[:, h * HEAD_DIM:(h + 1) * HEAD_DIM]
        kh = y[:, (HEADS + h) * HEAD_DIM:(HEADS + h + 1) * HEAD_DIM]
        qn = qh * (lax.rsqrt(jnp.sum(qh * qh, axis=-1, keepdims=True) + 1e-6) * (HEAD_DIM ** -0.5))
        kn = kh * lax.rsqrt(jnp.sum(kh * kh, axis=-1, keepdims=True) + 1e-6)
        q_ref[0, h] = qn.astype(q_ref.dtype)
        k_ref[0, h] = kn.astype(k_ref.dtype)
        v_ref[0, h] = y[:, (2 * HEADS + h) * HEAD_DIM:(2 * HEADS + h + 1) * HEAD_DIM].astype(v_ref.dtype)

    nd = 2 * HEADS
    gates = gt_ref[0]
    z = gates[:, 0:nd] + dtb_ref[...]
    softplus = jnp.maximum(z, 0.0) + jnp.log(1.0 + jnp.exp(-jnp.abs(z)))
    g = -jnp.exp(alog_ref[...]) * softplus
    beta_ref[0] = _sigmoid(gates[:, nd:2 * nd])
    r = lax.broadcasted_iota(jnp.int32, (tb, tb), 0)
    c = lax.broadcasted_iota(jnp.int32, (tb, tb), 1)
    lower = jnp.where(r >= c, 1.0, 0.0).astype(BF16)
    upper = jnp.where(r <= c, 1.0, 0.0).astype(BF16)
    parts = _split3(g)
    fwd = _dot(lower, parts[0]) + _dot(lower, parts[1]) + _dot(lower, parts[2])
    bwd = _dot(upper, parts[0]) + _dot(upper, parts[1]) + _dot(upper, parts[2])
    col = lax.broadcasted_iota(jnp.int32, (tb, nd), 1)
    gc_ref[0] = jnp.where(col < HEADS, fwd, bwd)


def deltanet_prep(proj, gates, conv_w, a_log, dt_bias, tb=DN_BLOCK):
    b, s, _ = proj.shape
    nb = s // tb
    rh = tb // HALO
    nd = 2 * HEADS
    hs = jax.ShapeDtypeStruct((b, HEADS, s, HEAD_DIM), BF16)
    head_spec = pl.BlockSpec((1, HEADS, tb, HEAD_DIM), lambda bi, i: (bi, 0, i, 0))
    vec_spec = pl.BlockSpec((1, tb, nd), lambda bi, i: (bi, i, 0))
    return pl.pallas_call(
        _dn_prep_kernel,
        grid=(b, nb),
        in_specs=[
            pl.BlockSpec((1, tb, A_QKV_W), lambda bi, i: (bi, i, 0)),
            pl.BlockSpec((1, HALO, A_QKV_W), lambda bi, i: (bi, jnp.maximum(i * rh - 1, 0), 0)),
            pl.BlockSpec((1, HALO, A_QKV_W), lambda bi, i: (bi, jnp.minimum((i + 1) * rh, s // HALO - 1), 0)),
            pl.BlockSpec((A_CONV_W, A_QKV_W), lambda bi, i: (0, 0)),
            pl.BlockSpec((1, tb, LANES), lambda bi, i: (bi, i, 0)),
            pl.BlockSpec((1, nd), lambda bi, i: (0, 0)),
            pl.BlockSpec((1, nd), lambda bi, i: (0, 0)),
        ],
        out_specs=[head_spec, head_spec, head_spec, vec_spec, vec_spec],
        out_shape=[hs, hs, hs, jax.ShapeDtypeStruct((b, s, nd), F32), jax.ShapeDtypeStruct((b, s, nd), F32)],
        scratch_shapes=[pltpu.VMEM((tb + 2 * HALO, A_QKV_W), F32)],
        compiler_params=_params("parallel", "arbitrary"),
        name="deltanet_prep",
    )(proj, proj, proj, conv_w, gates, a_log.reshape(1, nd), dt_bias.reshape(1, nd))


def _split2(x):
    hi = x.astype(BF16)
    return hi, (x - hi.astype(F32)).astype(BF16)


def _unit_triangular_solve(ms, rhss, same_block):
    n = ms[0].shape[0]
    grp = DN_GROUP
    ng = n // grp
    r = lax.broadcasted_iota(jnp.int32, (grp, grp), 0)
    c = lax.broadcasted_iota(jnp.int32, (grp, grp), 1)
    base_mask = (r // DN_BASE) == (c // DN_BASE)
    row = lax.broadcasted_iota(jnp.int32, (DN_BASE, grp), 0)
    lane = lax.broadcasted_iota(jnp.int32, (DN_BASE, grp), 1)
    left = lane < DN_BASE

    def pair_lhs(hi, lo):
        hi_f, lo_f = hi.astype(F32), lo.astype(F32)
        swapped = pltpu.roll(hi_f, DN_BASE, 1)
        top = jnp.concatenate([jnp.where(left, hi_f, swapped), jnp.where(left, lo_f, 0.0)], axis=1)
        bottom = jnp.concatenate([jnp.where(left, swapped, hi_f), jnp.where(left, 0.0, lo_f)], axis=1)
        return jnp.concatenate([top, bottom], axis=0).astype(BF16)

    def pair_product(lhs, rhs_hi, rhs_lo):
        res = _dot(lhs, jnp.concatenate([rhs_hi, rhs_lo, rhs_hi, rhs_hi], axis=0))
        return jnp.where(left, res[:DN_BASE], res[DN_BASE:])

    pieces = [jnp.where(base_mask, m[g * grp:(g + 1) * grp, g * grp:(g + 1) * grp], 0.0)
              for m in ms for g in range(ng)]
    pws = [piece[:DN_BASE] + piece[DN_BASE:] for piece in pieces]
    ts = [jnp.where(lane % DN_BASE == row, 1.0, 0.0) + pw for pw in pws]
    splits = [_split2(pw) for pw in pws]
    lhss = [pair_lhs(hi, lo) for hi, lo in splits]
    size = 2
    while size < DN_BASE:
        pws = [pair_product(lhs, hi, lo) for lhs, (hi, lo) in zip(lhss, splits)]
        splits = [_split2(pw) for pw in pws]
        lhss = [pair_lhs(hi, lo) for hi, lo in splits]
        ts = [t + pair_product(lhs, *_split2(t)) for lhs, t in zip(lhss, ts)]
        size *= 2
    t_bases = [jnp.concatenate([jnp.where(left, t, 0.0), jnp.where(left, 0.0, t)], axis=0).astype(BF16)
               for t in ts]
    offs = {}
    blk = 2 * DN_BASE
    while blk <= n:
        offs[blk] = [jnp.where(same_block(blk), jnp.where(same_block(blk // 2), 0.0, m), 0.0).astype(BF16)
                     for m in ms]
        blk *= 2

    def apply(blk, ys):
        if blk == DN_BASE:
            ybs = [y.astype(BF16) for y in ys]
            return [jnp.concatenate([_dot(t_bases[i * ng + g], yb[g * grp:(g + 1) * grp]) for g in range(ng)],
                                    axis=0) for i, yb in enumerate(ybs)]
        zs = apply(blk // 2, ys)
        corr = apply(blk // 2, [_dot(off, z.astype(BF16)) for off, z in zip(offs[blk], zs)])
        return [z + cr for z, cr in zip(zs, corr)]

    return apply(n, rhss)


def _dn_block(chains):
    n = chains[0][0].shape[0]
    r = lax.broadcasted_iota(jnp.int32, (n, n), 0)
    c = lax.broadcasted_iota(jnp.int32, (n, n), 1)

    def same_block(size):
        return (r // size) == (c // size)

    ms, rhss, pre = [], [], []
    for q, k, v, gcol, grow, beta, state_ref, forward in chains:
        d = (r - c) if forward else (c - r)
        kf = k.astype(F32)
        kb = kf * beta
        decay = jnp.exp(jnp.where(d >= 0, gcol - grow, NEG_BIG))
        ms.append(jnp.where(d > 0, -(_dot_nt(kb.astype(BF16), k) * decay), 0.0))
        intra = (_dot_nt(q, k) * decay).astype(BF16)
        eg = jnp.exp(gcol)
        rhss.append(jnp.concatenate([v.astype(F32) * beta, kb * eg], axis=1))
        g_last = grow[:, n - 1:n] if forward else grow[:, 0:1]
        pre.append((intra, (q.astype(F32) * eg).astype(BF16), (kf * jnp.exp(g_last - gcol)).astype(BF16),
                    jnp.exp(g_last)))
    sols = _unit_triangular_solve(ms, rhss, same_block)
    outs = []
    for chain, sol, (intra, q_dec, k_dec, blk_decay) in zip(chains, sols, pre):
        state_ref = chain[6]
        state = state_ref[...]
        sb = state.astype(BF16)
        v_new = sol[:, :HEAD_DIM] - _dot(sol[:, HEAD_DIM:].astype(BF16), sb)
        vb = v_new.astype(BF16)
        outs.append(_dot(q_dec, sb) + _dot(intra, vb))
        state_ref[...] = state * blk_decay + _dot_tn(k_dec, vb)
    return outs


def _dn_kernel(qf_ref, kf_ref, vf_ref, qb_ref, kb_ref, vb_ref, gcf_ref, gcb_ref, btf_ref, btb_ref,
               grf_ref, grb_ref, of_ref, ob_ref, state_ref):
    hb = qf_ref.shape[1]

    @pl.when(pl.program_id(2) == 0)
    def _():
        state_ref[...] = jnp.zeros_like(state_ref)

    lane = lax.broadcasted_iota(jnp.int32, gcf_ref.shape[1:], 1)

    def pick(ref, idx):
        return jnp.sum(jnp.where(lane == idx, ref[0], 0.0), axis=-1, keepdims=True)

    chains = []
    for hl in range(hb):
        h = pl.program_id(1) * hb + hl
        chains.append((qf_ref[0, hl], kf_ref[0, hl], vf_ref[0, hl], pick(gcf_ref, h),
                       grf_ref[0, pl.ds(h, 1), :], pick(btf_ref, h), state_ref.at[0, hl], True))
        chains.append((qb_ref[0, hl], kb_ref[0, hl], vb_ref[0, hl], pick(gcb_ref, HEADS + h),
                       grb_ref[0, pl.ds(HEADS + h, 1), :], pick(btb_ref, HEADS + h), state_ref.at[1, hl], False))
    outs = _dn_block(chains)
    for hl in range(hb):
        of_ref[0, hl] = outs[2 * hl]
        ob_ref[0, hl] = outs[2 * hl + 1]


def deltanet_scan(q, k, v, gc, beta, gc_rows, tb=DN_BLOCK, hb=DN_HEADS_PER_STEP):
    b, _, s, _ = q.shape
    nb = s // tb
    nd = 2 * HEADS
    fwd = pl.BlockSpec((1, hb, tb, HEAD_DIM), lambda bi, h, c: (bi, h, c, 0))
    bwd = pl.BlockSpec((1, hb, tb, HEAD_DIM), lambda bi, h, c: (bi, h, nb - 1 - c, 0))
    vec_f = pl.BlockSpec((1, tb, nd), lambda bi, h, c: (bi, c, 0))
    vec_b = pl.BlockSpec((1, tb, nd), lambda bi, h, c: (bi, nb - 1 - c, 0))
    row_f = pl.BlockSpec((1, nd, tb), lambda bi, h, c: (bi, 0, c))
    row_b = pl.BlockSpec((1, nd, tb), lambda bi, h, c: (bi, 0, nb - 1 - c))
    os_ = jax.ShapeDtypeStruct((b, HEADS, s, HEAD_DIM), F32)
    return pl.pallas_call(
        _dn_kernel,
        grid=(b, HEADS // hb, nb),
        in_specs=[fwd, fwd, fwd, bwd, bwd, bwd, vec_f, vec_b, vec_f, vec_b, row_f, row_b],
        out_specs=[fwd, bwd],
        out_shape=[os_, os_],
        scratch_shapes=[pltpu.VMEM((2, hb, HEAD_DIM, HEAD_DIM), F32)],
        compiler_params=_params("parallel", "parallel", "arbitrary"),
        name="deltanet_scan",
    )(q, k, v, q, k, v, gc, gc, beta, beta, gc_rows, gc_rows)


def _dn_out_kernel(of_ref, ob_ref, z_ref, nw_ref, o_ref):
    nw = nw_ref[...]
    for h in range(HEADS):
        o = _rms(of_ref[0, h] + ob_ref[0, h], nw)
        z = z_ref[0, :, h * HEAD_DIM:(h + 1) * HEAD_DIM].astype(F32)
        o_ref[0, :, h * HEAD_DIM:(h + 1) * HEAD_DIM] = (o * (z * _sigmoid(z))).astype(o_ref.dtype)


def deltanet_out(o_f, o_b, proj, z_col0, out_norm, ts=ROW_TILE):
    b, _, s, _ = o_f.shape
    head_spec = pl.BlockSpec((1, HEADS, ts, HEAD_DIM), lambda bi, i: (bi, 0, i, 0))
    zb = z_col0 // A_V_W
    return pl.pallas_call(
        _dn_out_kernel,
        grid=(b, s // ts),
        in_specs=[head_spec, head_spec,
                  pl.BlockSpec((1, ts, A_V_W), lambda bi, i: (bi, i, zb)),
                  pl.BlockSpec((1, HEAD_DIM), lambda bi, i: (0, 0))],
        out_specs=pl.BlockSpec((1, ts, A_V_W), lambda bi, i: (bi, i, 0)),
        out_shape=jax.ShapeDtypeStruct((b, s, A_V_W), BF16),
        compiler_params=_params("parallel", "parallel"),
        name="deltanet_out",
    )(o_f, o_b, proj, out_norm.reshape(1, HEAD_DIM))


def _rope_tables(s):
    half = ROPE_DIMS // 2
    inv = ROPE_THETA ** (-jnp.arange(0, ROPE_DIMS, 2, dtype=F32) / ROPE_DIMS)
    ang = jnp.arange(s, dtype=F32)[:, None] * inv[None, :]
    cos, sin = jnp.cos(ang), jnp.sin(ang)
    rest = HEAD_DIM - ROPE_DIMS
    c = jnp.concatenate([cos, cos, jnp.ones((s, rest), F32)], axis=1)
    s1 = jnp.concatenate([jnp.zeros((s, half), F32), sin, jnp.zeros((s, rest), F32)], axis=1)
    s2 = jnp.concatenate([-sin, jnp.zeros((s, half + rest), F32)], axis=1)
    return c, s1, s2


def _trunk(x_a, x_b, p_a, p_b, ab_w_in, ab_conv_w, ab_a_log, ab_dt_bias, ab_out_norm, ab_rpb, ab_w_out,
           c_w_in, c_lambda, c_subln, c_w_out, norms, ffn_w_in, ffn_conv_w, ffn_conv_b,
           ffn_w_out, ple_w_proj, ple_w_gate):
    b_a, s, d = x_a.shape
    b = b_a + x_b.shape[0]
    t, t_a = b * s, b_a * s
    depth = norms.shape[0]
    depth_p = p_a.shape[0]
    h = jnp.concatenate([x_a, x_b], axis=0).reshape(t, d)
    rope = _rope_tables(s)
    o1 = A_QKV_W
    o2 = o1 + A_V_W
    o3 = o2 + 4 * HEADS
    ab_main = _column_tiles(jnp.concatenate([ab_w_in[:, :, :o2], ab_w_in[:, :, o3:]], axis=2), COL_TILE)
    ab_gate = _column_tiles(jnp.pad(ab_w_in[:, :, o2:o3], ((0, 0), (0, 0), (0, LANES - 4 * HEADS))), LANES)
    c_in = _column_tiles(c_w_in, COL_TILE)
    ffn_in = _column_tiles(ffn_w_in, FF_TILE)
    ab_out, c_out, ffn_out = ab_w_out.astype(BF16), c_w_out.astype(BF16), ffn_w_out.astype(BF16)
    ple_gate, ple_proj = ple_w_gate.astype(BF16), ple_w_proj.astype(BF16)
    for layer in range(depth):
        j = layer // 2
        if layer % 2 == 0:
            proj = norm_matmul(h, norms[layer, 0], ab_main, j, BF16).reshape(b, s, -1)
            gates = norm_matmul(h, norms[layer, 0], ab_gate, j, F32).reshape(b, s, LANES)
            qa, ka, va, gc, beta = deltanet_prep(proj, gates, ab_conv_w[j], ab_a_log[j], ab_dt_bias[j])
            o_f, o_b = deltanet_scan(qa, ka, va, gc, beta, jnp.transpose(gc, (0, 2, 1)))
            o_a = deltanet_out(o_f, o_b, proj, o1, ab_out_norm[j])
            o_nb = neighbourhood_attention(proj, o2, _na_bias(ab_rpb[j]))
            h, xn = proj_norm_res([o_a.reshape(t, -1), o_nb.reshape(t, -1)], ab_out, j, norms[layer, 1],
                                  norms[layer, 2], h)
        else:
            proj = norm_matmul(h, norms[layer, 0], c_in, j, BF16, rope=rope,
                               rope_tiles=2 * C_QK_W // COL_TILE, seq_len=s).reshape(b, s, -1)
            o_c = diff_attention(proj, c_lambda[j], c_subln[j], layer)
            h, xn = proj_norm_res([o_c.reshape(t, -1)], c_out, j, norms[layer, 1], norms[layer, 2], h)
        f = conv_ffn(xn, ffn_in, ffn_out, layer, ffn_conv_w[layer], ffn_conv_b[layer], norms[layer, 3], s)
        ple_args = (f, p_a.reshape(depth_p, t_a, -1), p_b.reshape(depth_p, t - t_a, -1), ple_gate, ple_proj, layer)
        if layer + 1 < depth:
            h = ple(h, *ple_args)
    y_a = ple(h, *ple_args, row0=0, rows=t_a)
    y_b = ple(h, *ple_args, row0=t_a, rows=t - t_a)
    return y_a.reshape(b_a, s, d), y_b.reshape(b - b_a, s, d)


def kernel(x_prompt, x_sample, p_prompt, p_sample, ab_w_in, ab_conv_w, ab_a_log, ab_dt_bias, ab_out_norm,
           ab_rpb, ab_w_out, c_w_in, c_lambda, c_subln, c_w_out, norms, ffn_w_in, ffn_conv_w, ffn_conv_b,
           ffn_w_out, ple_w_proj, ple_w_gate):
    return _trunk(x_prompt, x_sample, p_prompt, p_sample, ab_w_in, ab_conv_w, ab_a_log, ab_dt_bias,
                  ab_out_norm, ab_rpb, ab_w_out, c_w_in, c_lambda, c_subln, c_w_out, norms, ffn_w_in,
                  ffn_conv_w, ffn_conv_b, ffn_w_out, ple_w_proj, ple_w_gate)
```

```python
import functools
import math

import jax
import jax.numpy as jnp
import numpy as np
from jax import lax
from jax.experimental import pallas as pl
from jax.experimental.pallas import tpu as pltpu

F32 = jnp.float32
BF16 = jnp.bfloat16

D_MODEL = 2048
PLE_DIM = 256
GRID_W = 64
RMS_EPS = 1e-6
HEADS = 8
HEAD_DIM = 128
A_CONV_W = 5
NA_ROWS = 8
NA_COLS = 16
ROPE_THETA = 500000.0
ROPE_DIMS = HEAD_DIM // 4
D_FF = 8192
FFN_CONV_W = 3
A_QKV_W = 3 * HEADS * HEAD_DIM
A_V_W = HEADS * HEAD_DIM
B_W = HEADS * HEAD_DIM
C_QK_W = HEADS * 2 * HEAD_DIM

LANES = 128
SUBLANES_BF16 = 16
VMEM_LIMIT = 56 * 1024 * 1024
NEG_BIG = -1e30

ROW_TILE = 512
MM_ROW_TILE = 1024
COL_TILE = 1024
FF_TILE = 512
FFN_ROW_TILE = 1024
FFN_ROW_PIECE = 256
DN_BLOCK = 256
DN_BASE = 64
DN_GROUP = 128
DN_HEADS_PER_STEP = 4
ATT_Q_TILE = 1024
ATT_SUB_ROWS = 256
NA_Q_ROWS = 8
NA_HEADS_PER_STEP = 4
NA_SUB_Q_ROWS = 4
NA_SUB_K_ROWS = 12


def _params(*sem):
    return pltpu.CompilerParams(dimension_semantics=sem, vmem_limit_bytes=VMEM_LIMIT)


def _rms(x, w):
    return x * lax.rsqrt(jnp.mean(x * x, axis=-1, keepdims=True) + RMS_EPS) * w


def _dot(a, b):
    return jnp.dot(a, b, preferred_element_type=F32)


def _dot_nt(a, b):
    return lax.dot_general(a, b, (((1,), (1,)), ((), ())), preferred_element_type=F32)


def _dot_tn(a, b):
    return lax.dot_general(a, b, (((0,), (0,)), ((), ())), preferred_element_type=F32)


def _sigmoid(x):
    return 1.0 / (1.0 + jnp.exp(-x))


def _norm_matmul_kernel(*refs, rope_tiles):
    if rope_tiles:
        x_ref, nw_ref, w_ref, rc_ref, rs1_ref, rs2_ref, o_ref, xn_ref = refs
    else:
        x_ref, nw_ref, w_ref, o_ref, xn_ref = refs
    j = pl.program_id(1)

    @pl.when(j == 0)
    def _():
        xn_ref[...] = _rms(x_ref[...], nw_ref[...]).astype(BF16)

    y = _dot(xn_ref[...], w_ref[0, 0])
    if not rope_tiles:
        o_ref[...] = y.astype(o_ref.dtype)
        return

    @pl.when(j < rope_tiles)
    def _():
        n = y.shape[1]
        reps = n // LANES
        c = jnp.concatenate([rc_ref[...]] * reps, axis=1)
        s1 = jnp.concatenate([rs1_ref[...]] * reps, axis=1)
        s2 = jnp.concatenate([rs2_ref[...]] * reps, axis=1)
        half = ROPE_DIMS // 2
        r = y * c + pltpu.roll(y, half, 1) * s1 + pltpu.roll(y, n - half, 1) * s2
        o_ref[...] = r.astype(o_ref.dtype)

    @pl.when(j >= rope_tiles)
    def _():
        o_ref[...] = y.astype(o_ref.dtype)


def _column_tiles(w, tn):
    nl, k, n = w.shape
    return jnp.transpose(w.reshape(nl, k, n // tn, tn), (0, 2, 1, 3)).astype(BF16)


def norm_matmul(x, nw, w_tiles, layer, out_dtype, rope=None, rope_tiles=0, seq_len=None, tm=MM_ROW_TILE):
    t, k = x.shape
    _, n_tiles, _, tn = w_tiles.shape
    n = n_tiles * tn
    tm = min(tm, t)
    assert t % tm == 0
    in_specs = [
        pl.BlockSpec((tm, k), lambda i, j: (i, 0)),
        pl.BlockSpec((1, k), lambda i, j: (0, 0)),
        pl.BlockSpec((1, 1, k, tn), lambda i, j: (layer, j, 0, 0)),
    ]
    args = [x, nw.reshape(1, k), w_tiles]
    if rope_tiles:
        per_seq = seq_len // tm
        for tab in rope:
            in_specs.append(pl.BlockSpec((tm, LANES), lambda i, j: (i % per_seq, 0)))
            args.append(tab)
    return pl.pallas_call(
        functools.partial(_norm_matmul_kernel, rope_tiles=rope_tiles),
        grid=(t // tm, n // tn),
        in_specs=in_specs,
        out_specs=pl.BlockSpec((tm, tn), lambda i, j: (i, j)),
        out_shape=jax.ShapeDtypeStruct((t, n), out_dtype),
        scratch_shapes=[pltpu.VMEM((tm, k), BF16)],
        compiler_params=_params("parallel", "arbitrary"),
        name="norm_matmul",
    )(*args)


def _proj_norm_res_kernel(*refs, n_in):
    a_refs = refs[:n_in]
    w_refs = refs[n_in:2 * n_in]
    nw_ref, nw_next_ref, h_ref, o_ref, xn_ref = refs[2 * n_in:]
    y = _dot(a_refs[0][...], w_refs[0][0])
    for a_ref, w_ref in zip(a_refs[1:], w_refs[1:]):
        y = y + _dot(a_ref[...], w_ref[0])
    h_new = h_ref[...] + _rms(y, nw_ref[...])
    o_ref[...] = h_new
    xn_ref[...] = _rms(h_new, nw_next_ref[...]).astype(BF16)


def proj_norm_res(acts, w, layer, nw, nw_next, h, tm=ROW_TILE):
    t, n = h.shape
    n_in = len(acts)
    kw = acts[0].shape[1]
    assert all(a.shape[1] == kw for a in acts) and w.shape[1] == n_in * kw
    in_specs = [pl.BlockSpec((tm, kw), lambda i: (i, 0)) for _ in acts]
    in_specs += [pl.BlockSpec((1, kw, n), functools.partial(lambda piece, i: (layer, piece, 0), piece))
                 for piece in range(n_in)]
    in_specs += [pl.BlockSpec((1, n), lambda i: (0, 0)), pl.BlockSpec((1, n), lambda i: (0, 0)),
                 pl.BlockSpec((tm, n), lambda i: (i, 0))]
    ws = [w] * n_in
    row_spec = pl.BlockSpec((tm, n), lambda i: (i, 0))
    return pl.pallas_call(
        functools.partial(_proj_norm_res_kernel, n_in=n_in),
        grid=(t // tm,),
        in_specs=in_specs,
        out_specs=[row_spec, row_spec],
        out_shape=[jax.ShapeDtypeStruct((t, n), F32), jax.ShapeDtypeStruct((t, n), BF16)],
        compiler_params=_params("parallel"),
        name="proj_norm_res",
    )(*acts, *ws, nw.reshape(1, n), nw_next.reshape(1, n), h)


def _ple_kernel(h_ref, f_ref, pa_ref, pb_ref, wg_ref, wp_ref, o_ref, *, first_block, a_blocks):
    h = h_ref[...] + f_ref[...]
    gate = _sigmoid(_dot(h.astype(BF16), wg_ref[0]))
    p = jnp.where(first_block + pl.program_id(0) < a_blocks, pa_ref[0], pb_ref[0])
    o_ref[...] = h + gate * _dot(p.astype(BF16), wp_ref[0])


def ple(h, f, p_a, p_b, wg, wp, layer, row0=0, rows=None, tm=ROW_TILE):
    n = h.shape[1]
    rows = h.shape[0] if rows is None else rows
    assert row0 % tm == 0 and rows % tm == 0 and p_a.shape[1] % tm == 0 and p_b.shape[1] % tm == 0
    i0 = row0 // tm
    na, nb = p_a.shape[1] // tm, p_b.shape[1] // tm
    return pl.pallas_call(
        functools.partial(_ple_kernel, first_block=i0, a_blocks=na),
        grid=(rows // tm,),
        in_specs=[
            pl.BlockSpec((tm, n), lambda i: (i0 + i, 0)),
            pl.BlockSpec((tm, n), lambda i: (i0 + i, 0)),
            pl.BlockSpec((1, tm, p_a.shape[2]), lambda i: (layer, jnp.minimum(i0 + i, na - 1), 0)),
            pl.BlockSpec((1, tm, p_b.shape[2]), lambda i: (layer, jnp.clip(i0 + i - na, 0, nb - 1), 0)),
            pl.BlockSpec((1,) + wg.shape[1:], lambda i: (layer, 0, 0)),
            pl.BlockSpec((1,) + wp.shape[1:], lambda i: (layer, 0, 0)),
        ],
        out_specs=pl.BlockSpec((tm, n), lambda i: (i, 0)),
        out_shape=jax.ShapeDtypeStruct((rows, n), F32),
        compiler_params=_params("parallel"),
        name="ple",
    )(h, f, p_a, p_b, wg, wp)


def _ffn_kernel(x_ref, xp_ref, xx_ref, wg_ref, wu_ref, cwg_ref, cwu_ref, cbg_ref, cbu_ref,
                wo_ref, nw2_ref, o_ref, xn_ref, hg_ref, hu_ref, *, tm, per_seq):
    i = pl.program_id(0)
    j = pl.program_id(1)

    @pl.when(j == 0)
    def _():
        pos = i % per_seq
        half = SUBLANES_BF16 // 2
        prev = jnp.where(pos == 0, 0.0, xp_ref[...].astype(F32))[half:]
        nxt = jnp.where(pos == per_seq - 1, 0.0, xx_ref[...].astype(F32))[:half]
        xn_ref[0:tm, :] = x_ref[...]
        xn_ref[tm:, :] = jnp.concatenate([nxt, prev], axis=0).astype(BF16)
        o_ref[...] = jnp.zeros_like(o_ref)

    xe = xn_ref[...]

    for w_ref, h_ref in ((wg_ref, hg_ref), (wu_ref, hu_ref)):
        hid = _dot(xe, w_ref[0, 0])
        h_ref[8:, :] = hid
        h_ref[0:8, :] = hid[tm + 8:, :]

    def conv(h_ref, cw_ref, cb_ref, r0, rows):
        cw = cw_ref[...]
        return (h_ref[pl.ds(7 + r0, rows), :] * cw[0:1] + h_ref[pl.ds(8 + r0, rows), :] * cw[1:2]
                + h_ref[pl.ds(9 + r0, rows), :] * cw[2:3] + cb_ref[...])

    rows = min(FFN_ROW_PIECE, tm)
    for r0 in range(0, tm, rows):
        g = conv(hg_ref, cwg_ref, cbg_ref, r0, rows)
        u = conv(hu_ref, cwu_ref, cbu_ref, r0, rows)
        c0 = math.sqrt(2.0 / math.pi)
        gelu = 0.5 * g * (1.0 + jnp.tanh(c0 * (g + 0.044715 * (g * g * g))))
        o_ref[r0:r0 + rows, :] += _dot((gelu * u).astype(BF16), wo_ref[0])

    @pl.when(j == pl.num_programs(1) - 1)
    def _():
        o_ref[...] = _rms(o_ref[...], nw2_ref[...])


def conv_ffn(xn, w_in, w_out, layer, conv_w, conv_b, nw_out, seq_len, tm=FFN_ROW_TILE):
    t, d = xn.shape
    f = w_out.shape[1]
    tf = w_in.shape[3]
    nf = f // tf
    per_seq = seq_len // tm
    rg = tm // SUBLANES_BF16
    last_group = t // SUBLANES_BF16 - 1
    cb = conv_b.reshape(1, 2 * f)
    return pl.pallas_call(
        functools.partial(_ffn_kernel, tm=tm, per_seq=per_seq),
        grid=(t // tm, nf),
        in_specs=[
            pl.BlockSpec((tm, d), lambda i, j: (i, 0)),
            pl.BlockSpec((SUBLANES_BF16, d), lambda i, j: (jnp.maximum(i * rg - 1, 0), 0)),
            pl.BlockSpec((SUBLANES_BF16, d), lambda i, j: (jnp.minimum((i + 1) * rg, last_group), 0)),
            pl.BlockSpec((1, 1, d, tf), lambda i, j: (layer, j, 0, 0)),
            pl.BlockSpec((1, 1, d, tf), lambda i, j: (layer, j + nf, 0, 0)),
            pl.BlockSpec((FFN_CONV_W, tf), lambda i, j: (0, j)),
            pl.BlockSpec((FFN_CONV_W, tf), lambda i, j: (0, j + nf)),
            pl.BlockSpec((1, tf), lambda i, j: (0, j)),
            pl.BlockSpec((1, tf), lambda i, j: (0, j + nf)),
            pl.BlockSpec((1, tf, d), lambda i, j: (layer, j, 0)),
            pl.BlockSpec((1, d), lambda i, j: (0, 0)),
        ],
        out_specs=pl.BlockSpec((tm, d), lambda i, j: (i, 0)),
        out_shape=jax.ShapeDtypeStruct((t, d), F32),
        scratch_shapes=[
            pltpu.VMEM((tm + SUBLANES_BF16, d), BF16),
            pltpu.VMEM((8 + tm + SUBLANES_BF16, tf), F32),
            pltpu.VMEM((8 + tm + SUBLANES_BF16, tf), F32),
        ],
        compiler_params=_params("parallel", "arbitrary"),
        name="conv_ffn",
    )(xn, xn, xn, w_in, w_in, conv_w, conv_w, cb, cb, w_out, nw_out.reshape(1, d))


def _diff_attn_kernel(lp_ref, sub_ref, q_ref, k_ref, v_ref, o_ref, *, lambda_init):
    lp = lp_ref[...]
    lam = (jnp.exp(jnp.sum(lp[0:1] * lp[1:2], axis=-1, keepdims=True))
           - jnp.exp(jnp.sum(lp[2:3] * lp[3:4], axis=-1, keepdims=True)) + lambda_init)
    sub = sub_ref[...]
    n_sub = q_ref.shape[1] // ATT_SUB_ROWS

    def scores(i):
        q = q_ref[0, i * ATT_SUB_ROWS:(i + 1) * ATT_SUB_ROWS, :].astype(F32)
        q = (q * (HEAD_DIM ** -0.5 * math.log2(math.e))).astype(BF16)
        return [_dot_nt(q[:, lo:lo + HEAD_DIM], k_ref[0, :, lo:lo + HEAD_DIM]) for lo in (0, HEAD_DIM)]

    def weights(s12):
        es = [jnp.exp2(s - jnp.max(s, axis=-1, keepdims=True)) for s in s12]
        l1, l2 = [jnp.sum(e, axis=-1, keepdims=True) for e in es]
        return (es[0] - es[1] * (lam * l1 / l2)).astype(BF16), 1.0 / l1

    def finish(i, a_inv):
        a, inv_l1 = a_inv
        o = _dot(a, v_ref[0]) * inv_l1
        o_ref[0, i * ATT_SUB_ROWS:(i + 1) * ATT_SUB_ROWS, :] = (_rms(o, sub) * (1.0 - lambda_init)).astype(o_ref.dtype)

    s_next = scores(0)
    a_prev = None
    for i in range(n_sub):
        s_cur = s_next
        if i + 1 < n_sub:
            s_next = scores(i + 1)
        a_cur = weights(s_cur)
        if a_prev is not None:
            finish(i - 1, a_prev)
        a_prev = a_cur
    finish(n_sub - 1, a_prev)


def diff_attention(proj, lam_params, subln, layer, tq=ATT_Q_TILE):
    b, s, _ = proj.shape
    w = 2 * HEAD_DIM
    tq = min(tq, s)
    lambda_init = 0.8 - 0.6 * math.exp(-0.3 * layer)
    return pl.pallas_call(
        functools.partial(_diff_attn_kernel, lambda_init=lambda_init),
        grid=(b, HEADS, s // tq),
        in_specs=[
            pl.BlockSpec((4, HEAD_DIM), lambda bi, h, qi: (0, 0)),
            pl.BlockSpec((1, w), lambda bi, h, qi: (0, 0)),
            pl.BlockSpec((1, tq, w), lambda bi, h, qi: (bi, qi, h)),
            pl.BlockSpec((1, s, w), lambda bi, h, qi: (bi, 0, HEADS + h)),
            pl.BlockSpec((1, s, w), lambda bi, h, qi: (bi, 0, 2 * HEADS + h)),
        ],
        out_specs=pl.BlockSpec((1, tq, w), lambda bi, h, qi: (bi, qi, h)),
        out_shape=jax.ShapeDtypeStruct((b, s, C_QK_W), BF16),
        compiler_params=_params("parallel", "parallel", "arbitrary"),
        name="diff_attention",
    )(lam_params, subln.reshape(1, w), proj, proj, proj)


def _na_key_start_row(first_query_row):
    return (first_query_row - NA_ROWS // 2, 0, GRID_W - NA_SUB_K_ROWS)


def _na_bias(rpb):
    rows = GRID_W
    n_sub = NA_Q_ROWS // NA_SUB_Q_ROWS
    n_dr = 2 * NA_ROWS - 1
    c = np.arange(GRID_W)
    col_start = np.clip(c - NA_COLS // 2, 0, GRID_W - NA_COLS)
    col_ok = (c[None, :] >= col_start[:, None]) & (c[None, :] < col_start[:, None] + NA_COLS)
    dc = np.clip(c[None, :] - c[:, None] + NA_COLS - 1, 0, 2 * NA_COLS - 2)
    sel_c = (np.arange(2 * NA_COLS - 1)[:, None, None] == dc[None]).astype(np.float32)
    tables = jnp.einsum("hab,bxy->haxy", rpb, sel_c, precision=lax.Precision.HIGHEST) * math.log2(math.e)
    tables = jnp.where(col_ok[None, None], tables, -jnp.inf)
    masked = jnp.full(tables.shape[:1] + tables.shape[2:], -jnp.inf, tables.dtype)
    blocks = []
    for rb in (0, 1, rows // NA_Q_ROWS - 1):
        for p in range(n_sub):
            r0 = rb * NA_Q_ROWS + p * NA_SUB_Q_ROWS
            base = int(np.clip(*_na_key_start_row(r0)))
            q_rows = []
            for ql in range(NA_SUB_Q_ROWS):
                r = r0 + ql
                row_start = int(np.clip(r - NA_ROWS // 2, 0, rows - NA_ROWS))
                assert base <= row_start and row_start + NA_ROWS <= base + NA_SUB_K_ROWS
                tiles = []
                for kl in range(NA_SUB_K_ROWS):
                    kr = base + kl
                    inside = row_start <= kr < row_start + NA_ROWS
                    assert not inside or 0 <= kr - r + NA_ROWS - 1 < n_dr
                    tiles.append(tables[:, kr - r + NA_ROWS - 1] if inside else masked)
                q_rows.append(jnp.concatenate(tiles, axis=-1))
            blocks.append(jnp.concatenate(q_rows, axis=-2))
    bias = jnp.stack(blocks, axis=1)
    return bias.reshape(rpb.shape[0], 3, n_sub, NA_SUB_Q_ROWS * GRID_W, NA_SUB_K_ROWS * GRID_W)


def _na_kernel(q_ref, k_ref, v_ref, b_ref, o_ref):
    rb = pl.program_id(2)
    nq = NA_SUB_Q_ROWS * GRID_W
    nk = NA_SUB_K_ROWS * GRID_W
    n_sub = NA_Q_ROWS // NA_SUB_Q_ROWS

    def key_start(p):
        row, lo, hi = _na_key_start_row(rb * NA_Q_ROWS + p * NA_SUB_Q_ROWS)
        return pl.multiple_of(jnp.clip(row, lo, hi) * GRID_W, NA_SUB_Q_ROWS * GRID_W)

    def scores(item):
        hl, p = item
        lanes = slice(hl * HEAD_DIM, (hl + 1) * HEAD_DIM)
        q = (q_ref[0, p * nq:(p + 1) * nq, lanes].astype(F32) * (HEAD_DIM ** -0.5 * math.log2(math.e))).astype(BF16)
        return _dot_nt(q, k_ref[0, pl.ds(key_start(p), nk), lanes]) + b_ref[hl, 0, p]

    def weights(s):
        e = jnp.exp2(s - jnp.max(s, axis=-1, keepdims=True))
        return e.astype(BF16), 1.0 / jnp.sum(e, axis=-1, keepdims=True)

    def finish(item, e_inv):
        hl, p = item
        lanes = slice(hl * HEAD_DIM, (hl + 1) * HEAD_DIM)
        e, inv_l = e_inv
        o = _dot(e, v_ref[0, pl.ds(key_start(p), nk), lanes]) * inv_l
        o_ref[0, p * nq:(p + 1) * nq, lanes] = o.astype(o_ref.dtype)

    items = [(hl, p) for hl in range(NA_HEADS_PER_STEP) for p in range(n_sub)]
    s_next = scores(items[0])
    e_prev = None
    for n, item in enumerate(items):
        s_cur = s_next
        if n + 1 < len(items):
            s_next = scores(items[n + 1])
        e_cur = weights(s_cur)
        if e_prev is not None:
            finish(items[n - 1], e_prev)
        e_prev = e_cur
    finish(items[-1], e_prev)


def neighbourhood_attention(proj, col0, bias):
    b, s, _ = proj.shape
    nq = NA_Q_ROWS * GRID_W
    nblk = s // nq
    hs = NA_HEADS_PER_STEP
    w = hs * HEAD_DIM
    c0 = col0 // w
    groups = HEADS // hs

    def bias_map(bi, h, rb):
        return (h, jnp.where(rb == 0, 0, jnp.where(rb == nblk - 1, 2, 1)), 0, 0, 0)

    return pl.pallas_call(
        _na_kernel,
        grid=(b, groups, nblk),
        in_specs=[
            pl.BlockSpec((1, nq, w), lambda bi, h, rb: (bi, rb, c0 + h)),
            pl.BlockSpec((1, s, w), lambda bi, h, rb: (bi, 0, c0 + groups + h)),
            pl.BlockSpec((1, s, w), lambda bi, h, rb: (bi, 0, c0 + 2 * groups + h)),
            pl.BlockSpec((hs, 1) + bias.shape[2:], bias_map),
        ],
        out_specs=pl.BlockSpec((1, nq, w), lambda bi, h, rb: (bi, rb, h)),
        out_shape=jax.ShapeDtypeStruct((b, s, B_W), BF16),
        compiler_params=_params("parallel", "parallel", "arbitrary"),
        name="neighbourhood_attention",
    )(proj, proj, proj, bias)


HALO = SUBLANES_BF16


def _split3(x):
    hi = x.astype(BF16)
    r1 = x - hi.astype(F32)
    mid = r1.astype(BF16)
    lo = (r1 - mid.astype(F32)).astype(BF16)
    return hi, mid, lo


def _dn_prep_kernel(x_ref, xp_ref, xx_ref, cw_ref, gt_ref, alog_ref, dtb_ref,
                    q_ref, k_ref, v_ref, gc_ref, beta_ref, xs_ref):
    i = pl.program_id(1)
    tb = x_ref.shape[1]
    pad = A_CONV_W // 2
    xs_ref[0:HALO, :] = jnp.where(i == 0, 0.0, xp_ref[0].astype(F32))
    xs_ref[HALO:HALO + tb, :] = x_ref[0].astype(F32)
    xs_ref[HALO + tb:, :] = jnp.where(i == pl.num_programs(1) - 1, 0.0, xx_ref[0].astype(F32))
    cw = cw_ref[...]
    y = xs_ref[pl.ds(HALO - pad, tb), :] * cw[0:1]
    for t in range(1, A_CONV_W):
        y = y + xs_ref[pl.ds(HALO - pad + t, tb), :] * cw[t:t + 1]
    y = y * _sigmoid(y)
    for h in range(HEADS):
        qh = y[:, h * HEAD_DIM:(h + 1) * HEAD_DIM]
        kh = y[:, (HEADS + h) * HEAD_DIM:(HEADS + h + 1) * HEAD_DIM]
        qn = qh * (lax.rsqrt(jnp.sum(qh * qh, axis=-1, keepdims=True) + 1e-6) * (HEAD_DIM ** -0.5))
        kn = kh * lax.rsqrt(jnp.sum(kh * kh, axis=-1, keepdims=True) + 1e-6)
        q_ref[0, h] = qn.astype(q_ref.dtype)
        k_ref[0, h] = kn.astype(k_ref.dtype)
        v_ref[0, h] = y[:, (2 * HEADS + h) * HEAD_DIM:(2 * HEADS + h + 1) * HEAD_DIM].astype(v_ref.dtype)

    nd = 2 * HEADS
    gates = gt_ref[0]
    z = gates[:, 0:nd] + dtb_ref[...]
    softplus = jnp.maximum(z, 0.0) + jnp.log(1.0 + jnp.exp(-jnp.abs(z)))
    g = -jnp.exp(alog_ref[...]) * softplus
    beta_ref[0] = _sigmoid(gates[:, nd:2 * nd])
    r = lax.broadcasted_iota(jnp.int32, (tb, tb), 0)
    c = lax.broadcasted_iota(jnp.int32, (tb, tb), 1)
    lower = jnp.where(r >= c, 1.0, 0.0).astype(BF16)
    upper = jnp.where(r <= c, 1.0, 0.0).astype(BF16)
    parts = _split3(g)
    fwd = _dot(lower, parts[0]) + _dot(lower, parts[1]) + _dot(lower, parts[2])
    bwd = _dot(upper, parts[0]) + _dot(upper, parts[1]) + _dot(upper, parts[2])
    col = lax.broadcasted_iota(jnp.int32, (tb, nd), 1)
    gc_ref[0] = jnp.where(col < HEADS, fwd, bwd)


def deltanet_prep(proj, gates, conv_w, a_log, dt_bias, tb=DN_BLOCK):
    b, s, _ = proj.shape
    nb = s // tb
    rh = tb // HALO
    nd = 2 * HEADS
    hs = jax.ShapeDtypeStruct((b, HEADS, s, HEAD_DIM), BF16)
    head_spec = pl.BlockSpec((1, HEADS, tb, HEAD_DIM), lambda bi, i: (bi, 0, i, 0))
    vec_spec = pl.BlockSpec((1, tb, nd), lambda bi, i: (bi, i, 0))
    return pl.pallas_call(
        _dn_prep_kernel,
        grid=(b, nb),
        in_specs=[
            pl.BlockSpec((1, tb, A_QKV_W), lambda bi, i: (bi, i, 0)),
            pl.BlockSpec((1, HALO, A_QKV_W), lambda bi, i: (bi, jnp.maximum(i * rh - 1, 0), 0)),
            pl.BlockSpec((1, HALO, A_QKV_W), lambda bi, i: (bi, jnp.minimum((i + 1) * rh, s // HALO - 1), 0)),
            pl.BlockSpec((A_CONV_W, A_QKV_W), lambda bi, i: (0, 0)),
            pl.BlockSpec((1, tb, LANES), lambda bi, i: (bi, i, 0)),
            pl.BlockSpec((1, nd), lambda bi, i: (0, 0)),
            pl.BlockSpec((1, nd), lambda bi, i: (0, 0)),
        ],
        out_specs=[head_spec, head_spec, head_spec, vec_spec, vec_spec],
        out_shape=[hs, hs, hs, jax.ShapeDtypeStruct((b, s, nd), F32), jax.ShapeDtypeStruct((b, s, nd), F32)],
        scratch_shapes=[pltpu.VMEM((tb + 2 * HALO, A_QKV_W), F32)],
        compiler_params=_params("parallel", "arbitrary"),
        name="deltanet_prep",
    )(proj, proj, proj, conv_w, gates, a_log.reshape(1, nd), dt_bias.reshape(1, nd))


def _split2(x):
    hi = x.astype(BF16)
    return hi, (x - hi.astype(F32)).astype(BF16)


def _unit_triangular_solve(ms, rhss, same_block):
    n = ms[0].shape[0]
    grp = DN_GROUP
    ng = n // grp
    r = lax.broadcasted_iota(jnp.int32, (grp, grp), 0)
    c = lax.broadcasted_iota(jnp.int32, (grp, grp), 1)
    base_mask = (r // DN_BASE) == (c // DN_BASE)
    row = lax.broadcasted_iota(jnp.int32, (DN_BASE, grp), 0)
    lane = lax.broadcasted_iota(jnp.int32, (DN_BASE, grp), 1)
    left = lane < DN_BASE

    def pair_lhs(hi, lo):
        hi_f, lo_f = hi.astype(F32), lo.astype(F32)
        swapped = pltpu.roll(hi_f, DN_BASE, 1)
        top = jnp.concatenate([jnp.where(left, hi_f, swapped), jnp.where(left, lo_f, 0.0)], axis=1)
        bottom = jnp.concatenate([jnp.where(left, swapped, hi_f), jnp.where(left, 0.0, lo_f)], axis=1)
        return jnp.concatenate([top, bottom], axis=0).astype(BF16)

    def pair_product(lhs, rhs_hi, rhs_lo):
        res = _dot(lhs, jnp.concatenate([rhs_hi, rhs_lo, rhs_hi, rhs_hi], axis=0))
        return jnp.where(left, res[:DN_BASE], res[DN_BASE:])

    pieces = [jnp.where(base_mask, m[g * grp:(g + 1) * grp, g * grp:(g + 1) * grp], 0.0)
              for m in ms for g in range(ng)]
    pws = [piece[:DN_BASE] + piece[DN_BASE:] for piece in pieces]
    ts = [jnp.where(lane % DN_BASE == row, 1.0, 0.0) + pw for pw in pws]
    splits = [_split2(pw) for pw in pws]
    lhss = [pair_lhs(hi, lo) for hi, lo in splits]
    size = 2
    while size < DN_BASE:
        pws = [pair_product(lhs, hi, lo) for lhs, (hi, lo) in zip(lhss, splits)]
        splits = [_split2(pw) for pw in pws]
        lhss = [pair_lhs(hi, lo) for hi, lo in splits]
        ts = [t + pair_product(lhs, *_split2(t)) for lhs, t in zip(lhss, ts)]
        size *= 2
    t_bases = [jnp.concatenate([jnp.where(left, t, 0.0), jnp.where(left, 0.0, t)], axis=0).astype(BF16)
               for t in ts]
    offs = {}
    blk = 2 * DN_BASE
    while blk <= n:
        offs[blk] = [jnp.where(same_block(blk), jnp.where(same_block(blk // 2), 0.0, m), 0.0).astype(BF16)
                     for m in ms]
        blk *= 2

    def apply(blk, ys):
        if blk == DN_BASE:
            ybs = [y.astype(BF16) for y in ys]
            return [jnp.concatenate([_dot(t_bases[i * ng + g], yb[g * grp:(g + 1) * grp]) for g in range(ng)],
                                    axis=0) for i, yb in enumerate(ybs)]
        zs = apply(blk // 2, ys)
        corr = apply(blk // 2, [_dot(off, z.astype(BF16)) for off, z in zip(offs[blk], zs)])
        return [z + cr for z, cr in zip(zs, corr)]

    return apply(n, rhss)


def _dn_block(chains):
    n = chains[0][0].shape[0]
    r = lax.broadcasted_iota(jnp.int32, (n, n), 0)
    c = lax.broadcasted_iota(jnp.int32, (n, n), 1)

    def same_block(size):
        return (r // size) == (c // size)

    ms, rhss, pre = [], [], []
    for q, k, v, gcol, grow, beta, state_ref, forward in chains:
        d = (r - c) if forward else (c - r)
        kf = k.astype(F32)
        kb = kf * beta
        decay = jnp.exp(jnp.where(d >= 0, gcol - grow, NEG_BIG))
        ms.append(jnp.where(d > 0, -(_dot_nt(kb.astype(BF16), k) * decay), 0.0))
        intra = (_dot_nt(q, k) * decay).astype(BF16)
        eg = jnp.exp(gcol)
        rhss.append(jnp.concatenate([v.astype(F32) * beta, kb * eg], axis=1))
        g_last = grow[:, n - 1:n] if forward else grow[:, 0:1]
        pre.append((intra, (q.astype(F32) * eg).astype(BF16), (kf * jnp.exp(g_last - gcol)).astype(BF16),
                    jnp.exp(g_last)))
    sols = _unit_triangular_solve(ms, rhss, same_block)
    outs = []
    for chain, sol, (intra, q_dec, k_dec, blk_decay) in zip(chains, sols, pre):
        state_ref = chain[6]
        state = state_ref[...]
        sb = state.astype(BF16)
        v_new = sol[:, :HEAD_DIM] - _dot(sol[:, HEAD_DIM:].astype(BF16), sb)
        vb = v_new.astype(BF16)
        outs.append(_dot(q_dec, sb) + _dot(intra, vb))
        state_ref[...] = state * blk_decay + _dot_tn(k_dec, vb)
    return outs


def _dn_kernel(qf_ref, kf_ref, vf_ref, qb_ref, kb_ref, vb_ref, gcf_ref, gcb_ref, btf_ref, btb_ref,
               grf_ref, grb_ref, of_ref, ob_ref, state_ref):
    hb = qf_ref.shape[1]

    @pl.when(pl.program_id(2) == 0)
    def _():
        state_ref[...] = jnp.zeros_like(state_ref)

    lane = lax.broadcasted_iota(jnp.int32, gcf_ref.shape[1:], 1)

    def pick(ref, idx):
        return jnp.sum(jnp.where(lane == idx, ref[0], 0.0), axis=-1, keepdims=True)

    chains = []
    for hl in range(hb):
        h = pl.program_id(1) * hb + hl
        chains.append((qf_ref[0, hl], kf_ref[0, hl], vf_ref[0, hl], pick(gcf_ref, h),
                       grf_ref[0, pl.ds(h, 1), :], pick(btf_ref, h), state_ref.at[0, hl], True))
        chains.append((qb_ref[0, hl], kb_ref[0, hl], vb_ref[0, hl], pick(gcb_ref, HEADS + h),
                       grb_ref[0, pl.ds(HEADS + h, 1), :], pick(btb_ref, HEADS + h), state_ref.at[1, hl], False))
    outs = _dn_block(chains)
    for hl in range(hb):
        of_ref[0, hl] = outs[2 * hl]
        ob_ref[0, hl] = outs[2 * hl + 1]


def deltanet_scan(q, k, v, gc, beta, gc_rows, tb=DN_BLOCK, hb=DN_HEADS_PER_STEP):
    b, _, s, _ = q.shape
    nb = s // tb
    nd = 2 * HEADS
    fwd = pl.BlockSpec((1, hb, tb, HEAD_DIM), lambda bi, h, c: (bi, h, c, 0))
    bwd = pl.BlockSpec((1, hb, tb, HEAD_DIM), lambda bi, h, c: (bi, h, nb - 1 - c, 0))
    vec_f = pl.BlockSpec((1, tb, nd), lambda bi, h, c: (bi, c, 0))
    vec_b = pl.BlockSpec((1, tb, nd), lambda bi, h, c: (bi, nb - 1 - c, 0))
    row_f = pl.BlockSpec((1, nd, tb), lambda bi, h, c: (bi, 0, c))
    row_b = pl.BlockSpec((1, nd, tb), lambda bi, h, c: (bi, 0, nb - 1 - c))
    os_ = jax.ShapeDtypeStruct((b, HEADS, s, HEAD_DIM), F32)
    return pl.pallas_call(
        _dn_kernel,
        grid=(b, HEADS // hb, nb),
        in_specs=[fwd, fwd, fwd, bwd, bwd, bwd, vec_f, vec_b, vec_f, vec_b, row_f, row_b],
        out_specs=[fwd, bwd],
        out_shape=[os_, os_],
        scratch_shapes=[pltpu.VMEM((2, hb, HEAD_DIM, HEAD_DIM), F32)],
        compiler_params=_params("parallel", "parallel", "arbitrary"),
        name="deltanet_scan",
    )(q, k, v, q, k, v, gc, gc, beta, beta, gc_rows, gc_rows)


def _dn_out_kernel(of_ref, ob_ref, z_ref, nw_ref, o_ref):
    nw = nw_ref[...]
    for h in range(HEADS):
        o = _rms(of_ref[0, h] + ob_ref[0, h], nw)
        z = z_ref[0, :, h * HEAD_DIM:(h + 1) * HEAD_DIM].astype(F32)
        o_ref[0, :, h * HEAD_DIM:(h + 1) * HEAD_DIM] = (o * (z * _sigmoid(z))).astype(o_ref.dtype)


def deltanet_out(o_f, o_b, proj, z_col0, out_norm, ts=ROW_TILE):
    b, _, s, _ = o_f.shape
    head_spec = pl.BlockSpec((1, HEADS, ts, HEAD_DIM), lambda bi, i: (bi, 0, i, 0))
    zb = z_col0 // A_V_W
    return pl.pallas_call(
        _dn_out_kernel,
        grid=(b, s // ts),
        in_specs=[head_spec, head_spec,
                  pl.BlockSpec((1, ts, A_V_W), lambda bi, i: (bi, i, zb)),
                  pl.BlockSpec((1, HEAD_DIM), lambda bi, i: (0, 0))],
        out_specs=pl.BlockSpec((1, ts, A_V_W), lambda bi, i: (bi, i, 0)),
        out_shape=jax.ShapeDtypeStruct((b, s, A_V_W), BF16),
        compiler_params=_params("parallel", "parallel"),
        name="deltanet_out",
    )(o_f, o_b, proj, out_norm.reshape(1, HEAD_DIM))


def _rope_tables(s):
    half = ROPE_DIMS // 2
    inv = ROPE_THETA ** (-jnp.arange(0, ROPE_DIMS, 2, dtype=F32) / ROPE_DIMS)
    ang = jnp.arange(s, dtype=F32)[:, None] * inv[None, :]
    cos, sin = jnp.cos(ang), jnp.sin(ang)
    rest = HEAD_DIM - ROPE_DIMS
    c = jnp.concatenate([cos, cos, jnp.ones((s, rest), F32)], axis=1)
    s1 = jnp.concatenate([jnp.zeros((s, half), F32), sin, jnp.zeros((s, rest), F32)], axis=1)
    s2 = jnp.concatenate([-sin, jnp.zeros((s, half + rest), F32)], axis=1)
    return c, s1, s2


def _trunk(x_a, x_b, p_a, p_b, ab_w_in, ab_conv_w, ab_a_log, ab_dt_bias, ab_out_norm, ab_rpb, ab_w_out,
           c_w_in, c_lambda, c_subln, c_w_out, norms, ffn_w_in, ffn_conv_w, ffn_conv_b,
           ffn_w_out, ple_w_proj, ple_w_gate):
    b_a, s, d = x_a.shape
    b = b_a + x_b.shape[0]
    t, t_a = b * s, b_a * s
    depth = norms.shape[0]
    depth_p = p_a.shape[0]
    h = jnp.concatenate([x_a, x_b], axis=0).reshape(t, d)
    rope = _rope_tables(s)
    o1 = A_QKV_W
    o2 = o1 + A_V_W
    o3 = o2 + 4 * HEADS
    ab_main = _column_tiles(jnp.concatenate([ab_w_in[:, :, :o2], ab_w_in[:, :, o3:]], axis=2), COL_TILE)
    ab_gate = _column_tiles(jnp.pad(ab_w_in[:, :, o2:o3], ((0, 0), (0, 0), (0, LANES - 4 * HEADS))), LANES)
    c_in = _column_tiles(c_w_in, COL_TILE)
    ffn_in = _column_tiles(ffn_w_in, FF_TILE)
    ab_out, c_out, ffn_out = ab_w_out.astype(BF16), c_w_out.astype(BF16), ffn_w_out.astype(BF16)
    ple_gate, ple_proj = ple_w_gate.astype(BF16), ple_w_proj.astype(BF16)
    for layer in range(depth):
        j = layer // 2
        if layer % 2 == 0:
            proj = norm_matmul(h, norms[layer, 0], ab_main, j, BF16).reshape(b, s, -1)
            gates = norm_matmul(h, norms[layer, 0], ab_gate, j, F32).reshape(b, s, LANES)
            qa, ka, va, gc, beta = deltanet_prep(proj, gates, ab_conv_w[j], ab_a_log[j], ab_dt_bias[j])
            o_f, o_b = deltanet_scan(qa, ka, va, gc, beta, jnp.transpose(gc, (0, 2, 1)))
            o_a = deltanet_out(o_f, o_b, proj, o1, ab_out_norm[j])
            o_nb = neighbourhood_attention(proj, o2, _na_bias(ab_rpb[j]))
            h, xn = proj_norm_res([o_a.reshape(t, -1), o_nb.reshape(t, -1)], ab_out, j, norms[layer, 1],
                                  norms[layer, 2], h)
        else:
            proj = norm_matmul(h, norms[layer, 0], c_in, j, BF16, rope=rope,
                               rope_tiles=2 * C_QK_W // COL_TILE, seq_len=s).reshape(b, s, -1)
            o_c = diff_attention(proj, c_lambda[j], c_subln[j], layer)
            h, xn = proj_norm_res([o_c.reshape(t, -1)], c_out, j, norms[layer, 1], norms[layer, 2], h)
        f = conv_ffn(xn, ffn_in, ffn_out, layer, ffn_conv_w[layer], ffn_conv_b[layer], norms[layer, 3], s)
        ple_args = (f, p_a.reshape(depth_p, t_a, -1), p_b.reshape(depth_p, t - t_a, -1), ple_gate, ple_proj, layer)
        if layer + 1 < depth:
            h = ple(h, *ple_args)
    y_a = ple(h, *ple_args, row0=0, rows=t_a)
    y_b = ple(h, *ple_args, row0=t_a, rows=t - t_a)
    return y_a.reshape(b_a, s, d), y_b.reshape(b - b_a, s, d)


def kernel(x_prompt, x_sample, p_prompt, p_sample, ab_w_in, ab_conv_w, ab_a_log, ab_dt_bias, ab_out_norm,
           ab_rpb, ab_w_out, c_w_in, c_lambda, c_subln, c_w_out, norms, ffn_w_in, ffn_conv_w, ffn_conv_b,
           ffn_w_out, ple_w_proj, ple_w_gate):
    return _trunk(x_prompt, x_sample, p_prompt, p_sample, ab_w_in, ab_conv_w, ab_a_log, ab_dt_bias,
                  ab_out_norm, ab_rpb, ab_w_out, c_w_in, c_lambda, c_subln, c_w_out, norms, ffn_w_in,
                  ffn_conv_w, ffn_conv_b, ffn_w_out, ple_w_proj, ple_w_gate)
```

```python
import functools
import math

import jax
import jax.numpy as jnp
import numpy as np
from jax import lax
from jax.experimental import pallas as pl
from jax.experimental.pallas import tpu as pltpu

F32 = jnp.float32
BF16 = jnp.bfloat16

D_MODEL = 2048
PLE_DIM = 256
GRID_W = 64
RMS_EPS = 1e-6
HEADS = 8
HEAD_DIM = 128
A_CONV_W = 5
NA_ROWS = 8
NA_COLS = 16
ROPE_THETA = 500000.0
ROPE_DIMS = HEAD_DIM // 4
D_FF = 8192
FFN_CONV_W = 3
A_QKV_W = 3 * HEADS * HEAD_DIM
A_V_W = HEADS * HEAD_DIM
B_W = HEADS * HEAD_DIM
C_QK_W = HEADS * 2 * HEAD_DIM

LANES = 128
SUBLANES_BF16 = 16
VMEM_LIMIT = 56 * 1024 * 1024
NEG_BIG = -1e30

ROW_TILE = 512
MM_ROW_TILE = 1024
COL_TILE = 1024
FF_TILE = 512
FFN_ROW_TILE = 1024
FFN_ROW_PIECE = 256
DN_BLOCK = 256
DN_BASE = 64
DN_GROUP = 128
DN_HEADS_PER_STEP = 4
ATT_Q_TILE = 1024
ATT_SUB_ROWS = 256
NA_Q_ROWS = 8
NA_HEADS_PER_STEP = 4
NA_SUB_Q_ROWS = 4
NA_SUB_K_ROWS = 12


def _params(*sem):
    return pltpu.CompilerParams(dimension_semantics=sem, vmem_limit_bytes=VMEM_LIMIT)


def _rms(x, w):
    return x * lax.rsqrt(jnp.mean(x * x, axis=-1, keepdims=True) + RMS_EPS) * w


def _dot(a, b):
    return jnp.dot(a, b, preferred_element_type=F32)


def _dot_nt(a, b):
    return lax.dot_general(a, b, (((1,), (1,)), ((), ())), preferred_element_type=F32)


def _dot_tn(a, b):
    return lax.dot_general(a, b, (((0,), (0,)), ((), ())), preferred_element_type=F32)


def _sigmoid(x):
    return 1.0 / (1.0 + jnp.exp(-x))


def _norm_matmul_kernel(*refs, rope_tiles, side):
    x_ref, nw_ref, w_ref = refs[:3]
    rest = list(refs[3:])
    side_w_ref = rest.pop(0) if side else None
    if rope_tiles:
        rc_ref, rs1_ref, rs2_ref = rest[:3]
        rest = rest[3:]
    o_ref = rest.pop(0)
    side_o_ref = rest.pop(0) if side else None
    xn_ref = rest.pop(0)
    j = pl.program_id(1)

    @pl.when(j == 0)
    def _():
        xn = _rms(x_ref[...], nw_ref[...]).astype(BF16)
        xn_ref[...] = xn
        if side:
            side_o_ref[...] = _dot(xn, side_w_ref[0])

    y = _dot(xn_ref[...], w_ref[0, 0])
    if not rope_tiles:
        o_ref[...] = y.astype(o_ref.dtype)
        return

    @pl.when(j < rope_tiles)
    def _():
        n = y.shape[1]
        reps = n // LANES
        c = jnp.concatenate([rc_ref[...]] * reps, axis=1)
        s1 = jnp.concatenate([rs1_ref[...]] * reps, axis=1)
        s2 = jnp.concatenate([rs2_ref[...]] * reps, axis=1)
        half = ROPE_DIMS // 2
        r = y * c + pltpu.roll(y, half, 1) * s1 + pltpu.roll(y, n - half, 1) * s2
        o_ref[...] = r.astype(o_ref.dtype)

    @pl.when(j >= rope_tiles)
    def _():
        o_ref[...] = y.astype(o_ref.dtype)


def _column_tiles(w, tn):
    nl, k, n = w.shape
    return jnp.transpose(w.reshape(nl, k, n // tn, tn), (0, 2, 1, 3)).astype(BF16)


def norm_matmul(x, nw, w_tiles, layer, out_dtype, rope=None, rope_tiles=0, seq_len=None, side_w=None,
                tm=MM_ROW_TILE):
    t, k = x.shape
    _, n_tiles, _, tn = w_tiles.shape
    n = n_tiles * tn
    tm = min(tm, t)
    assert t % tm == 0
    side = side_w is not None
    in_specs = [
        pl.BlockSpec((tm, k), lambda i, j: (i, 0)),
        pl.BlockSpec((1, k), lambda i, j: (0, 0)),
        pl.BlockSpec((1, 1, k, tn), lambda i, j: (layer, j, 0, 0)),
    ]
    args = [x, nw.reshape(1, k), w_tiles]
    out_specs = [pl.BlockSpec((tm, tn), lambda i, j: (i, j))]
    out_shape = [jax.ShapeDtypeStruct((t, n), out_dtype)]
    if side:
        m = side_w.shape[2]
        in_specs.append(pl.BlockSpec((1, k, m), lambda i, j: (layer, 0, 0)))
        args.append(side_w)
        out_specs.append(pl.BlockSpec((tm, m), lambda i, j: (i, 0)))
        out_shape.append(jax.ShapeDtypeStruct((t, m), F32))
    if rope_tiles:
        per_seq = seq_len // tm
        for tab in rope:
            in_specs.append(pl.BlockSpec((tm, LANES), lambda i, j: (i % per_seq, 0)))
            args.append(tab)
    outs = pl.pallas_call(
        functools.partial(_norm_matmul_kernel, rope_tiles=rope_tiles, side=side),
        grid=(t // tm, n // tn),
        in_specs=in_specs,
        out_specs=out_specs,
        out_shape=out_shape,
        scratch_shapes=[pltpu.VMEM((tm, k), BF16)],
        compiler_params=_params("parallel", "arbitrary"),
        name="norm_matmul",
    )(*args)
    return tuple(outs) if side else outs[0]


def _proj_norm_res_kernel(*refs, n_in):
    a_refs = refs[:n_in]
    w_refs = refs[n_in:2 * n_in]
    nw_ref, nw_next_ref, h_ref, o_ref, xn_ref = refs[2 * n_in:]
    y = _dot(a_refs[0][...], w_refs[0][0])
    for a_ref, w_ref in zip(a_refs[1:], w_refs[1:]):
        y = y + _dot(a_ref[...], w_ref[0])
    h_new = h_ref[...] + _rms(y, nw_ref[...])
    o_ref[...] = h_new
    xn_ref[...] = _rms(h_new, nw_next_ref[...]).astype(BF16)


def proj_norm_res(acts, w, layer, nw, nw_next, h, tm=ROW_TILE):
    t, n = h.shape
    n_in = len(acts)
    kw = acts[0].shape[1]
    assert all(a.shape[1] == kw for a in acts) and w.shape[1] == n_in * kw
    in_specs = [pl.BlockSpec((tm, kw), lambda i: (i, 0)) for _ in acts]
    in_specs += [pl.BlockSpec((1, kw, n), functools.partial(lambda piece, i: (layer, piece, 0), piece))
                 for piece in range(n_in)]
    in_specs += [pl.BlockSpec((1, n), lambda i: (0, 0)), pl.BlockSpec((1, n), lambda i: (0, 0)),
                 pl.BlockSpec((tm, n), lambda i: (i, 0))]
    ws = [w] * n_in
    row_spec = pl.BlockSpec((tm, n), lambda i: (i, 0))
    return pl.pallas_call(
        functools.partial(_proj_norm_res_kernel, n_in=n_in),
        grid=(t // tm,),
        in_specs=in_specs,
        out_specs=[row_spec, row_spec],
        out_shape=[jax.ShapeDtypeStruct((t, n), F32), jax.ShapeDtypeStruct((t, n), BF16)],
        compiler_params=_params("parallel"),
        name="proj_norm_res",
    )(*acts, *ws, nw.reshape(1, n), nw_next.reshape(1, n), h)


def _ple_kernel(h_ref, f_ref, pa_ref, pb_ref, wg_ref, wp_ref, o_ref, *, first_block, a_blocks):
    h = h_ref[...] + f_ref[...]
    gate = _sigmoid(_dot(h.astype(BF16), wg_ref[0]))
    p = jnp.where(first_block + pl.program_id(0) < a_blocks, pa_ref[0], pb_ref[0])
    o_ref[...] = h + gate * _dot(p.astype(BF16), wp_ref[0])


def ple(h, f, p_a, p_b, wg, wp, layer, row0=0, rows=None, tm=ROW_TILE):
    n = h.shape[1]
    rows = h.shape[0] if rows is None else rows
    assert row0 % tm == 0 and rows % tm == 0 and p_a.shape[1] % tm == 0 and p_b.shape[1] % tm == 0
    i0 = row0 // tm
    na, nb = p_a.shape[1] // tm, p_b.shape[1] // tm
    return pl.pallas_call(
        functools.partial(_ple_kernel, first_block=i0, a_blocks=na),
        grid=(rows // tm,),
        in_specs=[
            pl.BlockSpec((tm, n), lambda i: (i0 + i, 0)),
            pl.BlockSpec((tm, n), lambda i: (i0 + i, 0)),
            pl.BlockSpec((1, tm, p_a.shape[2]), lambda i: (layer, jnp.minimum(i0 + i, na - 1), 0)),
            pl.BlockSpec((1, tm, p_b.shape[2]), lambda i: (layer, jnp.clip(i0 + i - na, 0, nb - 1), 0)),
            pl.BlockSpec((1,) + wg.shape[1:], lambda i: (layer, 0, 0)),
            pl.BlockSpec((1,) + wp.shape[1:], lambda i: (layer, 0, 0)),
        ],
        out_specs=pl.BlockSpec((tm, n), lambda i: (i, 0)),
        out_shape=jax.ShapeDtypeStruct((rows, n), F32),
        compiler_params=_params("parallel"),
        name="ple",
    )(h, f, p_a, p_b, wg, wp)


def _ffn_kernel(x_ref, xp_ref, xx_ref, wg_ref, wu_ref, cwg_ref, cwu_ref, cbg_ref, cbu_ref,
                wo_ref, nw2_ref, o_ref, xn_ref, hg_ref, hu_ref, *, tm, per_seq):
    i = pl.program_id(0)
    j = pl.program_id(1)

    @pl.when(j == 0)
    def _():
        pos = i % per_seq
        half = SUBLANES_BF16 // 2
        prev = jnp.where(pos == 0, 0.0, xp_ref[...].astype(F32))[half:]
        nxt = jnp.where(pos == per_seq - 1, 0.0, xx_ref[...].astype(F32))[:half]
        xn_ref[0:tm, :] = x_ref[...]
        xn_ref[tm:, :] = jnp.concatenate([nxt, prev], axis=0).astype(BF16)
        o_ref[...] = jnp.zeros_like(o_ref)

    xe = xn_ref[...]

    for w_ref, h_ref in ((wg_ref, hg_ref), (wu_ref, hu_ref)):
        hid = _dot(xe, w_ref[0, 0])
        h_ref[8:, :] = hid
        h_ref[0:8, :] = hid[tm + 8:, :]

    def conv(h_ref, cw_ref, cb_ref, r0, rows):
        cw = cw_ref[...]
        return (h_ref[pl.ds(7 + r0, rows), :] * cw[0:1] + h_ref[pl.ds(8 + r0, rows), :] * cw[1:2]
                + h_ref[pl.ds(9 + r0, rows), :] * cw[2:3] + cb_ref[...])

    rows = min(FFN_ROW_PIECE, tm)
    for r0 in range(0, tm, rows):
        g = conv(hg_ref, cwg_ref, cbg_ref, r0, rows)
        u = conv(hu_ref, cwu_ref, cbu_ref, r0, rows)
        c0 = math.sqrt(2.0 / math.pi)
        gelu = 0.5 * g * (1.0 + jnp.tanh(c0 * (g + 0.044715 * (g * g * g))))
        o_ref[r0:r0 + rows, :] += _dot((gelu * u).astype(BF16), wo_ref[0])

    @pl.when(j == pl.num_programs(1) - 1)
    def _():
        o_ref[...] = _rms(o_ref[...], nw2_ref[...])


def conv_ffn(xn, w_in, w_out, layer, conv_w, conv_b, nw_out, seq_len, tm=FFN_ROW_TILE):
    t, d = xn.shape
    f = w_out.shape[1]
    tf = w_in.shape[3]
    nf = f // tf
    per_seq = seq_len // tm
    rg = tm // SUBLANES_BF16
    last_group = t // SUBLANES_BF16 - 1
    cb = conv_b.reshape(1, 2 * f)
    return pl.pallas_call(
        functools.partial(_ffn_kernel, tm=tm, per_seq=per_seq),
        grid=(t // tm, nf),
        in_specs=[
            pl.BlockSpec((tm, d), lambda i, j: (i, 0)),
            pl.BlockSpec((SUBLANES_BF16, d), lambda i, j: (jnp.maximum(i * rg - 1, 0), 0)),
            pl.BlockSpec((SUBLANES_BF16, d), lambda i, j: (jnp.minimum((i + 1) * rg, last_group), 0)),
            pl.BlockSpec((1, 1, d, tf), lambda i, j: (layer, j, 0, 0)),
            pl.BlockSpec((1, 1, d, tf), lambda i, j: (layer, j + nf, 0, 0)),
            pl.BlockSpec((FFN_CONV_W, tf), lambda i, j: (0, j)),
            pl.BlockSpec((FFN_CONV_W, tf), lambda i, j: (0, j + nf)),
            pl.BlockSpec((1, tf), lambda i, j: (0, j)),
            pl.BlockSpec((1, tf), lambda i, j: (0, j + nf)),
            pl.BlockSpec((1, tf, d), lambda i, j: (layer, j, 0)),
            pl.BlockSpec((1, d), lambda i, j: (0, 0)),
        ],
        out_specs=pl.BlockSpec((tm, d), lambda i, j: (i, 0)),
        out_shape=jax.ShapeDtypeStruct((t, d), F32),
        scratch_shapes=[
            pltpu.VMEM((tm + SUBLANES_BF16, d), BF16),
            pltpu.VMEM((8 + tm + SUBLANES_BF16, tf), F32),
            pltpu.VMEM((8 + tm + SUBLANES_BF16, tf), F32),
        ],
        compiler_params=_params("parallel", "arbitrary"),
        name="conv_ffn",
    )(xn, xn, xn, w_in, w_in, conv_w, conv_w, cb, cb, w_out, nw_out.reshape(1, d))


def _diff_attn_kernel(lp_ref, sub_ref, q_ref, k_ref, v_ref, o_ref, *, lambda_init):
    lp = lp_ref[...]
    lam = (jnp.exp(jnp.sum(lp[0:1] * lp[1:2], axis=-1, keepdims=True))
           - jnp.exp(jnp.sum(lp[2:3] * lp[3:4], axis=-1, keepdims=True)) + lambda_init)
    sub = sub_ref[...]
    n_sub = q_ref.shape[1] // ATT_SUB_ROWS

    def scores(i):
        q = q_ref[0, i * ATT_SUB_ROWS:(i + 1) * ATT_SUB_ROWS, :].astype(F32)
        q = (q * (HEAD_DIM ** -0.5 * math.log2(math.e))).astype(BF16)
        return [_dot_nt(q[:, lo:lo + HEAD_DIM], k_ref[0, :, lo:lo + HEAD_DIM]) for lo in (0, HEAD_DIM)]

    def weights(s12):
        es = [jnp.exp2(s - jnp.max(s, axis=-1, keepdims=True)) for s in s12]
        l1, l2 = [jnp.sum(e, axis=-1, keepdims=True) for e in es]
        return (es[0] - es[1] * (lam * l1 / l2)).astype(BF16), 1.0 / l1

    def finish(i, a_inv):
        a, inv_l1 = a_inv
        o = _dot(a, v_ref[0]) * inv_l1
        o_ref[0, i * ATT_SUB_ROWS:(i + 1) * ATT_SUB_ROWS, :] = (_rms(o, sub) * (1.0 - lambda_init)).astype(o_ref.dtype)

    s_next = scores(0)
    a_prev = None
    for i in range(n_sub):
        s_cur = s_next
        if i + 1 < n_sub:
            s_next = scores(i + 1)
        a_cur = weights(s_cur)
        if a_prev is not None:
            finish(i - 1, a_prev)
        a_prev = a_cur
    finish(n_sub - 1, a_prev)


def diff_attention(proj, lam_params, subln, layer, tq=ATT_Q_TILE):
    b, s, _ = proj.shape
    w = 2 * HEAD_DIM
    tq = min(tq, s)
    lambda_init = 0.8 - 0.6 * math.exp(-0.3 * layer)
    return pl.pallas_call(
        functools.partial(_diff_attn_kernel, lambda_init=lambda_init),
        grid=(b, HEADS, s // tq),
        in_specs=[
            pl.BlockSpec((4, HEAD_DIM), lambda bi, h, qi: (0, 0)),
            pl.BlockSpec((1, w), lambda bi, h, qi: (0, 0)),
            pl.BlockSpec((1, tq, w), lambda bi, h, qi: (bi, qi, h)),
            pl.BlockSpec((1, s, w), lambda bi, h, qi: (bi, 0, HEADS + h)),
            pl.BlockSpec((1, s, w), lambda bi, h, qi: (bi, 0, 2 * HEADS + h)),
        ],
        out_specs=pl.BlockSpec((1, tq, w), lambda bi, h, qi: (bi, qi, h)),
        out_shape=jax.ShapeDtypeStruct((b, s, C_QK_W), BF16),
        compiler_params=_params("parallel", "parallel", "arbitrary"),
        name="diff_attention",
    )(lam_params, subln.reshape(1, w), proj, proj, proj)


def _na_key_start_row(first_query_row):
    return (first_query_row - NA_ROWS // 2, 0, GRID_W - NA_SUB_K_ROWS)


def _na_bias(rpb):
    rows = GRID_W
    n_sub = NA_Q_ROWS // NA_SUB_Q_ROWS
    n_dr = 2 * NA_ROWS - 1
    c = np.arange(GRID_W)
    col_start = np.clip(c - NA_COLS // 2, 0, GRID_W - NA_COLS)
    col_ok = (c[None, :] >= col_start[:, None]) & (c[None, :] < col_start[:, None] + NA_COLS)
    dc = np.clip(c[None, :] - c[:, None] + NA_COLS - 1, 0, 2 * NA_COLS - 2)
    sel_c = (np.arange(2 * NA_COLS - 1)[:, None, None] == dc[None]).astype(np.float32)
    tables = jnp.einsum("hab,bxy->haxy", rpb, sel_c, precision=lax.Precision.HIGHEST) * math.log2(math.e)
    tables = jnp.where(col_ok[None, None], tables, -jnp.inf)
    masked = jnp.full(tables.shape[:1] + tables.shape[2:], -jnp.inf, tables.dtype)
    blocks = []
    for rb in (0, 1, rows // NA_Q_ROWS - 1):
        for p in range(n_sub):
            r0 = rb * NA_Q_ROWS + p * NA_SUB_Q_ROWS
            base = int(np.clip(*_na_key_start_row(r0)))
            q_rows = []
            for ql in range(NA_SUB_Q_ROWS):
                r = r0 + ql
                row_start = int(np.clip(r - NA_ROWS // 2, 0, rows - NA_ROWS))
                assert base <= row_start and row_start + NA_ROWS <= base + NA_SUB_K_ROWS
                tiles = []
                for kl in range(NA_SUB_K_ROWS):
                    kr = base + kl
                    inside = row_start <= kr < row_start + NA_ROWS
                    assert not inside or 0 <= kr - r + NA_ROWS - 1 < n_dr
                    tiles.append(tables[:, kr - r + NA_ROWS - 1] if inside else masked)
                q_rows.append(jnp.concatenate(tiles, axis=-1))
            blocks.append(jnp.concatenate(q_rows, axis=-2))
    bias = jnp.stack(blocks, axis=1)
    return bias.reshape(rpb.shape[0], 3, n_sub, NA_SUB_Q_ROWS * GRID_W, NA_SUB_K_ROWS * GRID_W)


def _na_kernel(q_ref, k_ref, v_ref, b_ref, o_ref):
    rb = pl.program_id(2)
    nq = NA_SUB_Q_ROWS * GRID_W
    nk = NA_SUB_K_ROWS * GRID_W
    n_sub = NA_Q_ROWS // NA_SUB_Q_ROWS

    def key_start(p):
        row, lo, hi = _na_key_start_row(rb * NA_Q_ROWS + p * NA_SUB_Q_ROWS)
        return pl.multiple_of(jnp.clip(row, lo, hi) * GRID_W, NA_SUB_Q_ROWS * GRID_W)

    def scores(item):
        hl, p = item
        lanes = slice(hl * HEAD_DIM, (hl + 1) * HEAD_DIM)
        q = (q_ref[0, p * nq:(p + 1) * nq, lanes].astype(F32) * (HEAD_DIM ** -0.5 * math.log2(math.e))).astype(BF16)
        return _dot_nt(q, k_ref[0, pl.ds(key_start(p), nk), lanes]) + b_ref[hl, 0, p]

    def weights(s):
        e = jnp.exp2(s - jnp.max(s, axis=-1, keepdims=True))
        return e.astype(BF16), 1.0 / jnp.sum(e, axis=-1, keepdims=True)

    def finish(item, e_inv):
        hl, p = item
        lanes = slice(hl * HEAD_DIM, (hl + 1) * HEAD_DIM)
        e, inv_l = e_inv
        o = _dot(e, v_ref[0, pl.ds(key_start(p), nk), lanes]) * inv_l
        o_ref[0, p * nq:(p + 1) * nq, lanes] = o.astype(o_ref.dtype)

    items = [(hl, p) for hl in range(NA_HEADS_PER_STEP) for p in range(n_sub)]
    s_next = scores(items[0])
    e_prev = None
    for n, item in enumerate(items):
        s_cur = s_next
        if n + 1 < len(items):
            s_next = scores(items[n + 1])
        e_cur = weights(s_cur)
        if e_prev is not None:
            finish(items[n - 1], e_prev)
        e_prev = e_cur
    finish(items[-1], e_prev)


def neighbourhood_attention(proj, col0, bias):
    b, s, _ = proj.shape
    nq = NA_Q_ROWS * GRID_W
    nblk = s // nq
    hs = NA_HEADS_PER_STEP
    w = hs * HEAD_DIM
    c0 = col0 // w
    groups = HEADS // hs

    def bias_map(bi, h, rb):
        return (h, jnp.where(rb == 0, 0, jnp.where(rb == nblk - 1, 2, 1)), 0, 0, 0)

    return pl.pallas_call(
        _na_kernel,
        grid=(b, groups, nblk),
        in_specs=[
            pl.BlockSpec((1, nq, w), lambda bi, h, rb: (bi, rb, c0 + h)),
            pl.BlockSpec((1, s, w), lambda bi, h, rb: (bi, 0, c0 + groups + h)),
            pl.BlockSpec((1, s, w), lambda bi, h, rb: (bi, 0, c0 + 2 * groups + h)),
            pl.BlockSpec((hs, 1) + bias.shape[2:], bias_map),
        ],
        out_specs=pl.BlockSpec((1, nq, w), lambda bi, h, rb: (bi, rb, h)),
        out_shape=jax.ShapeDtypeStruct((b, s, B_W), BF16),
        compiler_params=_params("parallel", "parallel", "arbitrary"),
        name="neighbourhood_attention",
    )(proj, proj, proj, bias)


HALO = SUBLANES_BF16


def _split3(x):
    hi = x.astype(BF16)
    r1 = x - hi.astype(F32)
    mid = r1.astype(BF16)
    lo = (r1 - mid.astype(F32)).astype(BF16)
    return hi, mid, lo


def _dn_prep_kernel(x_ref, xp_ref, xx_ref, cw_ref, gt_ref, alog_ref, dtb_ref,
                    q_ref, k_ref, v_ref, gc_ref, beta_ref, xs_ref):
    i = pl.program_id(1)
    tb = x_ref.shape[1]
    pad = A_CONV_W // 2
    xs_ref[0:HALO, :] = jnp.where(i == 0, 0.0, xp_ref[0].astype(F32))
    xs_ref[HALO:HALO + tb, :] = x_ref[0].astype(F32)
    xs_ref[HALO + tb:, :] = jnp.where(i == pl.num_programs(1) - 1, 0.0, xx_ref[0].astype(F32))
    cw = cw_ref[...]
    y = xs_ref[pl.ds(HALO - pad, tb), :] * cw[0:1]
    for t in range(1, A_CONV_W):
        y = y + xs_ref[pl.ds(HALO - pad + t, tb), :] * cw[t:t + 1]
    y = y * _sigmoid(y)
    for h in range(HEADS):
        qh = y[:, h * HEAD_DIM:(h + 1) * HEAD_DIM]
        kh = y[:, (HEADS + h) * HEAD_DIM:(HEADS + h + 1) * HEAD_DIM]
        qn = qh * (lax.rsqrt(jnp.sum(qh * qh, axis=-1, keepdims=True) + 1e-6) * (HEAD_DIM ** -0.5))
        kn = kh * lax.rsqrt(jnp.sum(kh * kh, axis=-1, keepdims=True) + 1e-6)
        q_ref[0, h] = qn.astype(q_ref.dtype)
        k_ref[0, h] = kn.astype(k_ref.dtype)
        v_ref[0, h] = y[:, (2 * HEADS + h) * HEAD_DIM:(2 * HEADS + h + 1) * HEAD_DIM].astype(v_ref.dtype)

    nd = 2 * HEADS
    gates = gt_ref[0]
    z = gates[:, 0:nd] + dtb_ref[...]
    softplus = jnp.maximum(z, 0.0) + jnp.log(1.0 + jnp.exp(-jnp.abs(z)))
    g = -jnp.exp(alog_ref[...]) * softplus
    beta_ref[0] = _sigmoid(gates[:, nd:2 * nd])
    r = lax.broadcasted_iota(jnp.int32, (tb, tb), 0)
    c = lax.broadcasted_iota(jnp.int32, (tb, tb), 1)
    lower = jnp.where(r >= c, 1.0, 0.0).astype(BF16)
    upper = jnp.where(r <= c, 1.0, 0.0).astype(BF16)
    parts = _split3(g)
    fwd = _dot(lower, parts[0]) + _dot(lower, parts[1]) + _dot(lower, parts[2])
    bwd = _dot(upper, parts[0]) + _dot(upper, parts[1]) + _dot(upper, parts[2])
    col = lax.broadcasted_iota(jnp.int32, (tb, nd), 1)
    gc_ref[0] = jnp.where(col < HEADS, fwd, bwd)


def deltanet_prep(proj, gates, conv_w, a_log, dt_bias, tb=DN_BLOCK):
    b, s, _ = proj.shape
    nb = s // tb
    rh = tb // HALO
    nd = 2 * HEADS
    hs = jax.ShapeDtypeStruct((b, HEADS, s, HEAD_DIM), BF16)
    head_spec = pl.BlockSpec((1, HEADS, tb, HEAD_DIM), lambda bi, i: (bi, 0, i, 0))
    vec_spec = pl.BlockSpec((1, tb, nd), lambda bi, i: (bi, i, 0))
    return pl.pallas_call(
        _dn_prep_kernel,
        grid=(b, nb),
        in_specs=[
            pl.BlockSpec((1, tb, A_QKV_W), lambda bi, i: (bi, i, 0)),
            pl.BlockSpec((1, HALO, A_QKV_W), lambda bi, i: (bi, jnp.maximum(i * rh - 1, 0), 0)),
            pl.BlockSpec((1, HALO, A_QKV_W), lambda bi, i: (bi, jnp.minimum((i + 1) * rh, s // HALO - 1), 0)),
            pl.BlockSpec((A_CONV_W, A_QKV_W), lambda bi, i: (0, 0)),
            pl.BlockSpec((1, tb, LANES), lambda bi, i: (bi, i, 0)),
            pl.BlockSpec((1, nd), lambda bi, i: (0, 0)),
            pl.BlockSpec((1, nd), lambda bi, i: (0, 0)),
        ],
        out_specs=[head_spec, head_spec, head_spec, vec_spec, vec_spec],
        out_shape=[hs, hs, hs, jax.ShapeDtypeStruct((b, s, nd), F32), jax.ShapeDtypeStruct((b, s, nd), F32)],
        scratch_shapes=[pltpu.VMEM((tb + 2 * HALO, A_QKV_W), F32)],
        compiler_params=_params("parallel", "arbitrary"),
        name="deltanet_prep",
    )(proj, proj, proj, conv_w, gates, a_log.reshape(1, nd), dt_bias.reshape(1, nd))


def _split2(x):
    hi = x.astype(BF16)
    return hi, (x - hi.astype(F32)).astype(BF16)


def _unit_triangular_solve(ms, rhss, same_block):
    n = ms[0].shape[0]
    grp = DN_GROUP
    ng = n // grp
    r = lax.broadcasted_iota(jnp.int32, (grp, grp), 0)
    c = lax.broadcasted_iota(jnp.int32, (grp, grp), 1)
    base_mask = (r // DN_BASE) == (c // DN_BASE)
    row = lax.broadcasted_iota(jnp.int32, (DN_BASE, grp), 0)
    lane = lax.broadcasted_iota(jnp.int32, (DN_BASE, grp), 1)
    left = lane < DN_BASE

    def pair_lhs(hi, lo):
        hi_f, lo_f = hi.astype(F32), lo.astype(F32)
        swapped = pltpu.roll(hi_f, DN_BASE, 1)
        top = jnp.concatenate([jnp.where(left, hi_f, swapped), jnp.where(left, lo_f, 0.0)], axis=1)
        bottom = jnp.concatenate([jnp.where(left, swapped, hi_f), jnp.where(left, 0.0, lo_f)], axis=1)
        return jnp.concatenate([top, bottom], axis=0).astype(BF16)

    def pair_product(lhs, rhs_hi, rhs_lo):
        res = _dot(lhs, jnp.concatenate([rhs_hi, rhs_lo, rhs_hi, rhs_hi], axis=0))
        return jnp.where(left, res[:DN_BASE], res[DN_BASE:])

    pieces = [jnp.where(base_mask, m[g * grp:(g + 1) * grp, g * grp:(g + 1) * grp], 0.0)
              for m in ms for g in range(ng)]
    pws = [piece[:DN_BASE] + piece[DN_BASE:] for piece in pieces]
    ts = [jnp.where(lane % DN_BASE == row, 1.0, 0.0) + pw for pw in pws]
    splits = [_split2(pw) for pw in pws]
    lhss = [pair_lhs(hi, lo) for hi, lo in splits]
    size = 2
    while size < DN_BASE:
        pws = [pair_product(lhs, hi, lo) for lhs, (hi, lo) in zip(lhss, splits)]
        splits = [_split2(pw) for pw in pws]
        lhss = [pair_lhs(hi, lo) for hi, lo in splits]
        ts = [t + pair_product(lhs, *_split2(t)) for lhs, t in zip(lhss, ts)]
        size *= 2
    t_bases = [jnp.concatenate([jnp.where(left, t, 0.0), jnp.where(left, 0.0, t)], axis=0).astype(BF16)
               for t in ts]
    offs = {}
    blk = 2 * DN_BASE
    while blk <= n:
        offs[blk] = [jnp.where(same_block(blk), jnp.where(same_block(blk // 2), 0.0, m), 0.0).astype(BF16)
                     for m in ms]
        blk *= 2

    def apply(blk, ys):
        if blk == DN_BASE:
            ybs = [y.astype(BF16) for y in ys]
            return [jnp.concatenate([_dot(t_bases[i * ng + g], yb[g * grp:(g + 1) * grp]) for g in range(ng)],
                                    axis=0) for i, yb in enumerate(ybs)]
        zs = apply(blk // 2, ys)
        corr = apply(blk // 2, [_dot(off, z.astype(BF16)) for off, z in zip(offs[blk], zs)])
        return [z + cr for z, cr in zip(zs, corr)]

    return apply(n, rhss)


def _dn_block(chains):
    n = chains[0][0].shape[0]
    r = lax.broadcasted_iota(jnp.int32, (n, n), 0)
    c = lax.broadcasted_iota(jnp.int32, (n, n), 1)

    def same_block(size):
        return (r // size) == (c // size)

    ms, rhss, pre = [], [], []
    for q, k, v, gcol, grow, beta, state_ref, forward in chains:
        d = (r - c) if forward else (c - r)
        kf = k.astype(F32)
        kb = kf * beta
        decay = jnp.exp(jnp.where(d >= 0, gcol - grow, NEG_BIG))
        ms.append(jnp.where(d > 0, -(_dot_nt(kb.astype(BF16), k) * decay), 0.0))
        intra = (_dot_nt(q, k) * decay).astype(BF16)
        eg = jnp.exp(gcol)
        rhss.append(jnp.concatenate([v.astype(F32) * beta, kb * eg], axis=1))
        g_last = grow[:, n - 1:n] if forward else grow[:, 0:1]
        pre.append((intra, (q.astype(F32) * eg).astype(BF16), (kf * jnp.exp(g_last - gcol)).astype(BF16),
                    jnp.exp(g_last)))
    sols = _unit_triangular_solve(ms, rhss, same_block)
    outs = []
    for chain, sol, (intra, q_dec, k_dec, blk_decay) in zip(chains, sols, pre):
        state_ref = chain[6]
        state = state_ref[...]
        sb = state.astype(BF16)
        v_new = sol[:, :HEAD_DIM] - _dot(sol[:, HEAD_DIM:].astype(BF16), sb)
        vb = v_new.astype(BF16)
        outs.append(_dot(q_dec, sb) + _dot(intra, vb))
        state_ref[...] = state * blk_decay + _dot_tn(k_dec, vb)
    return outs


def _dn_kernel(qf_ref, kf_ref, vf_ref, qb_ref, kb_ref, vb_ref, gcf_ref, gcb_ref, btf_ref, btb_ref,
               grf_ref, grb_ref, of_ref, ob_ref, state_ref):
    hb = qf_ref.shape[1]

    @pl.when(pl.program_id(2) == 0)
    def _():
        state_ref[...] = jnp.zeros_like(state_ref)

    lane = lax.broadcasted_iota(jnp.int32, gcf_ref.shape[1:], 1)

    def pick(ref, idx):
        return jnp.sum(jnp.where(lane == idx, ref[0], 0.0), axis=-1, keepdims=True)

    chains = []
    for hl in range(hb):
        h = pl.program_id(1) * hb + hl
        chains.append((qf_ref[0, hl], kf_ref[0, hl], vf_ref[0, hl], pick(gcf_ref, h),
                       grf_ref[0, pl.ds(h, 1), :], pick(btf_ref, h), state_ref.at[0, hl], True))
        chains.append((qb_ref[0, hl], kb_ref[0, hl], vb_ref[0, hl], pick(gcb_ref, HEADS + h),
                       grb_ref[0, pl.ds(HEADS + h, 1), :], pick(btb_ref, HEADS + h), state_ref.at[1, hl], False))
    outs = _dn_block(chains)
    for hl in range(hb):
        of_ref[0, hl] = outs[2 * hl]
        ob_ref[0, hl] = outs[2 * hl + 1]


def deltanet_scan(q, k, v, gc, beta, gc_rows, tb=DN_BLOCK, hb=DN_HEADS_PER_STEP):
    b, _, s, _ = q.shape
    nb = s // tb
    nd = 2 * HEADS
    fwd = pl.BlockSpec((1, hb, tb, HEAD_DIM), lambda bi, h, c: (bi, h, c, 0))
    bwd = pl.BlockSpec((1, hb, tb, HEAD_DIM), lambda bi, h, c: (bi, h, nb - 1 - c, 0))
    vec_f = pl.BlockSpec((1, tb, nd), lambda bi, h, c: (bi, c, 0))
    vec_b = pl.BlockSpec((1, tb, nd), lambda bi, h, c: (bi, nb - 1 - c, 0))
    row_f = pl.BlockSpec((1, nd, tb), lambda bi, h, c: (bi, 0, c))
    row_b = pl.BlockSpec((1, nd, tb), lambda bi, h, c: (bi, 0, nb - 1 - c))
    os_ = jax.ShapeDtypeStruct((b, HEADS, s, HEAD_DIM), F32)
    return pl.pallas_call(
        _dn_kernel,
        grid=(b, HEADS // hb, nb),
        in_specs=[fwd, fwd, fwd, bwd, bwd, bwd, vec_f, vec_b, vec_f, vec_b, row_f, row_b],
        out_specs=[fwd, bwd],
        out_shape=[os_, os_],
        scratch_shapes=[pltpu.VMEM((2, hb, HEAD_DIM, HEAD_DIM), F32)],
        compiler_params=_params("parallel", "parallel", "arbitrary"),
        name="deltanet_scan",
    )(q, k, v, q, k, v, gc, gc, beta, beta, gc_rows, gc_rows)


def _dn_out_kernel(of_ref, ob_ref, z_ref, nw_ref, o_ref):
    nw = nw_ref[...]
    for h in range(HEADS):
        o = _rms(of_ref[0, h] + ob_ref[0, h], nw)
        z = z_ref[0, :, h * HEAD_DIM:(h + 1) * HEAD_DIM].astype(F32)
        o_ref[0, :, h * HEAD_DIM:(h + 1) * HEAD_DIM] = (o * (z * _sigmoid(z))).astype(o_ref.dtype)


def deltanet_out(o_f, o_b, proj, z_col0, out_norm, ts=ROW_TILE):
    b, _, s, _ = o_f.shape
    head_spec = pl.BlockSpec((1, HEADS, ts, HEAD_DIM), lambda bi, i: (bi, 0, i, 0))
    zb = z_col0 // A_V_W
    return pl.pallas_call(
        _dn_out_kernel,
        grid=(b, s // ts),
        in_specs=[head_spec, head_spec,
                  pl.BlockSpec((1, ts, A_V_W), lambda bi, i: (bi, i, zb)),
                  pl.BlockSpec((1, HEAD_DIM), lambda bi, i: (0, 0))],
        out_specs=pl.BlockSpec((1, ts, A_V_W), lambda bi, i: (bi, i, 0)),
        out_shape=jax.ShapeDtypeStruct((b, s, A_V_W), BF16),
        compiler_params=_params("parallel", "parallel"),
        name="deltanet_out",
    )(o_f, o_b, proj, out_norm.reshape(1, HEAD_DIM))


def _rope_tables(s):
    half = ROPE_DIMS // 2
    inv = ROPE_THETA ** (-jnp.arange(0, ROPE_DIMS, 2, dtype=F32) / ROPE_DIMS)
    ang = jnp.arange(s, dtype=F32)[:, None] * inv[None, :]
    cos, sin = jnp.cos(ang), jnp.sin(ang)
    rest = HEAD_DIM - ROPE_DIMS
    c = jnp.concatenate([cos, cos, jnp.ones((s, rest), F32)], axis=1)
    s1 = jnp.concatenate([jnp.zeros((s, half), F32), sin, jnp.zeros((s, rest), F32)], axis=1)
    s2 = jnp.concatenate([-sin, jnp.zeros((s, half + rest), F32)], axis=1)
    return c, s1, s2


def _trunk(x_a, x_b, p_a, p_b, ab_w_in, ab_conv_w, ab_a_log, ab_dt_bias, ab_out_norm, ab_rpb, ab_w_out,
           c_w_in, c_lambda, c_subln, c_w_out, norms, ffn_w_in, ffn_conv_w, ffn_conv_b,
           ffn_w_out, ple_w_proj, ple_w_gate):
    b_a, s, d = x_a.shape
    b = b_a + x_b.shape[0]
    t, t_a = b * s, b_a * s
    depth = norms.shape[0]
    depth_p = p_a.shape[0]
    h = jnp.concatenate([x_a, x_b], axis=0).reshape(t, d)
    rope = _rope_tables(s)
    o1 = A_QKV_W
    o2 = o1 + A_V_W
    o3 = o2 + 4 * HEADS
    ab_main = _column_tiles(jnp.concatenate([ab_w_in[:, :, :o2], ab_w_in[:, :, o3:]], axis=2), COL_TILE)
    ab_gate = jnp.pad(ab_w_in[:, :, o2:o3], ((0, 0), (0, 0), (0, LANES - 4 * HEADS))).astype(BF16)
    c_in = _column_tiles(c_w_in, COL_TILE)
    ffn_in = _column_tiles(ffn_w_in, FF_TILE)
    ab_out, c_out, ffn_out = ab_w_out.astype(BF16), c_w_out.astype(BF16), ffn_w_out.astype(BF16)
    ple_gate, ple_proj = ple_w_gate.astype(BF16), ple_w_proj.astype(BF16)
    for layer in range(depth):
        j = layer // 2
        if layer % 2 == 0:
            proj, gates = norm_matmul(h, norms[layer, 0], ab_main, j, BF16, side_w=ab_gate)
            proj, gates = proj.reshape(b, s, -1), gates.reshape(b, s, LANES)
            qa, ka, va, gc, beta = deltanet_prep(proj, gates, ab_conv_w[j], ab_a_log[j], ab_dt_bias[j])
            o_f, o_b = deltanet_scan(qa, ka, va, gc, beta, jnp.transpose(gc, (0, 2, 1)))
            o_a = deltanet_out(o_f, o_b, proj, o1, ab_out_norm[j])
            o_nb = neighbourhood_attention(proj, o2, _na_bias(ab_rpb[j]))
            h, xn = proj_norm_res([o_a.reshape(t, -1), o_nb.reshape(t, -1)], ab_out, j, norms[layer, 1],
                                  norms[layer, 2], h)
        else:
            proj = norm_matmul(h, norms[layer, 0], c_in, j, BF16, rope=rope,
                               rope_tiles=2 * C_QK_W // COL_TILE, seq_len=s).reshape(b, s, -1)
            o_c = diff_attention(proj, c_lambda[j], c_subln[j], layer)
            h, xn = proj_norm_res([o_c.reshape(t, -1)], c_out, j, norms[layer, 1], norms[layer, 2], h)
        f = conv_ffn(xn, ffn_in, ffn_out, layer, ffn_conv_w[layer], ffn_conv_b[layer], norms[layer, 3], s)
        ple_args = (f, p_a.reshape(depth_p, t_a, -1), p_b.reshape(depth_p, t - t_a, -1), ple_gate, ple_proj, layer)
        if layer + 1 < depth:
            h = ple(h, *ple_args)
    y_a = ple(h, *ple_args, row0=0, rows=t_a)
    y_b = ple(h, *ple_args, row0=t_a, rows=t - t_a)
    return y_a.reshape(b_a, s, d), y_b.reshape(b - b_a, s, d)


def kernel(x_prompt, x_sample, p_prompt, p_sample, ab_w_in, ab_conv_w, ab_a_log, ab_dt_bias, ab_out_norm,
           ab_rpb, ab_w_out, c_w_in, c_lambda, c_subln, c_w_out, norms, ffn_w_in, ffn_conv_w, ffn_conv_b,
           ffn_w_out, ple_w_proj, ple_w_gate):
    return _trunk(x_prompt, x_sample, p_prompt, p_sample, ab_w_in, ab_conv_w, ab_a_log, ab_dt_bias,
                  ab_out_norm, ab_rpb, ab_w_out, c_w_in, c_lambda, c_subln, c_w_out, norms, ffn_w_in,
                  ffn_conv_w, ffn_conv_b, ffn_w_out, ple_w_proj, ple_w_gate)
```

```python
import functools
import math

import jax
import jax.numpy as jnp
import numpy as np
from jax import lax
from jax.experimental import pallas as pl
from jax.experimental.pallas import tpu as pltpu

F32 = jnp.float32
BF16 = jnp.bfloat16

D_MODEL = 2048
PLE_DIM = 256
GRID_W = 64
RMS_EPS = 1e-6
HEADS = 8
HEAD_DIM = 128
A_CONV_W = 5
NA_ROWS = 8
NA_COLS = 16
ROPE_THETA = 500000.0
ROPE_DIMS = HEAD_DIM // 4
D_FF = 8192
FFN_CONV_W = 3
A_QKV_W = 3 * HEADS * HEAD_DIM
A_V_W = HEADS * HEAD_DIM
B_W = HEADS * HEAD_DIM
C_QK_W = HEADS * 2 * HEAD_DIM

LANES = 128
SUBLANES_BF16 = 16
VMEM_LIMIT = 56 * 1024 * 1024
NEG_BIG = -1e30

ROW_TILE = 512
MM_ROW_TILE = 1024
COL_TILE = 1024
FF_TILE = 512
FFN_ROW_TILE = 1024
FFN_ROW_PIECE = 256
DN_BLOCK = 256
DN_BASE = 64
DN_GROUP = 128
DN_HEADS_PER_STEP = 4
ATT_Q_TILE = 1024
ATT_SUB_ROWS = 256
NA_Q_ROWS = 8
NA_HEADS_PER_STEP = 4
NA_SUB_Q_ROWS = 4
NA_SUB_K_ROWS = 12


def _params(*sem):
    return pltpu.CompilerParams(dimension_semantics=sem, vmem_limit_bytes=VMEM_LIMIT)


def _rms(x, w):
    return x * lax.rsqrt(jnp.mean(x * x, axis=-1, keepdims=True) + RMS_EPS) * w


def _dot(a, b):
    return jnp.dot(a, b, preferred_element_type=F32)


def _dot_nt(a, b):
    return lax.dot_general(a, b, (((1,), (1,)), ((), ())), preferred_element_type=F32)


def _dot_tn(a, b):
    return lax.dot_general(a, b, (((0,), (0,)), ((), ())), preferred_element_type=F32)


def _sigmoid(x):
    return 1.0 / (1.0 + jnp.exp(-x))


def _norm_matmul_kernel(*refs, rope_tiles, side):
    x_ref, nw_ref, w_ref = refs[:3]
    rest = list(refs[3:])
    side_w_ref = rest.pop(0) if side else None
    if rope_tiles:
        rc_ref, rs1_ref, rs2_ref = rest[:3]
        rest = rest[3:]
    o_ref = rest.pop(0)
    side_o_ref = rest.pop(0) if side else None
    xn_ref = rest.pop(0)
    j = pl.program_id(1)

    @pl.when(j == 0)
    def _():
        xn = _rms(x_ref[...], nw_ref[...]).astype(BF16)
        xn_ref[...] = xn
        if side:
            side_o_ref[...] = _dot(xn, side_w_ref[0])

    y = _dot(xn_ref[...], w_ref[0, 0])
    if not rope_tiles:
        o_ref[...] = y.astype(o_ref.dtype)
        return

    @pl.when(j < rope_tiles)
    def _():
        n = y.shape[1]
        reps = n // LANES
        c = jnp.concatenate([rc_ref[...]] * reps, axis=1)
        s1 = jnp.concatenate([rs1_ref[...]] * reps, axis=1)
        s2 = jnp.concatenate([rs2_ref[...]] * reps, axis=1)
        half = ROPE_DIMS // 2
        r = y * c + pltpu.roll(y, half, 1) * s1 + pltpu.roll(y, n - half, 1) * s2
        o_ref[...] = r.astype(o_ref.dtype)

    @pl.when(j >= rope_tiles)
    def _():
        o_ref[...] = y.astype(o_ref.dtype)


def _column_tiles(w, tn):
    nl, k, n = w.shape
    return jnp.transpose(w.reshape(nl, k, n // tn, tn), (0, 2, 1, 3)).astype(BF16)


def norm_matmul(x, nw, w_tiles, layer, out_dtype, rope=None, rope_tiles=0, seq_len=None, side_w=None,
                tm=MM_ROW_TILE):
    t, k = x.shape
    _, n_tiles, _, tn = w_tiles.shape
    n = n_tiles * tn
    tm = min(tm, t)
    assert t % tm == 0
    side = side_w is not None
    in_specs = [
        pl.BlockSpec((tm, k), lambda i, j: (i, 0)),
        pl.BlockSpec((1, k), lambda i, j: (0, 0)),
        pl.BlockSpec((1, 1, k, tn), lambda i, j: (layer, j, 0, 0)),
    ]
    args = [x, nw.reshape(1, k), w_tiles]
    out_specs = [pl.BlockSpec((tm, tn), lambda i, j: (i, j))]
    out_shape = [jax.ShapeDtypeStruct((t, n), out_dtype)]
    if side:
        m = side_w.shape[2]
        in_specs.append(pl.BlockSpec((1, k, m), lambda i, j: (layer, 0, 0)))
        args.append(side_w)
        out_specs.append(pl.BlockSpec((tm, m), lambda i, j: (i, 0)))
        out_shape.append(jax.ShapeDtypeStruct((t, m), F32))
    if rope_tiles:
        per_seq = seq_len // tm
        for tab in rope:
            in_specs.append(pl.BlockSpec((tm, LANES), lambda i, j: (i % per_seq, 0)))
            args.append(tab)
    outs = pl.pallas_call(
        functools.partial(_norm_matmul_kernel, rope_tiles=rope_tiles, side=side),
        grid=(t // tm, n // tn),
        in_specs=in_specs,
        out_specs=out_specs,
        out_shape=out_shape,
        scratch_shapes=[pltpu.VMEM((tm, k), BF16)],
        compiler_params=_params("parallel", "arbitrary"),
        name="norm_matmul",
    )(*args)
    return tuple(outs) if side else outs[0]


def _proj_norm_res_kernel(*refs, n_in):
    a_refs = refs[:n_in]
    w_refs = refs[n_in:2 * n_in]
    nw_ref, nw_next_ref, h_ref, o_ref, xn_ref = refs[2 * n_in:]
    y = _dot(a_refs[0][...], w_refs[0][0])
    for a_ref, w_ref in zip(a_refs[1:], w_refs[1:]):
        y = y + _dot(a_ref[...], w_ref[0])
    h_new = h_ref[...] + _rms(y, nw_ref[...])
    o_ref[...] = h_new
    xn_ref[...] = _rms(h_new, nw_next_ref[...]).astype(BF16)


def proj_norm_res(acts, w, layer, nw, nw_next, h, tm=ROW_TILE):
    t, n = h.shape
    n_in = len(acts)
    kw = acts[0].shape[1]
    assert all(a.shape[1] == kw for a in acts) and w.shape[1] == n_in * kw
    in_specs = [pl.BlockSpec((tm, kw), lambda i: (i, 0)) for _ in acts]
    in_specs += [pl.BlockSpec((1, kw, n), functools.partial(lambda piece, i: (layer, piece, 0), piece))
                 for piece in range(n_in)]
    in_specs += [pl.BlockSpec((1, n), lambda i: (0, 0)), pl.BlockSpec((1, n), lambda i: (0, 0)),
                 pl.BlockSpec((tm, n), lambda i: (i, 0))]
    ws = [w] * n_in
    row_spec = pl.BlockSpec((tm, n), lambda i: (i, 0))
    return pl.pallas_call(
        functools.partial(_proj_norm_res_kernel, n_in=n_in),
        grid=(t // tm,),
        in_specs=in_specs,
        out_specs=[row_spec, row_spec],
        out_shape=[jax.ShapeDtypeStruct((t, n), F32), jax.ShapeDtypeStruct((t, n), BF16)],
        compiler_params=_params("parallel"),
        name="proj_norm_res",
    )(*acts, *ws, nw.reshape(1, n), nw_next.reshape(1, n), h)


def _ple_kernel(h_ref, f_ref, pa_ref, pb_ref, wg_ref, wp_ref, o_ref, *, first_block, a_blocks):
    h = h_ref[...] + f_ref[...]
    gate = _sigmoid(_dot(h.astype(BF16), wg_ref[0]))
    p = jnp.where(first_block + pl.program_id(0) < a_blocks, pa_ref[0], pb_ref[0])
    o_ref[...] = h + gate * _dot(p.astype(BF16), wp_ref[0])


def ple(h, f, p_a, p_b, wg, wp, layer, row0=0, rows=None, tm=ROW_TILE):
    n = h.shape[1]
    rows = h.shape[0] if rows is None else rows
    assert row0 % tm == 0 and rows % tm == 0 and p_a.shape[1] % tm == 0 and p_b.shape[1] % tm == 0
    i0 = row0 // tm
    na, nb = p_a.shape[1] // tm, p_b.shape[1] // tm
    return pl.pallas_call(
        functools.partial(_ple_kernel, first_block=i0, a_blocks=na),
        grid=(rows // tm,),
        in_specs=[
            pl.BlockSpec((tm, n), lambda i: (i0 + i, 0)),
            pl.BlockSpec((tm, n), lambda i: (i0 + i, 0)),
            pl.BlockSpec((1, tm, p_a.shape[2]), lambda i: (layer, jnp.minimum(i0 + i, na - 1), 0)),
            pl.BlockSpec((1, tm, p_b.shape[2]), lambda i: (layer, jnp.clip(i0 + i - na, 0, nb - 1), 0)),
            pl.BlockSpec((1,) + wg.shape[1:], lambda i: (layer, 0, 0)),
            pl.BlockSpec((1,) + wp.shape[1:], lambda i: (layer, 0, 0)),
        ],
        out_specs=pl.BlockSpec((tm, n), lambda i: (i, 0)),
        out_shape=jax.ShapeDtypeStruct((rows, n), F32),
        compiler_params=_params("parallel"),
        name="ple",
    )(h, f, p_a, p_b, wg, wp)


def _ffn_kernel(x_ref, xp_ref, xx_ref, wg_ref, wu_ref, cwg_ref, cwu_ref, cbg_ref, cbu_ref,
                wo_ref, nw2_ref, o_ref, xn_ref, hg_ref, hu_ref, *, tm, per_seq):
    i = pl.program_id(0)
    j = pl.program_id(1)

    @pl.when(j == 0)
    def _():
        pos = i % per_seq
        half = SUBLANES_BF16 // 2
        prev = jnp.where(pos == 0, 0.0, xp_ref[...].astype(F32))[half:]
        nxt = jnp.where(pos == per_seq - 1, 0.0, xx_ref[...].astype(F32))[:half]
        xn_ref[0:tm, :] = x_ref[...]
        xn_ref[tm:, :] = jnp.concatenate([nxt, prev], axis=0).astype(BF16)
        o_ref[...] = jnp.zeros_like(o_ref)

    xe = xn_ref[...]

    for w_ref, h_ref in ((wg_ref, hg_ref), (wu_ref, hu_ref)):
        hid = _dot(xe, w_ref[0])
        h_ref[8:, :] = hid
        h_ref[0:8, :] = hid[tm + 8:, :]

    def conv(h_ref, cw_ref, cb_ref, r0, rows):
        cw = cw_ref[...]
        return (h_ref[pl.ds(7 + r0, rows), :] * cw[0:1] + h_ref[pl.ds(8 + r0, rows), :] * cw[1:2]
                + h_ref[pl.ds(9 + r0, rows), :] * cw[2:3] + cb_ref[...])

    rows = min(FFN_ROW_PIECE, tm)
    for r0 in range(0, tm, rows):
        g = conv(hg_ref, cwg_ref, cbg_ref, r0, rows)
        u = conv(hu_ref, cwu_ref, cbu_ref, r0, rows)
        c0 = math.sqrt(2.0 / math.pi)
        gelu = 0.5 * g * (1.0 + jnp.tanh(c0 * (g + 0.044715 * (g * g * g))))
        o_ref[r0:r0 + rows, :] += _dot((gelu * u).astype(BF16), wo_ref[0])

    @pl.when(j == pl.num_programs(1) - 1)
    def _():
        o_ref[...] = _rms(o_ref[...], nw2_ref[...])


def conv_ffn(xn, w_in, w_out, layer, conv_w, conv_b, nw_out, seq_len, tm=FFN_ROW_TILE, tf=FF_TILE):
    t, d = xn.shape
    f = w_out.shape[1]
    tf = min(tf, f)
    nf = f // tf
    per_seq = seq_len // tm
    rg = tm // SUBLANES_BF16
    last_group = t // SUBLANES_BF16 - 1
    cb = conv_b.reshape(1, 2 * f)
    return pl.pallas_call(
        functools.partial(_ffn_kernel, tm=tm, per_seq=per_seq),
        grid=(t // tm, nf),
        in_specs=[
            pl.BlockSpec((tm, d), lambda i, j: (i, 0)),
            pl.BlockSpec((SUBLANES_BF16, d), lambda i, j: (jnp.maximum(i * rg - 1, 0), 0)),
            pl.BlockSpec((SUBLANES_BF16, d), lambda i, j: (jnp.minimum((i + 1) * rg, last_group), 0)),
            pl.BlockSpec((1, d, tf), lambda i, j: (layer, 0, j)),
            pl.BlockSpec((1, d, tf), lambda i, j: (layer, 0, j + nf)),
            pl.BlockSpec((FFN_CONV_W, tf), lambda i, j: (0, j)),
            pl.BlockSpec((FFN_CONV_W, tf), lambda i, j: (0, j + nf)),
            pl.BlockSpec((1, tf), lambda i, j: (0, j)),
            pl.BlockSpec((1, tf), lambda i, j: (0, j + nf)),
            pl.BlockSpec((1, tf, d), lambda i, j: (layer, j, 0)),
            pl.BlockSpec((1, d), lambda i, j: (0, 0)),
        ],
        out_specs=pl.BlockSpec((tm, d), lambda i, j: (i, 0)),
        out_shape=jax.ShapeDtypeStruct((t, d), F32),
        scratch_shapes=[
            pltpu.VMEM((tm + SUBLANES_BF16, d), BF16),
            pltpu.VMEM((8 + tm + SUBLANES_BF16, tf), F32),
            pltpu.VMEM((8 + tm + SUBLANES_BF16, tf), F32),
        ],
        compiler_params=_params("parallel", "arbitrary"),
        name="conv_ffn",
    )(xn, xn, xn, w_in, w_in, conv_w, conv_w, cb, cb, w_out, nw_out.reshape(1, d))


def _diff_attn_kernel(lp_ref, sub_ref, q_ref, k_ref, v_ref, o_ref, *, lambda_init):
    lp = lp_ref[...]
    lam = (jnp.exp(jnp.sum(lp[0:1] * lp[1:2], axis=-1, keepdims=True))
           - jnp.exp(jnp.sum(lp[2:3] * lp[3:4], axis=-1, keepdims=True)) + lambda_init)
    sub = sub_ref[...]
    n_sub = q_ref.shape[1] // ATT_SUB_ROWS

    def scores(i):
        q = q_ref[0, i * ATT_SUB_ROWS:(i + 1) * ATT_SUB_ROWS, :].astype(F32)
        q = (q * (HEAD_DIM ** -0.5 * math.log2(math.e))).astype(BF16)
        return [_dot_nt(q[:, lo:lo + HEAD_DIM], k_ref[0, :, lo:lo + HEAD_DIM]) for lo in (0, HEAD_DIM)]

    def weights(s12):
        es = [jnp.exp2(s - jnp.max(s, axis=-1, keepdims=True)) for s in s12]
        l1, l2 = [jnp.sum(e, axis=-1, keepdims=True) for e in es]
        return (es[0] - es[1] * (lam * l1 / l2)).astype(BF16), 1.0 / l1

    def finish(i, a_inv):
        a, inv_l1 = a_inv
        o = _dot(a, v_ref[0]) * inv_l1
        o_ref[0, i * ATT_SUB_ROWS:(i + 1) * ATT_SUB_ROWS, :] = (_rms(o, sub) * (1.0 - lambda_init)).astype(o_ref.dtype)

    s_next = scores(0)
    a_prev = None
    for i in range(n_sub):
        s_cur = s_next
        if i + 1 < n_sub:
            s_next = scores(i + 1)
        a_cur = weights(s_cur)
        if a_prev is not None:
            finish(i - 1, a_prev)
        a_prev = a_cur
    finish(n_sub - 1, a_prev)


def diff_attention(proj, lam_params, subln, layer, tq=ATT_Q_TILE):
    b, s, _ = proj.shape
    w = 2 * HEAD_DIM
    tq = min(tq, s)
    lambda_init = 0.8 - 0.6 * math.exp(-0.3 * layer)
    return pl.pallas_call(
        functools.partial(_diff_attn_kernel, lambda_init=lambda_init),
        grid=(b, HEADS, s // tq),
        in_specs=[
            pl.BlockSpec((4, HEAD_DIM), lambda bi, h, qi: (0, 0)),
            pl.BlockSpec((1, w), lambda bi, h, qi: (0, 0)),
            pl.BlockSpec((1, tq, w), lambda bi, h, qi: (bi, qi, h)),
            pl.BlockSpec((1, s, w), lambda bi, h, qi: (bi, 0, HEADS + h)),
            pl.BlockSpec((1, s, w), lambda bi, h, qi: (bi, 0, 2 * HEADS + h)),
        ],
        out_specs=pl.BlockSpec((1, tq, w), lambda bi, h, qi: (bi, qi, h)),
        out_shape=jax.ShapeDtypeStruct((b, s, C_QK_W), BF16),
        compiler_params=_params("parallel", "parallel", "arbitrary"),
        name="diff_attention",
    )(lam_params, subln.reshape(1, w), proj, proj, proj)


def _na_key_start_row(first_query_row):
    return (first_query_row - NA_ROWS // 2, 0, GRID_W - NA_SUB_K_ROWS)


def _na_bias(rpb):
    rows = GRID_W
    n_sub = NA_Q_ROWS // NA_SUB_Q_ROWS
    n_dr = 2 * NA_ROWS - 1
    c = np.arange(GRID_W)
    col_start = np.clip(c - NA_COLS // 2, 0, GRID_W - NA_COLS)
    col_ok = (c[None, :] >= col_start[:, None]) & (c[None, :] < col_start[:, None] + NA_COLS)
    dc = np.clip(c[None, :] - c[:, None] + NA_COLS - 1, 0, 2 * NA_COLS - 2)
    sel_c = (np.arange(2 * NA_COLS - 1)[:, None, None] == dc[None]).astype(np.float32)
    tables = jnp.einsum("hab,bxy->haxy", rpb, sel_c, precision=lax.Precision.HIGHEST) * math.log2(math.e)
    tables = jnp.where(col_ok[None, None], tables, -jnp.inf)
    masked = jnp.full(tables.shape[:1] + tables.shape[2:], -jnp.inf, tables.dtype)
    blocks = []
    for rb in (0, 1, rows // NA_Q_ROWS - 1):
        for p in range(n_sub):
            r0 = rb * NA_Q_ROWS + p * NA_SUB_Q_ROWS
            base = int(np.clip(*_na_key_start_row(r0)))
            q_rows = []
            for ql in range(NA_SUB_Q_ROWS):
                r = r0 + ql
                row_start = int(np.clip(r - NA_ROWS // 2, 0, rows - NA_ROWS))
                assert base <= row_start and row_start + NA_ROWS <= base + NA_SUB_K_ROWS
                tiles = []
                for kl in range(NA_SUB_K_ROWS):
                    kr = base + kl
                    inside = row_start <= kr < row_start + NA_ROWS
                    assert not inside or 0 <= kr - r + NA_ROWS - 1 < n_dr
                    tiles.append(tables[:, kr - r + NA_ROWS - 1] if inside else masked)
                q_rows.append(jnp.concatenate(tiles, axis=-1))
            blocks.append(jnp.concatenate(q_rows, axis=-2))
    bias = jnp.stack(blocks, axis=1)
    return bias.reshape(rpb.shape[0], 3, n_sub, NA_SUB_Q_ROWS * GRID_W, NA_SUB_K_ROWS * GRID_W)


def _na_kernel(q_ref, k_ref, v_ref, b_ref, o_ref):
    rb = pl.program_id(2)
    nq = NA_SUB_Q_ROWS * GRID_W
    nk = NA_SUB_K_ROWS * GRID_W
    n_sub = NA_Q_ROWS // NA_SUB_Q_ROWS

    def key_start(p):
        row, lo, hi = _na_key_start_row(rb * NA_Q_ROWS + p * NA_SUB_Q_ROWS)
        return pl.multiple_of(jnp.clip(row, lo, hi) * GRID_W, NA_SUB_Q_ROWS * GRID_W)

    def scores(item):
        hl, p = item
        lanes = slice(hl * HEAD_DIM, (hl + 1) * HEAD_DIM)
        q = (q_ref[0, p * nq:(p + 1) * nq, lanes].astype(F32) * (HEAD_DIM ** -0.5 * math.log2(math.e))).astype(BF16)
        return _dot_nt(q, k_ref[0, pl.ds(key_start(p), nk), lanes]) + b_ref[hl, 0, p]

    def weights(s):
        e = jnp.exp2(s - jnp.max(s, axis=-1, keepdims=True))
        return e.astype(BF16), 1.0 / jnp.sum(e, axis=-1, keepdims=True)

    def finish(item, e_inv):
        hl, p = item
        lanes = slice(hl * HEAD_DIM, (hl + 1) * HEAD_DIM)
        e, inv_l = e_inv
        o = _dot(e, v_ref[0, pl.ds(key_start(p), nk), lanes]) * inv_l
        o_ref[0, p * nq:(p + 1) * nq, lanes] = o.astype(o_ref.dtype)

    items = [(hl, p) for hl in range(NA_HEADS_PER_STEP) for p in range(n_sub)]
    s_next = scores(items[0])
    e_prev = None
    for n, item in enumerate(items):
        s_cur = s_next
        if n + 1 < len(items):
            s_next = scores(items[n + 1])
        e_cur = weights(s_cur)
        if e_prev is not None:
            finish(items[n - 1], e_prev)
        e_prev = e_cur
    finish(items[-1], e_prev)


def neighbourhood_attention(proj, col0, bias):
    b, s, _ = proj.shape
    nq = NA_Q_ROWS * GRID_W
    nblk = s // nq
    hs = NA_HEADS_PER_STEP
    w = hs * HEAD_DIM
    c0 = col0 // w
    groups = HEADS // hs

    def bias_map(bi, h, rb):
        return (h, jnp.where(rb == 0, 0, jnp.where(rb == nblk - 1, 2, 1)), 0, 0, 0)

    return pl.pallas_call(
        _na_kernel,
        grid=(b, groups, nblk),
        in_specs=[
            pl.BlockSpec((1, nq, w), lambda bi, h, rb: (bi, rb, c0 + h)),
            pl.BlockSpec((1, s, w), lambda bi, h, rb: (bi, 0, c0 + groups + h)),
            pl.BlockSpec((1, s, w), lambda bi, h, rb: (bi, 0, c0 + 2 * groups + h)),
            pl.BlockSpec((hs, 1) + bias.shape[2:], bias_map),
        ],
        out_specs=pl.BlockSpec((1, nq, w), lambda bi, h, rb: (bi, rb, h)),
        out_shape=jax.ShapeDtypeStruct((b, s, B_W), BF16),
        compiler_params=_params("parallel", "parallel", "arbitrary"),
        name="neighbourhood_attention",
    )(proj, proj, proj, bias)


HALO = SUBLANES_BF16


def _split3(x):
    hi = x.astype(BF16)
    r1 = x - hi.astype(F32)
    mid = r1.astype(BF16)
    lo = (r1 - mid.astype(F32)).astype(BF16)
    return hi, mid, lo


def _dn_prep_kernel(x_ref, xp_ref, xx_ref, cw_ref, gt_ref, alog_ref, dtb_ref,
                    q_ref, k_ref, v_ref, gc_ref, beta_ref, xs_ref):
    i = pl.program_id(1)
    tb = x_ref.shape[1]
    pad = A_CONV_W // 2
    xs_ref[0:HALO, :] = jnp.where(i == 0, 0.0, xp_ref[0].astype(F32))
    xs_ref[HALO:HALO + tb, :] = x_ref[0].astype(F32)
    xs_ref[HALO + tb:, :] = jnp.where(i == pl.num_programs(1) - 1, 0.0, xx_ref[0].astype(F32))
    cw = cw_ref[...]
    y = xs_ref[pl.ds(HALO - pad, tb), :] * cw[0:1]
    for t in range(1, A_CONV_W):
        y = y + xs_ref[pl.ds(HALO - pad + t, tb), :] * cw[t:t + 1]
    y = y * _sigmoid(y)
    for h in range(HEADS):
        qh = y[:, h * HEAD_DIM:(h + 1) * HEAD_DIM]
        kh = y[:, (HEADS + h) * HEAD_DIM:(HEADS + h + 1) * HEAD_DIM]
        qn = qh * (lax.rsqrt(jnp.sum(qh * qh, axis=-1, keepdims=True) + 1e-6) * (HEAD_DIM ** -0.5))
        kn = kh * lax.rsqrt(jnp.sum(kh * kh, axis=-1, keepdims=True) + 1e-6)
        q_ref[0, h] = qn.astype(q_ref.dtype)
        k_ref[0, h] = kn.astype(k_ref.dtype)
        v_ref[0, h] = y[:, (2 * HEADS + h) * HEAD_DIM:(2 * HEADS + h + 1) * HEAD_DIM].astype(v_ref.dtype)

    nd = 2 * HEADS
    gates = gt_ref[0]
    z = gates[:, 0:nd] + dtb_ref[...]
    softplus = jnp.maximum(z, 0.0) + jnp.log(1.0 + jnp.exp(-jnp.abs(z)))
    g = -jnp.exp(alog_ref[...]) * softplus
    beta_ref[0] = _sigmoid(gates[:, nd:2 * nd])
    r = lax.broadcasted_iota(jnp.int32, (tb, tb), 0)
    c = lax.broadcasted_iota(jnp.int32, (tb, tb), 1)
    lower = jnp.where(r >= c, 1.0, 0.0).astype(BF16)
    upper = jnp.where(r <= c, 1.0, 0.0).astype(BF16)
    parts = _split3(g)
    fwd = _dot(lower, parts[0]) + _dot(lower, parts[1]) + _dot(lower, parts[2])
    bwd = _dot(upper, parts[0]) + _dot(upper, parts[1]) + _dot(upper, parts[2])
    col = lax.broadcasted_iota(jnp.int32, (tb, nd), 1)
    gc_ref[0] = jnp.where(col < HEADS, fwd, bwd)


def deltanet_prep(proj, gates, conv_w, a_log, dt_bias, tb=DN_BLOCK):
    b, s, _ = proj.shape
    nb = s // tb
    rh = tb // HALO
    nd = 2 * HEADS
    hs = jax.ShapeDtypeStruct((b, HEADS, s, HEAD_DIM), BF16)
    head_spec = pl.BlockSpec((1, HEADS, tb, HEAD_DIM), lambda bi, i: (bi, 0, i, 0))
    vec_spec = pl.BlockSpec((1, tb, nd), lambda bi, i: (bi, i, 0))
    return pl.pallas_call(
        _dn_prep_kernel,
        grid=(b, nb),
        in_specs=[
            pl.BlockSpec((1, tb, A_QKV_W), lambda bi, i: (bi, i, 0)),
            pl.BlockSpec((1, HALO, A_QKV_W), lambda bi, i: (bi, jnp.maximum(i * rh - 1, 0), 0)),
            pl.BlockSpec((1, HALO, A_QKV_W), lambda bi, i: (bi, jnp.minimum((i + 1) * rh, s // HALO - 1), 0)),
            pl.BlockSpec((A_CONV_W, A_QKV_W), lambda bi, i: (0, 0)),
            pl.BlockSpec((1, tb, LANES), lambda bi, i: (bi, i, 0)),
            pl.BlockSpec((1, nd), lambda bi, i: (0, 0)),
            pl.BlockSpec((1, nd), lambda bi, i: (0, 0)),
        ],
        out_specs=[head_spec, head_spec, head_spec, vec_spec, vec_spec],
        out_shape=[hs, hs, hs, jax.ShapeDtypeStruct((b, s, nd), F32), jax.ShapeDtypeStruct((b, s, nd), F32)],
        scratch_shapes=[pltpu.VMEM((tb + 2 * HALO, A_QKV_W), F32)],
        compiler_params=_params("parallel", "arbitrary"),
        name="deltanet_prep",
    )(proj, proj, proj, conv_w, gates, a_log.reshape(1, nd), dt_bias.reshape(1, nd))


def _split2(x):
    hi = x.astype(BF16)
    return hi, (x - hi.astype(F32)).astype(BF16)


def _unit_triangular_solve(ms, rhss, same_block):
    n = ms[0].shape[0]
    grp = DN_GROUP
    ng = n // grp
    r = lax.broadcasted_iota(jnp.int32, (grp, grp), 0)
    c = lax.broadcasted_iota(jnp.int32, (grp, grp), 1)
    base_mask = (r // DN_BASE) == (c // DN_BASE)
    row = lax.broadcasted_iota(jnp.int32, (DN_BASE, grp), 0)
    lane = lax.broadcasted_iota(jnp.int32, (DN_BASE, grp), 1)
    left = lane < DN_BASE

    def pair_lhs(hi, lo):
        hi_f, lo_f = hi.astype(F32), lo.astype(F32)
        swapped = pltpu.roll(hi_f, DN_BASE, 1)
        top = jnp.concatenate([jnp.where(left, hi_f, swapped), jnp.where(left, lo_f, 0.0)], axis=1)
        bottom = jnp.concatenate([jnp.where(left, swapped, hi_f), jnp.where(left, 0.0, lo_f)], axis=1)
        return jnp.concatenate([top, bottom], axis=0).astype(BF16)

    def pair_product(lhs, rhs_hi, rhs_lo):
        res = _dot(lhs, jnp.concatenate([rhs_hi, rhs_lo, rhs_hi, rhs_hi], axis=0))
        return jnp.where(left, res[:DN_BASE], res[DN_BASE:])

    pieces = [jnp.where(base_mask, m[g * grp:(g + 1) * grp, g * grp:(g + 1) * grp], 0.0)
              for m in ms for g in range(ng)]
    pws = [piece[:DN_BASE] + piece[DN_BASE:] for piece in pieces]
    ts = [jnp.where(lane % DN_BASE == row, 1.0, 0.0) + pw for pw in pws]
    splits = [_split2(pw) for pw in pws]
    lhss = [pair_lhs(hi, lo) for hi, lo in splits]
    size = 2
    while size < DN_BASE:
        pws = [pair_product(lhs, hi, lo) for lhs, (hi, lo) in zip(lhss, splits)]
        splits = [_split2(pw) for pw in pws]
        lhss = [pair_lhs(hi, lo) for hi, lo in splits]
        ts = [t + pair_product(lhs, *_split2(t)) for lhs, t in zip(lhss, ts)]
        size *= 2
    t_bases = [jnp.concatenate([jnp.where(left, t, 0.0), jnp.where(left, 0.0, t)], axis=0).astype(BF16)
               for t in ts]
    offs = {}
    blk = 2 * DN_BASE
    while blk <= n:
        offs[blk] = [jnp.where(same_block(blk), jnp.where(same_block(blk // 2), 0.0, m), 0.0).astype(BF16)
                     for m in ms]
        blk *= 2

    def apply(blk, ys):
        if blk == DN_BASE:
            ybs = [y.astype(BF16) for y in ys]
            return [jnp.concatenate([_dot(t_bases[i * ng + g], yb[g * grp:(g + 1) * grp]) for g in range(ng)],
                                    axis=0) for i, yb in enumerate(ybs)]
        zs = apply(blk // 2, ys)
        corr = apply(blk // 2, [_dot(off, z.astype(BF16)) for off, z in zip(offs[blk], zs)])
        return [z + cr for z, cr in zip(zs, corr)]

    return apply(n, rhss)


def _dn_block(chains):
    n = chains[0][0].shape[0]
    r = lax.broadcasted_iota(jnp.int32, (n, n), 0)
    c = lax.broadcasted_iota(jnp.int32, (n, n), 1)

    def same_block(size):
        return (r // size) == (c // size)

    ms, rhss, pre = [], [], []
    for q, k, v, gcol, grow, beta, state_ref, forward in chains:
        d = (r - c) if forward else (c - r)
        kf = k.astype(F32)
        kb = kf * beta
        decay = jnp.exp(jnp.where(d >= 0, gcol - grow, NEG_BIG))
        ms.append(jnp.where(d > 0, -(_dot_nt(kb.astype(BF16), k) * decay), 0.0))
        intra = (_dot_nt(q, k) * decay).astype(BF16)
        eg = jnp.exp(gcol)
        rhss.append(jnp.concatenate([v.astype(F32) * beta, kb * eg], axis=1))
        g_last = grow[:, n - 1:n] if forward else grow[:, 0:1]
        pre.append((intra, (q.astype(F32) * eg).astype(BF16), (kf * jnp.exp(g_last - gcol)).astype(BF16),
                    jnp.exp(g_last)))
    sols = _unit_triangular_solve(ms, rhss, same_block)
    outs = []
    for chain, sol, (intra, q_dec, k_dec, blk_decay) in zip(chains, sols, pre):
        state_ref = chain[6]
        state = state_ref[...]
        sb = state.astype(BF16)
        v_new = sol[:, :HEAD_DIM] - _dot(sol[:, HEAD_DIM:].astype(BF16), sb)
        vb = v_new.astype(BF16)
        outs.append(_dot(q_dec, sb) + _dot(intra, vb))
        state_ref[...] = state * blk_decay + _dot_tn(k_dec, vb)
    return outs


def _dn_kernel(qf_ref, kf_ref, vf_ref, qb_ref, kb_ref, vb_ref, gcf_ref, gcb_ref, btf_ref, btb_ref,
               grf_ref, grb_ref, of_ref, ob_ref, state_ref):
    hb = qf_ref.shape[1]

    @pl.when(pl.program_id(2) == 0)
    def _():
        state_ref[...] = jnp.zeros_like(state_ref)

    lane = lax.broadcasted_iota(jnp.int32, gcf_ref.shape[1:], 1)

    def pick(ref, idx):
        return jnp.sum(jnp.where(lane == idx, ref[0], 0.0), axis=-1, keepdims=True)

    chains = []
    for hl in range(hb):
        h = pl.program_id(1) * hb + hl
        chains.append((qf_ref[0, hl], kf_ref[0, hl], vf_ref[0, hl], pick(gcf_ref, h),
                       grf_ref[0, pl.ds(h, 1), :], pick(btf_ref, h), state_ref.at[0, hl], True))
        chains.append((qb_ref[0, hl], kb_ref[0, hl], vb_ref[0, hl], pick(gcb_ref, HEADS + h),
                       grb_ref[0, pl.ds(HEADS + h, 1), :], pick(btb_ref, HEADS + h), state_ref.at[1, hl], False))
    outs = _dn_block(chains)
    for hl in range(hb):
        of_ref[0, hl] = outs[2 * hl]
        ob_ref[0, hl] = outs[2 * hl + 1]


def deltanet_scan(q, k, v, gc, beta, gc_rows, tb=DN_BLOCK, hb=DN_HEADS_PER_STEP):
    b, _, s, _ = q.shape
    nb = s // tb
    nd = 2 * HEADS
    fwd = pl.BlockSpec((1, hb, tb, HEAD_DIM), lambda bi, h, c: (bi, h, c, 0))
    bwd = pl.BlockSpec((1, hb, tb, HEAD_DIM), lambda bi, h, c: (bi, h, nb - 1 - c, 0))
    vec_f = pl.BlockSpec((1, tb, nd), lambda bi, h, c: (bi, c, 0))
    vec_b = pl.BlockSpec((1, tb, nd), lambda bi, h, c: (bi, nb - 1 - c, 0))
    row_f = pl.BlockSpec((1, nd, tb), lambda bi, h, c: (bi, 0, c))
    row_b = pl.BlockSpec((1, nd, tb), lambda bi, h, c: (bi, 0, nb - 1 - c))
    os_ = jax.ShapeDtypeStruct((b, HEADS, s, HEAD_DIM), F32)
    return pl.pallas_call(
        _dn_kernel,
        grid=(b, HEADS // hb, nb),
        in_specs=[fwd, fwd, fwd, bwd, bwd, bwd, vec_f, vec_b, vec_f, vec_b, row_f, row_b],
        out_specs=[fwd, bwd],
        out_shape=[os_, os_],
        scratch_shapes=[pltpu.VMEM((2, hb, HEAD_DIM, HEAD_DIM), F32)],
        compiler_params=_params("parallel", "parallel", "arbitrary"),
        name="deltanet_scan",
    )(q, k, v, q, k, v, gc, gc, beta, beta, gc_rows, gc_rows)


def _dn_out_kernel(of_ref, ob_ref, z_ref, nw_ref, o_ref):
    nw = nw_ref[...]
    for h in range(HEADS):
        o = _rms(of_ref[0, h] + ob_ref[0, h], nw)
        z = z_ref[0, :, h * HEAD_DIM:(h + 1) * HEAD_DIM].astype(F32)
        o_ref[0, :, h * HEAD_DIM:(h + 1) * HEAD_DIM] = (o * (z * _sigmoid(z))).astype(o_ref.dtype)


def deltanet_out(o_f, o_b, proj, z_col0, out_norm, ts=ROW_TILE):
    b, _, s, _ = o_f.shape
    head_spec = pl.BlockSpec((1, HEADS, ts, HEAD_DIM), lambda bi, i: (bi, 0, i, 0))
    zb = z_col0 // A_V_W
    return pl.pallas_call(
        _dn_out_kernel,
        grid=(b, s // ts),
        in_specs=[head_spec, head_spec,
                  pl.BlockSpec((1, ts, A_V_W), lambda bi, i: (bi, i, zb)),
                  pl.BlockSpec((1, HEAD_DIM), lambda bi, i: (0, 0))],
        out_specs=pl.BlockSpec((1, ts, A_V_W), lambda bi, i: (bi, i, 0)),
        out_shape=jax.ShapeDtypeStruct((b, s, A_V_W), BF16),
        compiler_params=_params("parallel", "parallel"),
        name="deltanet_out",
    )(o_f, o_b, proj, out_norm.reshape(1, HEAD_DIM))


def _rope_tables(s):
    half = ROPE_DIMS // 2
    inv = ROPE_THETA ** (-jnp.arange(0, ROPE_DIMS, 2, dtype=F32) / ROPE_DIMS)
    ang = jnp.arange(s, dtype=F32)[:, None] * inv[None, :]
    cos, sin = jnp.cos(ang), jnp.sin(ang)
    rest = HEAD_DIM - ROPE_DIMS
    c = jnp.concatenate([cos, cos, jnp.ones((s, rest), F32)], axis=1)
    s1 = jnp.concatenate([jnp.zeros((s, half), F32), sin, jnp.zeros((s, rest), F32)], axis=1)
    s2 = jnp.concatenate([-sin, jnp.zeros((s, half + rest), F32)], axis=1)
    return c, s1, s2


def _trunk(x_a, x_b, p_a, p_b, ab_w_in, ab_conv_w, ab_a_log, ab_dt_bias, ab_out_norm, ab_rpb, ab_w_out,
           c_w_in, c_lambda, c_subln, c_w_out, norms, ffn_w_in, ffn_conv_w, ffn_conv_b,
           ffn_w_out, ple_w_proj, ple_w_gate):
    b_a, s, d = x_a.shape
    b = b_a + x_b.shape[0]
    t, t_a = b * s, b_a * s
    depth = norms.shape[0]
    depth_p = p_a.shape[0]
    h = jnp.concatenate([x_a, x_b], axis=0).reshape(t, d)
    rope = _rope_tables(s)
    o1 = A_QKV_W
    o2 = o1 + A_V_W
    o3 = o2 + 4 * HEADS
    ab_main = _column_tiles(jnp.concatenate([ab_w_in[:, :, :o2], ab_w_in[:, :, o3:]], axis=2), COL_TILE)
    ab_gate = jnp.pad(ab_w_in[:, :, o2:o3], ((0, 0), (0, 0), (0, LANES - 4 * HEADS))).astype(BF16)
    c_in = _column_tiles(c_w_in, COL_TILE)
    ffn_in = ffn_w_in.astype(BF16)
    ab_out, c_out, ffn_out = ab_w_out.astype(BF16), c_w_out.astype(BF16), ffn_w_out.astype(BF16)
    ple_gate, ple_proj = ple_w_gate.astype(BF16), ple_w_proj.astype(BF16)
    for layer in range(depth):
        j = layer // 2
        if layer % 2 == 0:
            proj, gates = norm_matmul(h, norms[layer, 0], ab_main, j, BF16, side_w=ab_gate)
            proj, gates = proj.reshape(b, s, -1), gates.reshape(b, s, LANES)
            qa, ka, va, gc, beta = deltanet_prep(proj, gates, ab_conv_w[j], ab_a_log[j], ab_dt_bias[j])
            o_f, o_b = deltanet_scan(qa, ka, va, gc, beta, jnp.transpose(gc, (0, 2, 1)))
            o_a = deltanet_out(o_f, o_b, proj, o1, ab_out_norm[j])
            o_nb = neighbourhood_attention(proj, o2, _na_bias(ab_rpb[j]))
            h, xn = proj_norm_res([o_a.reshape(t, -1), o_nb.reshape(t, -1)], ab_out, j, norms[layer, 1],
                                  norms[layer, 2], h)
        else:
            proj = norm_matmul(h, norms[layer, 0], c_in, j, BF16, rope=rope,
                               rope_tiles=2 * C_QK_W // COL_TILE, seq_len=s).reshape(b, s, -1)
            o_c = diff_attention(proj, c_lambda[j], c_subln[j], layer)
            h, xn = proj_norm_res([o_c.reshape(t, -1)], c_out, j, norms[layer, 1], norms[layer, 2], h)
        f = conv_ffn(xn, ffn_in, ffn_out, layer, ffn_conv_w[layer], ffn_conv_b[layer], norms[layer, 3], s)
        ple_args = (f, p_a.reshape(depth_p, t_a, -1), p_b.reshape(depth_p, t - t_a, -1), ple_gate, ple_proj, layer)
        if layer + 1 < depth:
            h = ple(h, *ple_args)
    y_a = ple(h, *ple_args, row0=0, rows=t_a)
    y_b = ple(h, *ple_args, row0=t_a, rows=t - t_a)
    return y_a.reshape(b_a, s, d), y_b.reshape(b - b_a, s, d)


def kernel(x_prompt, x_sample, p_prompt, p_sample, ab_w_in, ab_conv_w, ab_a_log, ab_dt_bias, ab_out_norm,
           ab_rpb, ab_w_out, c_w_in, c_lambda, c_subln, c_w_out, norms, ffn_w_in, ffn_conv_w, ffn_conv_b,
           ffn_w_out, ple_w_proj, ple_w_gate):
    return _trunk(x_prompt, x_sample, p_prompt, p_sample, ab_w_in, ab_conv_w, ab_a_log, ab_dt_bias,
                  ab_out_norm, ab_rpb, ab_w_out, c_w_in, c_lambda, c_subln, c_w_out, norms, ffn_w_in,
                  ffn_conv_w, ffn_conv_b, ffn_w_out, ple_w_proj, ple_w_gate)
```

```python
import functools
import math

import jax
import jax.numpy as jnp
import numpy as np
from jax import lax
from jax.experimental import pallas as pl
from jax.experimental.pallas import tpu as pltpu

F32 = jnp.float32
BF16 = jnp.bfloat16

D_MODEL = 2048
PLE_DIM = 256
GRID_W = 64
RMS_EPS = 1e-6
HEADS = 8
HEAD_DIM = 128
A_CONV_W = 5
NA_ROWS = 8
NA_COLS = 16
ROPE_THETA = 500000.0
ROPE_DIMS = HEAD_DIM // 4
D_FF = 8192
FFN_CONV_W = 3
A_QKV_W = 3 * HEADS * HEAD_DIM
A_V_W = HEADS * HEAD_DIM
B_W = HEADS * HEAD_DIM
C_QK_W = HEADS * 2 * HEAD_DIM

LANES = 128
SUBLANES_BF16 = 16
VMEM_LIMIT = 56 * 1024 * 1024
NEG_BIG = -1e30

ROW_TILE = 512
MM_ROW_TILE = 1024
COL_TILE = 1024
FF_TILE = 512
FFN_ROW_TILE = 1024
FFN_ROW_PIECE = 256
DN_BLOCK = 256
DN_BASE = 64
DN_GROUP = 128
DN_HEADS_PER_STEP = 4
ATT_Q_TILE = 1024
ATT_SUB_ROWS = 256
NA_Q_ROWS = 8
NA_HEADS_PER_STEP = 4
NA_SUB_Q_ROWS = 4
NA_SUB_K_ROWS = 12


def _params(*sem):
    return pltpu.CompilerParams(dimension_semantics=sem, vmem_limit_bytes=VMEM_LIMIT)


def _rms(x, w):
    return x * lax.rsqrt(jnp.mean(x * x, axis=-1, keepdims=True) + RMS_EPS) * w


def _dot(a, b):
    return jnp.dot(a, b, preferred_element_type=F32)


def _dot_nt(a, b):
    return lax.dot_general(a, b, (((1,), (1,)), ((), ())), preferred_element_type=F32)


def _dot_tn(a, b):
    return lax.dot_general(a, b, (((0,), (0,)), ((), ())), preferred_element_type=F32)


def _sigmoid(x):
    return 1.0 / (1.0 + jnp.exp(-x))


def _norm_matmul_kernel(*refs, rope_tiles, side):
    x_ref, nw_ref, w_ref = refs[:3]
    rest = list(refs[3:])
    side_w_ref = rest.pop(0) if side else None
    if rope_tiles:
        rc_ref, rs1_ref, rs2_ref = rest[:3]
        rest = rest[3:]
    o_ref = rest.pop(0)
    side_o_ref = rest.pop(0) if side else None
    xn_ref = rest.pop(0)
    j = pl.program_id(1)

    @pl.when(j == 0)
    def _():
        xn = _rms(x_ref[...], nw_ref[...]).astype(BF16)
        xn_ref[...] = xn
        if side:
            side_o_ref[...] = _dot(xn, side_w_ref[0])

    y = _dot(xn_ref[...], w_ref[0])
    if not rope_tiles:
        o_ref[...] = y.astype(o_ref.dtype)
        return

    @pl.when(j < rope_tiles)
    def _():
        n = y.shape[1]
        reps = n // LANES
        c = jnp.concatenate([rc_ref[...]] * reps, axis=1)
        s1 = jnp.concatenate([rs1_ref[...]] * reps, axis=1)
        s2 = jnp.concatenate([rs2_ref[...]] * reps, axis=1)
        half = ROPE_DIMS // 2
        r = y * c + pltpu.roll(y, half, 1) * s1 + pltpu.roll(y, n - half, 1) * s2
        o_ref[...] = r.astype(o_ref.dtype)

    @pl.when(j >= rope_tiles)
    def _():
        o_ref[...] = y.astype(o_ref.dtype)


def norm_matmul(x, nw, w, layer, out_dtype, rope=None, rope_tiles=0, seq_len=None, side_w=None,
                tm=MM_ROW_TILE, tn=COL_TILE):
    t, k = x.shape
    n = w.shape[2]
    tm, tn = min(tm, t), min(tn, n)
    assert t % tm == 0 and n % tn == 0
    side = side_w is not None
    in_specs = [
        pl.BlockSpec((tm, k), lambda i, j: (i, 0)),
        pl.BlockSpec((1, k), lambda i, j: (0, 0)),
        pl.BlockSpec((1, k, tn), lambda i, j: (layer, 0, j)),
    ]
    args = [x, nw.reshape(1, k), w]
    out_specs = [pl.BlockSpec((tm, tn), lambda i, j: (i, j))]
    out_shape = [jax.ShapeDtypeStruct((t, n), out_dtype)]
    if side:
        m = side_w.shape[2]
        in_specs.append(pl.BlockSpec((1, k, m), lambda i, j: (layer, 0, 0)))
        args.append(side_w)
        out_specs.append(pl.BlockSpec((tm, m), lambda i, j: (i, 0)))
        out_shape.append(jax.ShapeDtypeStruct((t, m), F32))
    if rope_tiles:
        per_seq = seq_len // tm
        for tab in rope:
            in_specs.append(pl.BlockSpec((tm, LANES), lambda i, j: (i % per_seq, 0)))
            args.append(tab)
    outs = pl.pallas_call(
        functools.partial(_norm_matmul_kernel, rope_tiles=rope_tiles, side=side),
        grid=(t // tm, n // tn),
        in_specs=in_specs,
        out_specs=out_specs,
        out_shape=out_shape,
        scratch_shapes=[pltpu.VMEM((tm, k), BF16)],
        compiler_params=_params("parallel", "arbitrary"),
        name="norm_matmul",
    )(*args)
    return tuple(outs) if side else outs[0]


def _proj_norm_res_kernel(*refs, n_in):
    a_refs = refs[:n_in]
    w_refs = refs[n_in:2 * n_in]
    nw_ref, nw_next_ref, h_ref, o_ref, xn_ref = refs[2 * n_in:]
    y = _dot(a_refs[0][...], w_refs[0][0])
    for a_ref, w_ref in zip(a_refs[1:], w_refs[1:]):
        y = y + _dot(a_ref[...], w_ref[0])
    h_new = h_ref[...] + _rms(y, nw_ref[...])
    o_ref[...] = h_new
    xn_ref[...] = _rms(h_new, nw_next_ref[...]).astype(BF16)


def proj_norm_res(acts, w, layer, nw, nw_next, h, tm=ROW_TILE):
    t, n = h.shape
    n_in = len(acts)
    kw = acts[0].shape[1]
    assert all(a.shape[1] == kw for a in acts) and w.shape[1] == n_in * kw
    in_specs = [pl.BlockSpec((tm, kw), lambda i: (i, 0)) for _ in acts]
    in_specs += [pl.BlockSpec((1, kw, n), functools.partial(lambda piece, i: (layer, piece, 0), piece))
                 for piece in range(n_in)]
    in_specs += [pl.BlockSpec((1, n), lambda i: (0, 0)), pl.BlockSpec((1, n), lambda i: (0, 0)),
                 pl.BlockSpec((tm, n), lambda i: (i, 0))]
    ws = [w] * n_in
    row_spec = pl.BlockSpec((tm, n), lambda i: (i, 0))
    return pl.pallas_call(
        functools.partial(_proj_norm_res_kernel, n_in=n_in),
        grid=(t // tm,),
        in_specs=in_specs,
        out_specs=[row_spec, row_spec],
        out_shape=[jax.ShapeDtypeStruct((t, n), F32), jax.ShapeDtypeStruct((t, n), BF16)],
        compiler_params=_params("parallel"),
        name="proj_norm_res",
    )(*acts, *ws, nw.reshape(1, n), nw_next.reshape(1, n), h)


def _ple_kernel(h_ref, f_ref, pa_ref, pb_ref, wg_ref, wp_ref, o_ref, *, first_block, a_blocks):
    h = h_ref[...] + f_ref[...]
    gate = _sigmoid(_dot(h.astype(BF16), wg_ref[0]))
    p = jnp.where(first_block + pl.program_id(0) < a_blocks, pa_ref[0], pb_ref[0])
    o_ref[...] = h + gate * _dot(p.astype(BF16), wp_ref[0])


def ple(h, f, p_a, p_b, wg, wp, layer, row0=0, rows=None, tm=ROW_TILE):
    n = h.shape[1]
    rows = h.shape[0] if rows is None else rows
    assert row0 % tm == 0 and rows % tm == 0 and p_a.shape[1] % tm == 0 and p_b.shape[1] % tm == 0
    i0 = row0 // tm
    na, nb = p_a.shape[1] // tm, p_b.shape[1] // tm
    return pl.pallas_call(
        functools.partial(_ple_kernel, first_block=i0, a_blocks=na),
        grid=(rows // tm,),
        in_specs=[
            pl.BlockSpec((tm, n), lambda i: (i0 + i, 0)),
            pl.BlockSpec((tm, n), lambda i: (i0 + i, 0)),
            pl.BlockSpec((1, tm, p_a.shape[2]), lambda i: (layer, jnp.minimum(i0 + i, na - 1), 0)),
            pl.BlockSpec((1, tm, p_b.shape[2]), lambda i: (layer, jnp.clip(i0 + i - na, 0, nb - 1), 0)),
            pl.BlockSpec((1,) + wg.shape[1:], lambda i: (layer, 0, 0)),
            pl.BlockSpec((1,) + wp.shape[1:], lambda i: (layer, 0, 0)),
        ],
        out_specs=pl.BlockSpec((tm, n), lambda i: (i, 0)),
        out_shape=jax.ShapeDtypeStruct((rows, n), F32),
        compiler_params=_params("parallel"),
        name="ple",
    )(h, f, p_a, p_b, wg, wp)


def _ffn_kernel(x_ref, xp_ref, xx_ref, wg_ref, wu_ref, cwg_ref, cwu_ref, cbg_ref, cbu_ref,
                wo_ref, nw2_ref, o_ref, xn_ref, hg_ref, hu_ref, *, tm, per_seq):
    i = pl.program_id(0)
    j = pl.program_id(1)

    @pl.when(j == 0)
    def _():
        pos = i % per_seq
        half = SUBLANES_BF16 // 2
        prev = jnp.where(pos == 0, 0.0, xp_ref[...].astype(F32))[half:]
        nxt = jnp.where(pos == per_seq - 1, 0.0, xx_ref[...].astype(F32))[:half]
        xn_ref[0:tm, :] = x_ref[...]
        xn_ref[tm:, :] = jnp.concatenate([nxt, prev], axis=0).astype(BF16)
        o_ref[...] = jnp.zeros_like(o_ref)

    xe = xn_ref[...]

    for w_ref, h_ref in ((wg_ref, hg_ref), (wu_ref, hu_ref)):
        hid = _dot(xe, w_ref[0])
        h_ref[8:, :] = hid
        h_ref[0:8, :] = hid[tm + 8:, :]

    def conv(h_ref, cw_ref, cb_ref, r0, rows):
        cw = cw_ref[...]
        return (h_ref[pl.ds(7 + r0, rows), :] * cw[0:1] + h_ref[pl.ds(8 + r0, rows), :] * cw[1:2]
                + h_ref[pl.ds(9 + r0, rows), :] * cw[2:3] + cb_ref[...])

    rows = min(FFN_ROW_PIECE, tm)
    for r0 in range(0, tm, rows):
        g = conv(hg_ref, cwg_ref, cbg_ref, r0, rows)
        u = conv(hu_ref, cwu_ref, cbu_ref, r0, rows)
        c0 = math.sqrt(2.0 / math.pi)
        gelu = 0.5 * g * (1.0 + jnp.tanh(c0 * (g + 0.044715 * (g * g * g))))
        o_ref[r0:r0 + rows, :] += _dot((gelu * u).astype(BF16), wo_ref[0])

    @pl.when(j == pl.num_programs(1) - 1)
    def _():
        o_ref[...] = _rms(o_ref[...], nw2_ref[...])


def conv_ffn(xn, w_in, w_out, layer, conv_w, conv_b, nw_out, seq_len, tm=FFN_ROW_TILE, tf=FF_TILE):
    t, d = xn.shape
    f = w_out.shape[1]
    tf = min(tf, f)
    nf = f // tf
    per_seq = seq_len // tm
    rg = tm // SUBLANES_BF16
    last_group = t // SUBLANES_BF16 - 1
    cb = conv_b.reshape(1, 2 * f)
    return pl.pallas_call(
        functools.partial(_ffn_kernel, tm=tm, per_seq=per_seq),
        grid=(t // tm, nf),
        in_specs=[
            pl.BlockSpec((tm, d), lambda i, j: (i, 0)),
            pl.BlockSpec((SUBLANES_BF16, d), lambda i, j: (jnp.maximum(i * rg - 1, 0), 0)),
            pl.BlockSpec((SUBLANES_BF16, d), lambda i, j: (jnp.minimum((i + 1) * rg, last_group), 0)),
            pl.BlockSpec((1, d, tf), lambda i, j: (layer, 0, j)),
            pl.BlockSpec((1, d, tf), lambda i, j: (layer, 0, j + nf)),
            pl.BlockSpec((FFN_CONV_W, tf), lambda i, j: (0, j)),
            pl.BlockSpec((FFN_CONV_W, tf), lambda i, j: (0, j + nf)),
            pl.BlockSpec((1, tf), lambda i, j: (0, j)),
            pl.BlockSpec((1, tf), lambda i, j: (0, j + nf)),
            pl.BlockSpec((1, tf, d), lambda i, j: (layer, j, 0)),
            pl.BlockSpec((1, d), lambda i, j: (0, 0)),
        ],
        out_specs=pl.BlockSpec((tm, d), lambda i, j: (i, 0)),
        out_shape=jax.ShapeDtypeStruct((t, d), F32),
        scratch_shapes=[
            pltpu.VMEM((tm + SUBLANES_BF16, d), BF16),
            pltpu.VMEM((8 + tm + SUBLANES_BF16, tf), F32),
            pltpu.VMEM((8 + tm + SUBLANES_BF16, tf), F32),
        ],
        compiler_params=_params("parallel", "arbitrary"),
        name="conv_ffn",
    )(xn, xn, xn, w_in, w_in, conv_w, conv_w, cb, cb, w_out, nw_out.reshape(1, d))


def _diff_attn_kernel(lp_ref, sub_ref, q_ref, k_ref, v_ref, o_ref, *, lambda_init):
    lp = lp_ref[...]
    lam = (jnp.exp(jnp.sum(lp[0:1] * lp[1:2], axis=-1, keepdims=True))
           - jnp.exp(jnp.sum(lp[2:3] * lp[3:4], axis=-1, keepdims=True)) + lambda_init)
    sub = sub_ref[...]
    n_sub = q_ref.shape[1] // ATT_SUB_ROWS

    def scores(i):
        q = q_ref[0, i * ATT_SUB_ROWS:(i + 1) * ATT_SUB_ROWS, :].astype(F32)
        q = (q * (HEAD_DIM ** -0.5 * math.log2(math.e))).astype(BF16)
        return [_dot_nt(q[:, lo:lo + HEAD_DIM], k_ref[0, :, lo:lo + HEAD_DIM]) for lo in (0, HEAD_DIM)]

    def weights(s12):
        es = [jnp.exp2(s - jnp.max(s, axis=-1, keepdims=True)) for s in s12]
        l1, l2 = [jnp.sum(e, axis=-1, keepdims=True) for e in es]
        return (es[0] - es[1] * (lam * l1 / l2)).astype(BF16), 1.0 / l1

    def finish(i, a_inv):
        a, inv_l1 = a_inv
        o = _dot(a, v_ref[0]) * inv_l1
        o_ref[0, i * ATT_SUB_ROWS:(i + 1) * ATT_SUB_ROWS, :] = (_rms(o, sub) * (1.0 - lambda_init)).astype(o_ref.dtype)

    s_next = scores(0)
    a_prev = None
    for i in range(n_sub):
        s_cur = s_next
        if i + 1 < n_sub:
            s_next = scores(i + 1)
        a_cur = weights(s_cur)
        if a_prev is not None:
            finish(i - 1, a_prev)
        a_prev = a_cur
    finish(n_sub - 1, a_prev)


def diff_attention(proj, lam_params, subln, layer, tq=ATT_Q_TILE):
    b, s, _ = proj.shape
    w = 2 * HEAD_DIM
    tq = min(tq, s)
    lambda_init = 0.8 - 0.6 * math.exp(-0.3 * layer)
    return pl.pallas_call(
        functools.partial(_diff_attn_kernel, lambda_init=lambda_init),
        grid=(b, HEADS, s // tq),
        in_specs=[
            pl.BlockSpec((4, HEAD_DIM), lambda bi, h, qi: (0, 0)),
            pl.BlockSpec((1, w), lambda bi, h, qi: (0, 0)),
            pl.BlockSpec((1, tq, w), lambda bi, h, qi: (bi, qi, h)),
            pl.BlockSpec((1, s, w), lambda bi, h, qi: (bi, 0, HEADS + h)),
            pl.BlockSpec((1, s, w), lambda bi, h, qi: (bi, 0, 2 * HEADS + h)),
        ],
        out_specs=pl.BlockSpec((1, tq, w), lambda bi, h, qi: (bi, qi, h)),
        out_shape=jax.ShapeDtypeStruct((b, s, C_QK_W), BF16),
        compiler_params=_params("parallel", "parallel", "arbitrary"),
        name="diff_attention",
    )(lam_params, subln.reshape(1, w), proj, proj, proj)


def _na_key_start_row(first_query_row):
    return (first_query_row - NA_ROWS // 2, 0, GRID_W - NA_SUB_K_ROWS)


def _na_bias(rpb):
    rows = GRID_W
    n_sub = NA_Q_ROWS // NA_SUB_Q_ROWS
    n_dr = 2 * NA_ROWS - 1
    c = np.arange(GRID_W)
    col_start = np.clip(c - NA_COLS // 2, 0, GRID_W - NA_COLS)
    col_ok = (c[None, :] >= col_start[:, None]) & (c[None, :] < col_start[:, None] + NA_COLS)
    dc = np.clip(c[None, :] - c[:, None] + NA_COLS - 1, 0, 2 * NA_COLS - 2)
    sel_c = (np.arange(2 * NA_COLS - 1)[:, None, None] == dc[None]).astype(np.float32)
    tables = jnp.einsum("hab,bxy->haxy", rpb, sel_c, precision=lax.Precision.HIGHEST) * math.log2(math.e)
    tables = jnp.where(col_ok[None, None], tables, -jnp.inf)
    masked = jnp.full(tables.shape[:1] + tables.shape[2:], -jnp.inf, tables.dtype)
    blocks = []
    for rb in (0, 1, rows // NA_Q_ROWS - 1):
        for p in range(n_sub):
            r0 = rb * NA_Q_ROWS + p * NA_SUB_Q_ROWS
            base = int(np.clip(*_na_key_start_row(r0)))
            q_rows = []
            for ql in range(NA_SUB_Q_ROWS):
                r = r0 + ql
                row_start = int(np.clip(r - NA_ROWS // 2, 0, rows - NA_ROWS))
                assert base <= row_start and row_start + NA_ROWS <= base + NA_SUB_K_ROWS
                tiles = []
                for kl in range(NA_SUB_K_ROWS):
                    kr = base + kl
                    inside = row_start <= kr < row_start + NA_ROWS
                    assert not inside or 0 <= kr - r + NA_ROWS - 1 < n_dr
                    tiles.append(tables[:, kr - r + NA_ROWS - 1] if inside else masked)
                q_rows.append(jnp.concatenate(tiles, axis=-1))
            blocks.append(jnp.concatenate(q_rows, axis=-2))
    bias = jnp.stack(blocks, axis=1)
    return bias.reshape(rpb.shape[0], 3, n_sub, NA_SUB_Q_ROWS * GRID_W, NA_SUB_K_ROWS * GRID_W)


def _na_kernel(q_ref, k_ref, v_ref, b_ref, o_ref):
    rb = pl.program_id(2)
    nq = NA_SUB_Q_ROWS * GRID_W
    nk = NA_SUB_K_ROWS * GRID_W
    n_sub = NA_Q_ROWS // NA_SUB_Q_ROWS

    def key_start(p):
        row, lo, hi = _na_key_start_row(rb * NA_Q_ROWS + p * NA_SUB_Q_ROWS)
        return pl.multiple_of(jnp.clip(row, lo, hi) * GRID_W, NA_SUB_Q_ROWS * GRID_W)

    def scores(item):
        hl, p = item
        lanes = slice(hl * HEAD_DIM, (hl + 1) * HEAD_DIM)
        q = (q_ref[0, p * nq:(p + 1) * nq, lanes].astype(F32) * (HEAD_DIM ** -0.5 * math.log2(math.e))).astype(BF16)
        return _dot_nt(q, k_ref[0, pl.ds(key_start(p), nk), lanes]) + b_ref[hl, 0, p]

    def weights(s):
        e = jnp.exp2(s - jnp.max(s, axis=-1, keepdims=True))
        return e.astype(BF16), 1.0 / jnp.sum(e, axis=-1, keepdims=True)

    def finish(item, e_inv):
        hl, p = item
        lanes = slice(hl * HEAD_DIM, (hl + 1) * HEAD_DIM)
        e, inv_l = e_inv
        o = _dot(e, v_ref[0, pl.ds(key_start(p), nk), lanes]) * inv_l
        o_ref[0, p * nq:(p + 1) * nq, lanes] = o.astype(o_ref.dtype)

    items = [(hl, p) for hl in range(NA_HEADS_PER_STEP) for p in range(n_sub)]
    s_next = scores(items[0])
    e_prev = None
    for n, item in enumerate(items):
        s_cur = s_next
        if n + 1 < len(items):
            s_next = scores(items[n + 1])
        e_cur = weights(s_cur)
        if e_prev is not None:
            finish(items[n - 1], e_prev)
        e_prev = e_cur
    finish(items[-1], e_prev)


def neighbourhood_attention(proj, col0, bias):
    b, s, _ = proj.shape
    nq = NA_Q_ROWS * GRID_W
    nblk = s // nq
    hs = NA_HEADS_PER_STEP
    w = hs * HEAD_DIM
    c0 = col0 // w
    groups = HEADS // hs

    def bias_map(bi, h, rb):
        return (h, jnp.where(rb == 0, 0, jnp.where(rb == nblk - 1, 2, 1)), 0, 0, 0)

    return pl.pallas_call(
        _na_kernel,
        grid=(b, groups, nblk),
        in_specs=[
            pl.BlockSpec((1, nq, w), lambda bi, h, rb: (bi, rb, c0 + h)),
            pl.BlockSpec((1, s, w), lambda bi, h, rb: (bi, 0, c0 + groups + h)),
            pl.BlockSpec((1, s, w), lambda bi, h, rb: (bi, 0, c0 + 2 * groups + h)),
            pl.BlockSpec((hs, 1) + bias.shape[2:], bias_map),
        ],
        out_specs=pl.BlockSpec((1, nq, w), lambda bi, h, rb: (bi, rb, h)),
        out_shape=jax.ShapeDtypeStruct((b, s, B_W), BF16),
        compiler_params=_params("parallel", "parallel", "arbitrary"),
        name="neighbourhood_attention",
    )(proj, proj, proj, bias)


HALO = SUBLANES_BF16


def _split3(x):
    hi = x.astype(BF16)
    r1 = x - hi.astype(F32)
    mid = r1.astype(BF16)
    lo = (r1 - mid.astype(F32)).astype(BF16)
    return hi, mid, lo


def _dn_prep_kernel(x_ref, xp_ref, xx_ref, cw_ref, gt_ref, alog_ref, dtb_ref,
                    q_ref, k_ref, v_ref, gc_ref, beta_ref, xs_ref):
    i = pl.program_id(1)
    tb = x_ref.shape[1]
    pad = A_CONV_W // 2
    xs_ref[0:HALO, :] = jnp.where(i == 0, 0.0, xp_ref[0].astype(F32))
    xs_ref[HALO:HALO + tb, :] = x_ref[0].astype(F32)
    xs_ref[HALO + tb:, :] = jnp.where(i == pl.num_programs(1) - 1, 0.0, xx_ref[0].astype(F32))
    cw = cw_ref[...]
    y = xs_ref[pl.ds(HALO - pad, tb), :] * cw[0:1]
    for t in range(1, A_CONV_W):
        y = y + xs_ref[pl.ds(HALO - pad + t, tb), :] * cw[t:t + 1]
    y = y * _sigmoid(y)
    for h in range(HEADS):
        qh = y[:, h * HEAD_DIM:(h + 1) * HEAD_DIM]
        kh = y[:, (HEADS + h) * HEAD_DIM:(HEADS + h + 1) * HEAD_DIM]
        qn = qh * (lax.rsqrt(jnp.sum(qh * qh, axis=-1, keepdims=True) + 1e-6) * (HEAD_DIM ** -0.5))
        kn = kh * lax.rsqrt(jnp.sum(kh * kh, axis=-1, keepdims=True) + 1e-6)
        q_ref[0, h] = qn.astype(q_ref.dtype)
        k_ref[0, h] = kn.astype(k_ref.dtype)
        v_ref[0, h] = y[:, (2 * HEADS + h) * HEAD_DIM:(2 * HEADS + h + 1) * HEAD_DIM].astype(v_ref.dtype)

    nd = 2 * HEADS
    gates = gt_ref[0]
    z = gates[:, 0:nd] + dtb_ref[...]
    softplus = jnp.maximum(z, 0.0) + jnp.log(1.0 + jnp.exp(-jnp.abs(z)))
    g = -jnp.exp(alog_ref[...]) * softplus
    beta_ref[0] = _sigmoid(gates[:, nd:2 * nd])
    r = lax.broadcasted_iota(jnp.int32, (tb, tb), 0)
    c = lax.broadcasted_iota(jnp.int32, (tb, tb), 1)
    lower = jnp.where(r >= c, 1.0, 0.0).astype(BF16)
    upper = jnp.where(r <= c, 1.0, 0.0).astype(BF16)
    parts = _split3(g)
    fwd = _dot(lower, parts[0]) + _dot(lower, parts[1]) + _dot(lower, parts[2])
    bwd = _dot(upper, parts[0]) + _dot(upper, parts[1]) + _dot(upper, parts[2])
    col = lax.broadcasted_iota(jnp.int32, (tb, nd), 1)
    gc_ref[0] = jnp.where(col < HEADS, fwd, bwd)


def deltanet_prep(proj, gates, conv_w, a_log, dt_bias, tb=DN_BLOCK):
    b, s, _ = proj.shape
    nb = s // tb
    rh = tb // HALO
    nd = 2 * HEADS
    hs = jax.ShapeDtypeStruct((b, HEADS, s, HEAD_DIM), BF16)
    head_spec = pl.BlockSpec((1, HEADS, tb, HEAD_DIM), lambda bi, i: (bi, 0, i, 0))
    vec_spec = pl.BlockSpec((1, tb, nd), lambda bi, i: (bi, i, 0))
    return pl.pallas_call(
        _dn_prep_kernel,
        grid=(b, nb),
        in_specs=[
            pl.BlockSpec((1, tb, A_QKV_W), lambda bi, i: (bi, i, 0)),
            pl.BlockSpec((1, HALO, A_QKV_W), lambda bi, i: (bi, jnp.maximum(i * rh - 1, 0), 0)),
            pl.BlockSpec((1, HALO, A_QKV_W), lambda bi, i: (bi, jnp.minimum((i + 1) * rh, s // HALO - 1), 0)),
            pl.BlockSpec((A_CONV_W, A_QKV_W), lambda bi, i: (0, 0)),
            pl.BlockSpec((1, tb, LANES), lambda bi, i: (bi, i, 0)),
            pl.BlockSpec((1, nd), lambda bi, i: (0, 0)),
            pl.BlockSpec((1, nd), lambda bi, i: (0, 0)),
        ],
        out_specs=[head_spec, head_spec, head_spec, vec_spec, vec_spec],
        out_shape=[hs, hs, hs, jax.ShapeDtypeStruct((b, s, nd), F32), jax.ShapeDtypeStruct((b, s, nd), F32)],
        scratch_shapes=[pltpu.VMEM((tb + 2 * HALO, A_QKV_W), F32)],
        compiler_params=_params("parallel", "arbitrary"),
        name="deltanet_prep",
    )(proj, proj, proj, conv_w, gates, a_log.reshape(1, nd), dt_bias.reshape(1, nd))


def _split2(x):
    hi = x.astype(BF16)
    return hi, (x - hi.astype(F32)).astype(BF16)


def _unit_triangular_solve(ms, rhss, same_block):
    n = ms[0].shape[0]
    grp = DN_GROUP
    ng = n // grp
    r = lax.broadcasted_iota(jnp.int32, (grp, grp), 0)
    c = lax.broadcasted_iota(jnp.int32, (grp, grp), 1)
    base_mask = (r // DN_BASE) == (c // DN_BASE)
    row = lax.broadcasted_iota(jnp.int32, (DN_BASE, grp), 0)
    lane = lax.broadcasted_iota(jnp.int32, (DN_BASE, grp), 1)
    left = lane < DN_BASE

    def pair_lhs(hi, lo):
        hi_f, lo_f = hi.astype(F32), lo.astype(F32)
        swapped = pltpu.roll(hi_f, DN_BASE, 1)
        top = jnp.concatenate([jnp.where(left, hi_f, swapped), jnp.where(left, lo_f, 0.0)], axis=1)
        bottom = jnp.concatenate([jnp.where(left, swapped, hi_f), jnp.where(left, 0.0, lo_f)], axis=1)
        return jnp.concatenate([top, bottom], axis=0).astype(BF16)

    def pair_product(lhs, rhs_hi, rhs_lo):
        res = _dot(lhs, jnp.concatenate([rhs_hi, rhs_lo, rhs_hi, rhs_hi], axis=0))
        return jnp.where(left, res[:DN_BASE], res[DN_BASE:])

    pieces = [jnp.where(base_mask, m[g * grp:(g + 1) * grp, g * grp:(g + 1) * grp], 0.0)
              for m in ms for g in range(ng)]
    pws = [piece[:DN_BASE] + piece[DN_BASE:] for piece in pieces]
    ts = [jnp.where(lane % DN_BASE == row, 1.0, 0.0) + pw for pw in pws]
    splits = [_split2(pw) for pw in pws]
    lhss = [pair_lhs(hi, lo) for hi, lo in splits]
    size = 2
    while size < DN_BASE:
        pws = [pair_product(lhs, hi, lo) for lhs, (hi, lo) in zip(lhss, splits)]
        splits = [_split2(pw) for pw in pws]
        lhss = [pair_lhs(hi, lo) for hi, lo in splits]
        ts = [t + pair_product(lhs, *_split2(t)) for lhs, t in zip(lhss, ts)]
        size *= 2
    t_bases = [jnp.concatenate([jnp.where(left, t, 0.0), jnp.where(left, 0.0, t)], axis=0).astype(BF16)
               for t in ts]
    offs = {}
    blk = 2 * DN_BASE
    while blk <= n:
        offs[blk] = [jnp.where(same_block(blk), jnp.where(same_block(blk // 2), 0.0, m), 0.0).astype(BF16)
                     for m in ms]
        blk *= 2

    def apply(blk, ys):
        if blk == DN_BASE:
            ybs = [y.astype(BF16) for y in ys]
            return [jnp.concatenate([_dot(t_bases[i * ng + g], yb[g * grp:(g + 1) * grp]) for g in range(ng)],
                                    axis=0) for i, yb in enumerate(ybs)]
        zs = apply(blk // 2, ys)
        corr = apply(blk // 2, [_dot(off, z.astype(BF16)) for off, z in zip(offs[blk], zs)])
        return [z + cr for z, cr in zip(zs, corr)]

    return apply(n, rhss)


def _dn_block(chains):
    n = chains[0][0].shape[0]
    r = lax.broadcasted_iota(jnp.int32, (n, n), 0)
    c = lax.broadcasted_iota(jnp.int32, (n, n), 1)

    def same_block(size):
        return (r // size) == (c // size)

    ms, rhss, pre = [], [], []
    for q, k, v, gcol, grow, beta, state_ref, forward in chains:
        d = (r - c) if forward else (c - r)
        kf = k.astype(F32)
        kb = kf * beta
        decay = jnp.exp(jnp.where(d >= 0, gcol - grow, NEG_BIG))
        ms.append(jnp.where(d > 0, -(_dot_nt(kb.astype(BF16), k) * decay), 0.0))
        intra = (_dot_nt(q, k) * decay).astype(BF16)
        eg = jnp.exp(gcol)
        rhss.append(jnp.concatenate([v.astype(F32) * beta, kb * eg], axis=1))
        g_last = grow[:, n - 1:n] if forward else grow[:, 0:1]
        pre.append((intra, (q.astype(F32) * eg).astype(BF16), (kf * jnp.exp(g_last - gcol)).astype(BF16),
                    jnp.exp(g_last)))
    sols = _unit_triangular_solve(ms, rhss, same_block)
    outs = []
    for chain, sol, (intra, q_dec, k_dec, blk_decay) in zip(chains, sols, pre):
        state_ref = chain[6]
        state = state_ref[...]
        sb = state.astype(BF16)
        v_new = sol[:, :HEAD_DIM] - _dot(sol[:, HEAD_DIM:].astype(BF16), sb)
        vb = v_new.astype(BF16)
        outs.append(_dot(q_dec, sb) + _dot(intra, vb))
        state_ref[...] = state * blk_decay + _dot_tn(k_dec, vb)
    return outs


def _dn_kernel(qf_ref, kf_ref, vf_ref, qb_ref, kb_ref, vb_ref, gcf_ref, gcb_ref, btf_ref, btb_ref,
               grf_ref, grb_ref, of_ref, ob_ref, state_ref):
    hb = qf_ref.shape[1]

    @pl.when(pl.program_id(2) == 0)
    def _():
        state_ref[...] = jnp.zeros_like(state_ref)

    lane = lax.broadcasted_iota(jnp.int32, gcf_ref.shape[1:], 1)

    def pick(ref, idx):
        return jnp.sum(jnp.where(lane == idx, ref[0], 0.0), axis=-1, keepdims=True)

    chains = []
    for hl in range(hb):
        h = pl.program_id(1) * hb + hl
        chains.append((qf_ref[0, hl], kf_ref[0, hl], vf_ref[0, hl], pick(gcf_ref, h),
                       grf_ref[0, pl.ds(h, 1), :], pick(btf_ref, h), state_ref.at[0, hl], True))
        chains.append((qb_ref[0, hl], kb_ref[0, hl], vb_ref[0, hl], pick(gcb_ref, HEADS + h),
                       grb_ref[0, pl.ds(HEADS + h, 1), :], pick(btb_ref, HEADS + h), state_ref.at[1, hl], False))
    outs = _dn_block(chains)
    for hl in range(hb):
        of_ref[0, hl] = outs[2 * hl]
        ob_ref[0, hl] = outs[2 * hl + 1]


def deltanet_scan(q, k, v, gc, beta, gc_rows, tb=DN_BLOCK, hb=DN_HEADS_PER_STEP):
    b, _, s, _ = q.shape
    nb = s // tb
    nd = 2 * HEADS
    fwd = pl.BlockSpec((1, hb, tb, HEAD_DIM), lambda bi, h, c: (bi, h, c, 0))
    bwd = pl.BlockSpec((1, hb, tb, HEAD_DIM), lambda bi, h, c: (bi, h, nb - 1 - c, 0))
    vec_f = pl.BlockSpec((1, tb, nd), lambda bi, h, c: (bi, c, 0))
    vec_b = pl.BlockSpec((1, tb, nd), lambda bi, h, c: (bi, nb - 1 - c, 0))
    row_f = pl.BlockSpec((1, nd, tb), lambda bi, h, c: (bi, 0, c))
    row_b = pl.BlockSpec((1, nd, tb), lambda bi, h, c: (bi, 0, nb - 1 - c))
    os_ = jax.ShapeDtypeStruct((b, HEADS, s, HEAD_DIM), F32)
    return pl.pallas_call(
        _dn_kernel,
        grid=(b, HEADS // hb, nb),
        in_specs=[fwd, fwd, fwd, bwd, bwd, bwd, vec_f, vec_b, vec_f, vec_b, row_f, row_b],
        out_specs=[fwd, bwd],
        out_shape=[os_, os_],
        scratch_shapes=[pltpu.VMEM((2, hb, HEAD_DIM, HEAD_DIM), F32)],
        compiler_params=_params("parallel", "parallel", "arbitrary"),
        name="deltanet_scan",
    )(q, k, v, q, k, v, gc, gc, beta, beta, gc_rows, gc_rows)


def _dn_out_kernel(of_ref, ob_ref, z_ref, nw_ref, o_ref):
    nw = nw_ref[...]
    for h in range(HEADS):
        o = _rms(of_ref[0, h] + ob_ref[0, h], nw)
        z = z_ref[0, :, h * HEAD_DIM:(h + 1) * HEAD_DIM].astype(F32)
        o_ref[0, :, h * HEAD_DIM:(h + 1) * HEAD_DIM] = (o * (z * _sigmoid(z))).astype(o_ref.dtype)


def deltanet_out(o_f, o_b, proj, z_col0, out_norm, ts=ROW_TILE):
    b, _, s, _ = o_f.shape
    head_spec = pl.BlockSpec((1, HEADS, ts, HEAD_DIM), lambda bi, i: (bi, 0, i, 0))
    zb = z_col0 // A_V_W
    return pl.pallas_call(
        _dn_out_kernel,
        grid=(b, s // ts),
        in_specs=[head_spec, head_spec,
                  pl.BlockSpec((1, ts, A_V_W), lambda bi, i: (bi, i, zb)),
                  pl.BlockSpec((1, HEAD_DIM), lambda bi, i: (0, 0))],
        out_specs=pl.BlockSpec((1, ts, A_V_W), lambda bi, i: (bi, i, 0)),
        out_shape=jax.ShapeDtypeStruct((b, s, A_V_W), BF16),
        compiler_params=_params("parallel", "parallel"),
        name="deltanet_out",
    )(o_f, o_b, proj, out_norm.reshape(1, HEAD_DIM))


def _rope_tables(s):
    half = ROPE_DIMS // 2
    inv = ROPE_THETA ** (-jnp.arange(0, ROPE_DIMS, 2, dtype=F32) / ROPE_DIMS)
    ang = jnp.arange(s, dtype=F32)[:, None] * inv[None, :]
    cos, sin = jnp.cos(ang), jnp.sin(ang)
    rest = HEAD_DIM - ROPE_DIMS
    c = jnp.concatenate([cos, cos, jnp.ones((s, rest), F32)], axis=1)
    s1 = jnp.concatenate([jnp.zeros((s, half), F32), sin, jnp.zeros((s, rest), F32)], axis=1)
    s2 = jnp.concatenate([-sin, jnp.zeros((s, half + rest), F32)], axis=1)
    return c, s1, s2


def _trunk(x_a, x_b, p_a, p_b, ab_w_in, ab_conv_w, ab_a_log, ab_dt_bias, ab_out_norm, ab_rpb, ab_w_out,
           c_w_in, c_lambda, c_subln, c_w_out, norms, ffn_w_in, ffn_conv_w, ffn_conv_b,
           ffn_w_out, ple_w_proj, ple_w_gate):
    b_a, s, d = x_a.shape
    b = b_a + x_b.shape[0]
    t, t_a = b * s, b_a * s
    depth = norms.shape[0]
    depth_p = p_a.shape[0]
    h = jnp.concatenate([x_a, x_b], axis=0).reshape(t, d)
    rope = _rope_tables(s)
    o1 = A_QKV_W
    o2 = o1 + A_V_W
    o3 = o2 + 4 * HEADS
    ab_main = jnp.concatenate([ab_w_in[:, :, :o2], ab_w_in[:, :, o3:]], axis=2).astype(BF16)
    ab_gate = jnp.pad(ab_w_in[:, :, o2:o3], ((0, 0), (0, 0), (0, LANES - 4 * HEADS))).astype(BF16)
    c_in = c_w_in.astype(BF16)
    ffn_in = ffn_w_in.astype(BF16)
    ab_out, c_out, ffn_out = ab_w_out.astype(BF16), c_w_out.astype(BF16), ffn_w_out.astype(BF16)
    ple_gate, ple_proj = ple_w_gate.astype(BF16), ple_w_proj.astype(BF16)
    for layer in range(depth):
        j = layer // 2
        if layer % 2 == 0:
            proj, gates = norm_matmul(h, norms[layer, 0], ab_main, j, BF16, side_w=ab_gate)
            proj, gates = proj.reshape(b, s, -1), gates.reshape(b, s, LANES)
            qa, ka, va, gc, beta = deltanet_prep(proj, gates, ab_conv_w[j], ab_a_log[j], ab_dt_bias[j])
            o_f, o_b = deltanet_scan(qa, ka, va, gc, beta, jnp.transpose(gc, (0, 2, 1)))
            o_a = deltanet_out(o_f, o_b, proj, o1, ab_out_norm[j])
            o_nb = neighbourhood_attention(proj, o2, _na_bias(ab_rpb[j]))
            h, xn = proj_norm_res([o_a.reshape(t, -1), o_nb.reshape(t, -1)], ab_out, j, norms[layer, 1],
                                  norms[layer, 2], h)
        else:
            proj = norm_matmul(h, norms[layer, 0], c_in, j, BF16, rope=rope,
                               rope_tiles=2 * C_QK_W // COL_TILE, seq_len=s).reshape(b, s, -1)
            o_c = diff_attention(proj, c_lambda[j], c_subln[j], layer)
            h, xn = proj_norm_res([o_c.reshape(t, -1)], c_out, j, norms[layer, 1], norms[layer, 2], h)
        f = conv_ffn(xn, ffn_in, ffn_out, layer, ffn_conv_w[layer], ffn_conv_b[layer], norms[layer, 3], s)
        ple_args = (f, p_a.reshape(depth_p, t_a, -1), p_b.reshape(depth_p, t - t_a, -1), ple_gate, ple_proj, layer)
        if layer + 1 < depth:
            h = ple(h, *ple_args)
    y_a = ple(h, *ple_args, row0=0, rows=t_a)
    y_b = ple(h, *ple_args, row0=t_a, rows=t - t_a)
    return y_a.reshape(b_a, s, d), y_b.reshape(b - b_a, s, d)


def kernel(x_prompt, x_sample, p_prompt, p_sample, ab_w_in, ab_conv_w, ab_a_log, ab_dt_bias, ab_out_norm,
           ab_rpb, ab_w_out, c_w_in, c_lambda, c_subln, c_w_out, norms, ffn_w_in, ffn_conv_w, ffn_conv_b,
           ffn_w_out, ple_w_proj, ple_w_gate):
    return _trunk(x_prompt, x_sample, p_prompt, p_sample, ab_w_in, ab_conv_w, ab_a_log, ab_dt_bias,
                  ab_out_norm, ab_rpb, ab_w_out, c_w_in, c_lambda, c_subln, c_w_out, norms, ffn_w_in,
                  ffn_conv_w, ffn_conv_b, ffn_w_out, ple_w_proj, ple_w_gate)
```

```python
import functools
import math

import jax
import jax.numpy as jnp
import numpy as np
from jax import lax
from jax.experimental import pallas as pl
from jax.experimental.pallas import tpu as pltpu

F32 = jnp.float32
BF16 = jnp.bfloat16

GRID_W = 64
RMS_EPS = 1e-6
HEADS = 8
HEAD_DIM = 128
A_CONV_W = 5
NA_ROWS = 8
NA_COLS = 16
ROPE_THETA = 500000.0
ROPE_DIMS = HEAD_DIM // 4
FFN_CONV_W = 3
A_QKV_W = 3 * HEADS * HEAD_DIM
A_V_W = HEADS * HEAD_DIM
B_W = HEADS * HEAD_DIM
C_QK_W = HEADS * 2 * HEAD_DIM

LANES = 128
SUBLANES_BF16 = 16
VMEM_LIMIT = 56 * 1024 * 1024
NEG_BIG = -1e30

ROW_TILE = 512
MM_ROW_TILE = 1024
COL_TILE = 1024
FF_TILE = 512
FFN_ROW_TILE = 1024
FFN_ROW_PIECE = 256
DN_BLOCK = 256
DN_BASE = 64
DN_GROUP = 128
DN_HEADS_PER_STEP = 4
ATT_Q_TILE = 1024
ATT_SUB_ROWS = 256
NA_Q_ROWS = 8
NA_HEADS_PER_STEP = 4
NA_SUB_Q_ROWS = 4
NA_SUB_K_ROWS = 12


def _params(*sem):
    return pltpu.CompilerParams(dimension_semantics=sem, vmem_limit_bytes=VMEM_LIMIT)


def _rms(x, w):
    return x * lax.rsqrt(jnp.mean(x * x, axis=-1, keepdims=True) + RMS_EPS) * w


def _dot(a, b):
    return jnp.dot(a, b, preferred_element_type=F32)


def _dot_nt(a, b):
    return lax.dot_general(a, b, (((1,), (1,)), ((), ())), preferred_element_type=F32)


def _dot_tn(a, b):
    return lax.dot_general(a, b, (((0,), (0,)), ((), ())), preferred_element_type=F32)


def _sigmoid(x):
    return 1.0 / (1.0 + jnp.exp(-x))


def _norm_matmul_kernel(*refs, rope_tiles, side):
    x_ref, nw_ref, w_ref = refs[:3]
    rest = list(refs[3:])
    side_w_ref = rest.pop(0) if side else None
    if rope_tiles:
        rc_ref, rs1_ref, rs2_ref = rest[:3]
        rest = rest[3:]
    o_ref = rest.pop(0)
    side_o_ref = rest.pop(0) if side else None
    xn_ref = rest.pop(0)
    j = pl.program_id(1)

    @pl.when(j == 0)
    def _():
        xn = _rms(x_ref[...], nw_ref[...]).astype(BF16)
        xn_ref[...] = xn
        if side:
            side_o_ref[...] = _dot(xn, side_w_ref[0])

    y = _dot(xn_ref[...], w_ref[0])
    if not rope_tiles:
        o_ref[...] = y.astype(o_ref.dtype)
        return

    @pl.when(j < rope_tiles)
    def _():
        n = y.shape[1]
        reps = n // LANES
        c = jnp.concatenate([rc_ref[...]] * reps, axis=1)
        s1 = jnp.concatenate([rs1_ref[...]] * reps, axis=1)
        s2 = jnp.concatenate([rs2_ref[...]] * reps, axis=1)
        half = ROPE_DIMS // 2
        r = y * c + pltpu.roll(y, half, 1) * s1 + pltpu.roll(y, n - half, 1) * s2
        o_ref[...] = r.astype(o_ref.dtype)

    @pl.when(j >= rope_tiles)
    def _():
        o_ref[...] = y.astype(o_ref.dtype)


def norm_matmul(x, nw, w, layer, out_dtype, rope=None, rope_tiles=0, seq_len=None, side_w=None,
                tm=MM_ROW_TILE, tn=COL_TILE):
    t, k = x.shape
    n = w.shape[2]
    tm, tn = min(tm, t), min(tn, n)
    assert t % tm == 0 and n % tn == 0
    side = side_w is not None
    in_specs = [
        pl.BlockSpec((tm, k), lambda i, j: (i, 0)),
        pl.BlockSpec((1, k), lambda i, j: (0, 0)),
        pl.BlockSpec((1, k, tn), lambda i, j: (layer, 0, j)),
    ]
    args = [x, nw.reshape(1, k), w]
    out_specs = [pl.BlockSpec((tm, tn), lambda i, j: (i, j))]
    out_shape = [jax.ShapeDtypeStruct((t, n), out_dtype)]
    if side:
        m = side_w.shape[2]
        in_specs.append(pl.BlockSpec((1, k, m), lambda i, j: (layer, 0, 0)))
        args.append(side_w)
        out_specs.append(pl.BlockSpec((tm, m), lambda i, j: (i, 0)))
        out_shape.append(jax.ShapeDtypeStruct((t, m), F32))
    if rope_tiles:
        per_seq = seq_len // tm
        for tab in rope:
            in_specs.append(pl.BlockSpec((tm, LANES), lambda i, j: (i % per_seq, 0)))
            args.append(tab)
    outs = pl.pallas_call(
        functools.partial(_norm_matmul_kernel, rope_tiles=rope_tiles, side=side),
        grid=(t // tm, n // tn),
        in_specs=in_specs,
        out_specs=out_specs,
        out_shape=out_shape,
        scratch_shapes=[pltpu.VMEM((tm, k), BF16)],
        compiler_params=_params("parallel", "arbitrary"),
        name="norm_matmul",
    )(*args)
    return tuple(outs) if side else outs[0]


def _proj_norm_res_kernel(*refs, n_in):
    a_refs = refs[:n_in]
    w_refs = refs[n_in:2 * n_in]
    nw_ref, nw_next_ref, h_ref, o_ref, xn_ref = refs[2 * n_in:]
    y = _dot(a_refs[0][...], w_refs[0][0])
    for a_ref, w_ref in zip(a_refs[1:], w_refs[1:]):
        y = y + _dot(a_ref[...], w_ref[0])
    h_new = h_ref[...] + _rms(y, nw_ref[...])
    o_ref[...] = h_new
    xn_ref[...] = _rms(h_new, nw_next_ref[...]).astype(BF16)


def proj_norm_res(acts, w, layer, nw, nw_next, h, tm=ROW_TILE):
    t, n = h.shape
    n_in = len(acts)
    kw = acts[0].shape[1]
    assert all(a.shape[1] == kw for a in acts) and w.shape[1] == n_in * kw
    in_specs = [pl.BlockSpec((tm, kw), lambda i: (i, 0)) for _ in acts]
    in_specs += [pl.BlockSpec((1, kw, n), functools.partial(lambda piece, i: (layer, piece, 0), piece))
                 for piece in range(n_in)]
    in_specs += [pl.BlockSpec((1, n), lambda i: (0, 0)), pl.BlockSpec((1, n), lambda i: (0, 0)),
                 pl.BlockSpec((tm, n), lambda i: (i, 0))]
    ws = [w] * n_in
    row_spec = pl.BlockSpec((tm, n), lambda i: (i, 0))
    return pl.pallas_call(
        functools.partial(_proj_norm_res_kernel, n_in=n_in),
        grid=(t // tm,),
        in_specs=in_specs,
        out_specs=[row_spec, row_spec],
        out_shape=[jax.ShapeDtypeStruct((t, n), F32), jax.ShapeDtypeStruct((t, n), BF16)],
        compiler_params=_params("parallel"),
        name="proj_norm_res",
    )(*acts, *ws, nw.reshape(1, n), nw_next.reshape(1, n), h)


def _ple_kernel(h_ref, f_ref, pa_ref, pb_ref, wg_ref, wp_ref, o_ref, *, first_block, a_blocks):
    h = h_ref[...] + f_ref[...]
    gate = _sigmoid(_dot(h.astype(BF16), wg_ref[0]))
    p = jnp.where(first_block + pl.program_id(0) < a_blocks, pa_ref[0], pb_ref[0])
    o_ref[...] = h + gate * _dot(p.astype(BF16), wp_ref[0])


def ple(h, f, p_a, p_b, wg, wp, layer, row0=0, rows=None, tm=ROW_TILE):
    n = h.shape[1]
    rows = h.shape[0] if rows is None else rows
    assert row0 % tm == 0 and rows % tm == 0 and p_a.shape[1] % tm == 0 and p_b.shape[1] % tm == 0
    i0 = row0 // tm
    na, nb = p_a.shape[1] // tm, p_b.shape[1] // tm
    return pl.pallas_call(
        functools.partial(_ple_kernel, first_block=i0, a_blocks=na),
        grid=(rows // tm,),
        in_specs=[
            pl.BlockSpec((tm, n), lambda i: (i0 + i, 0)),
            pl.BlockSpec((tm, n), lambda i: (i0 + i, 0)),
            pl.BlockSpec((1, tm, p_a.shape[2]), lambda i: (layer, jnp.minimum(i0 + i, na - 1), 0)),
            pl.BlockSpec((1, tm, p_b.shape[2]), lambda i: (layer, jnp.clip(i0 + i - na, 0, nb - 1), 0)),
            pl.BlockSpec((1,) + wg.shape[1:], lambda i: (layer, 0, 0)),
            pl.BlockSpec((1,) + wp.shape[1:], lambda i: (layer, 0, 0)),
        ],
        out_specs=pl.BlockSpec((tm, n), lambda i: (i, 0)),
        out_shape=jax.ShapeDtypeStruct((rows, n), F32),
        compiler_params=_params("parallel"),
        name="ple",
    )(h, f, p_a, p_b, wg, wp)


def _ffn_kernel(x_ref, xp_ref, xx_ref, wg_ref, wu_ref, cwg_ref, cwu_ref, cbg_ref, cbu_ref,
                wo_ref, nw2_ref, o_ref, xn_ref, hg_ref, hu_ref, *, tm, per_seq):
    i = pl.program_id(0)
    j = pl.program_id(1)

    @pl.when(j == 0)
    def _():
        pos = i % per_seq
        half = SUBLANES_BF16 // 2
        prev = jnp.where(pos == 0, 0.0, xp_ref[...].astype(F32))[half:]
        nxt = jnp.where(pos == per_seq - 1, 0.0, xx_ref[...].astype(F32))[:half]
        xn_ref[0:tm, :] = x_ref[...]
        xn_ref[tm:, :] = jnp.concatenate([nxt, prev], axis=0).astype(BF16)
        o_ref[...] = jnp.zeros_like(o_ref)

    xe = xn_ref[...]

    for w_ref, h_ref in ((wg_ref, hg_ref), (wu_ref, hu_ref)):
        hid = _dot(xe, w_ref[0])
        h_ref[8:, :] = hid
        h_ref[0:8, :] = hid[tm + 8:, :]

    def conv(h_ref, cw_ref, cb_ref, r0, rows):
        cw = cw_ref[...]
        return (h_ref[pl.ds(7 + r0, rows), :] * cw[0:1] + h_ref[pl.ds(8 + r0, rows), :] * cw[1:2]
                + h_ref[pl.ds(9 + r0, rows), :] * cw[2:3] + cb_ref[...])

    rows = min(FFN_ROW_PIECE, tm)
    for r0 in range(0, tm, rows):
        g = conv(hg_ref, cwg_ref, cbg_ref, r0, rows)
        u = conv(hu_ref, cwu_ref, cbu_ref, r0, rows)
        c0 = math.sqrt(2.0 / math.pi)
        gelu = 0.5 * g * (1.0 + jnp.tanh(c0 * (g + 0.044715 * (g * g * g))))
        o_ref[r0:r0 + rows, :] += _dot((gelu * u).astype(BF16), wo_ref[0])

    @pl.when(j == pl.num_programs(1) - 1)
    def _():
        o_ref[...] = _rms(o_ref[...], nw2_ref[...])


def conv_ffn(xn, w_in, w_out, layer, conv_w, conv_b, nw_out, seq_len, tm=FFN_ROW_TILE, tf=FF_TILE):
    t, d = xn.shape
    f = w_out.shape[1]
    tf = min(tf, f)
    nf = f // tf
    per_seq = seq_len // tm
    rg = tm // SUBLANES_BF16
    last_group = t // SUBLANES_BF16 - 1
    cb = conv_b.reshape(1, 2 * f)
    return pl.pallas_call(
        functools.partial(_ffn_kernel, tm=tm, per_seq=per_seq),
        grid=(t // tm, nf),
        in_specs=[
            pl.BlockSpec((tm, d), lambda i, j: (i, 0)),
            pl.BlockSpec((SUBLANES_BF16, d), lambda i, j: (jnp.maximum(i * rg - 1, 0), 0)),
            pl.BlockSpec((SUBLANES_BF16, d), lambda i, j: (jnp.minimum((i + 1) * rg, last_group), 0)),
            pl.BlockSpec((1, d, tf), lambda i, j: (layer, 0, j)),
            pl.BlockSpec((1, d, tf), lambda i, j: (layer, 0, j + nf)),
            pl.BlockSpec((FFN_CONV_W, tf), lambda i, j: (0, j)),
            pl.BlockSpec((FFN_CONV_W, tf), lambda i, j: (0, j + nf)),
            pl.BlockSpec((1, tf), lambda i, j: (0, j)),
            pl.BlockSpec((1, tf), lambda i, j: (0, j + nf)),
            pl.BlockSpec((1, tf, d), lambda i, j: (layer, j, 0)),
            pl.BlockSpec((1, d), lambda i, j: (0, 0)),
        ],
        out_specs=pl.BlockSpec((tm, d), lambda i, j: (i, 0)),
        out_shape=jax.ShapeDtypeStruct((t, d), F32),
        scratch_shapes=[
            pltpu.VMEM((tm + SUBLANES_BF16, d), BF16),
            pltpu.VMEM((8 + tm + SUBLANES_BF16, tf), F32),
            pltpu.VMEM((8 + tm + SUBLANES_BF16, tf), F32),
        ],
        compiler_params=_params("parallel", "arbitrary"),
        name="conv_ffn",
    )(xn, xn, xn, w_in, w_in, conv_w, conv_w, cb, cb, w_out, nw_out.reshape(1, d))


def _diff_attn_kernel(lp_ref, sub_ref, q_ref, k_ref, v_ref, o_ref, *, lambda_init):
    lp = lp_ref[...]
    lam = (jnp.exp(jnp.sum(lp[0:1] * lp[1:2], axis=-1, keepdims=True))
           - jnp.exp(jnp.sum(lp[2:3] * lp[3:4], axis=-1, keepdims=True)) + lambda_init)
    sub = sub_ref[...]
    n_sub = q_ref.shape[1] // ATT_SUB_ROWS

    def scores(i):
        q = q_ref[0, i * ATT_SUB_ROWS:(i + 1) * ATT_SUB_ROWS, :].astype(F32)
        q = (q * (HEAD_DIM ** -0.5 * math.log2(math.e))).astype(BF16)
        return [_dot_nt(q[:, lo:lo + HEAD_DIM], k_ref[0, :, lo:lo + HEAD_DIM]) for lo in (0, HEAD_DIM)]

    def weights(s12):
        es = [jnp.exp2(s - jnp.max(s, axis=-1, keepdims=True)) for s in s12]
        l1, l2 = [jnp.sum(e, axis=-1, keepdims=True) for e in es]
        return (es[0] - es[1] * (lam * l1 / l2)).astype(BF16), 1.0 / l1

    def finish(i, a_inv):
        a, inv_l1 = a_inv
        o = _dot(a, v_ref[0]) * inv_l1
        o_ref[0, i * ATT_SUB_ROWS:(i + 1) * ATT_SUB_ROWS, :] = (_rms(o, sub) * (1.0 - lambda_init)).astype(o_ref.dtype)

    s_next = scores(0)
    a_prev = None
    for i in range(n_sub):
        s_cur = s_next
        if i + 1 < n_sub:
            s_next = scores(i + 1)
        a_cur = weights(s_cur)
        if a_prev is not None:
            finish(i - 1, a_prev)
        a_prev = a_cur
    finish(n_sub - 1, a_prev)


def diff_attention(proj, lam_params, subln, layer, tq=ATT_Q_TILE):
    b, s, _ = proj.shape
    w = 2 * HEAD_DIM
    tq = min(tq, s)
    lambda_init = 0.8 - 0.6 * math.exp(-0.3 * layer)
    return pl.pallas_call(
        functools.partial(_diff_attn_kernel, lambda_init=lambda_init),
        grid=(b, HEADS, s // tq),
        in_specs=[
            pl.BlockSpec((4, HEAD_DIM), lambda bi, h, qi: (0, 0)),
            pl.BlockSpec((1, w), lambda bi, h, qi: (0, 0)),
            pl.BlockSpec((1, tq, w), lambda bi, h, qi: (bi, qi, h)),
            pl.BlockSpec((1, s, w), lambda bi, h, qi: (bi, 0, HEADS + h)),
            pl.BlockSpec((1, s, w), lambda bi, h, qi: (bi, 0, 2 * HEADS + h)),
        ],
        out_specs=pl.BlockSpec((1, tq, w), lambda bi, h, qi: (bi, qi, h)),
        out_shape=jax.ShapeDtypeStruct((b, s, C_QK_W), BF16),
        compiler_params=_params("parallel", "parallel", "arbitrary"),
        name="diff_attention",
    )(lam_params, subln.reshape(1, w), proj, proj, proj)


def _na_key_start_row(first_query_row):
    return (first_query_row - NA_ROWS // 2, 0, GRID_W - NA_SUB_K_ROWS)


def _na_bias(rpb):
    rows = GRID_W
    n_sub = NA_Q_ROWS // NA_SUB_Q_ROWS
    n_dr = 2 * NA_ROWS - 1
    c = np.arange(GRID_W)
    col_start = np.clip(c - NA_COLS // 2, 0, GRID_W - NA_COLS)
    col_ok = (c[None, :] >= col_start[:, None]) & (c[None, :] < col_start[:, None] + NA_COLS)
    dc = np.clip(c[None, :] - c[:, None] + NA_COLS - 1, 0, 2 * NA_COLS - 2)
    sel_c = (np.arange(2 * NA_COLS - 1)[:, None, None] == dc[None]).astype(np.float32)
    tables = jnp.einsum("hab,bxy->haxy", rpb, sel_c, precision=lax.Precision.HIGHEST) * math.log2(math.e)
    tables = jnp.where(col_ok[None, None], tables, -jnp.inf)
    masked = jnp.full(tables.shape[:1] + tables.shape[2:], -jnp.inf, tables.dtype)
    blocks = []
    for rb in (0, 1, rows // NA_Q_ROWS - 1):
        for p in range(n_sub):
            r0 = rb * NA_Q_ROWS + p * NA_SUB_Q_ROWS
            base = int(np.clip(*_na_key_start_row(r0)))
            q_rows = []
            for ql in range(NA_SUB_Q_ROWS):
                r = r0 + ql
                row_start = int(np.clip(r - NA_ROWS // 2, 0, rows - NA_ROWS))
                assert base <= row_start and row_start + NA_ROWS <= base + NA_SUB_K_ROWS
                tiles = []
                for kl in range(NA_SUB_K_ROWS):
                    kr = base + kl
                    inside = row_start <= kr < row_start + NA_ROWS
                    assert not inside or 0 <= kr - r + NA_ROWS - 1 < n_dr
                    tiles.append(tables[:, kr - r + NA_ROWS - 1] if inside else masked)
                q_rows.append(jnp.concatenate(tiles, axis=-1))
            blocks.append(jnp.concatenate(q_rows, axis=-2))
    bias = jnp.stack(blocks, axis=1)
    return bias.reshape(rpb.shape[0], 3, n_sub, NA_SUB_Q_ROWS * GRID_W, NA_SUB_K_ROWS * GRID_W)


def _na_kernel(q_ref, k_ref, v_ref, b_ref, o_ref):
    rb = pl.program_id(2)
    nq = NA_SUB_Q_ROWS * GRID_W
    nk = NA_SUB_K_ROWS * GRID_W
    n_sub = NA_Q_ROWS // NA_SUB_Q_ROWS

    def key_start(p):
        row, lo, hi = _na_key_start_row(rb * NA_Q_ROWS + p * NA_SUB_Q_ROWS)
        return pl.multiple_of(jnp.clip(row, lo, hi) * GRID_W, NA_SUB_Q_ROWS * GRID_W)

    def scores(item):
        hl, p = item
        lanes = slice(hl * HEAD_DIM, (hl + 1) * HEAD_DIM)
        q = (q_ref[0, p * nq:(p + 1) * nq, lanes].astype(F32) * (HEAD_DIM ** -0.5 * math.log2(math.e))).astype(BF16)
        return _dot_nt(q, k_ref[0, pl.ds(key_start(p), nk), lanes]) + b_ref[hl, 0, p]

    def weights(s):
        e = jnp.exp2(s - jnp.max(s, axis=-1, keepdims=True))
        return e.astype(BF16), 1.0 / jnp.sum(e, axis=-1, keepdims=True)

    def finish(item, e_inv):
        hl, p = item
        lanes = slice(hl * HEAD_DIM, (hl + 1) * HEAD_DIM)
        e, inv_l = e_inv
        o = _dot(e, v_ref[0, pl.ds(key_start(p), nk), lanes]) * inv_l
        o_ref[0, p * nq:(p + 1) * nq, lanes] = o.astype(o_ref.dtype)

    items = [(hl, p) for hl in range(NA_HEADS_PER_STEP) for p in range(n_sub)]
    s_next = scores(items[0])
    e_prev = None
    for n, item in enumerate(items):
        s_cur = s_next
        if n + 1 < len(items):
            s_next = scores(items[n + 1])
        e_cur = weights(s_cur)
        if e_prev is not None:
            finish(items[n - 1], e_prev)
        e_prev = e_cur
    finish(items[-1], e_prev)


def neighbourhood_attention(proj, col0, bias):
    b, s, _ = proj.shape
    nq = NA_Q_ROWS * GRID_W
    nblk = s // nq
    hs = NA_HEADS_PER_STEP
    w = hs * HEAD_DIM
    c0 = col0 // w
    groups = HEADS // hs

    def bias_map(bi, h, rb):
        return (h, jnp.where(rb == 0, 0, jnp.where(rb == nblk - 1, 2, 1)), 0, 0, 0)

    return pl.pallas_call(
        _na_kernel,
        grid=(b, groups, nblk),
        in_specs=[
            pl.BlockSpec((1, nq, w), lambda bi, h, rb: (bi, rb, c0 + h)),
            pl.BlockSpec((1, s, w), lambda bi, h, rb: (bi, 0, c0 + groups + h)),
            pl.BlockSpec((1, s, w), lambda bi, h, rb: (bi, 0, c0 + 2 * groups + h)),
            pl.BlockSpec((hs, 1) + bias.shape[2:], bias_map),
        ],
        out_specs=pl.BlockSpec((1, nq, w), lambda bi, h, rb: (bi, rb, h)),
        out_shape=jax.ShapeDtypeStruct((b, s, B_W), BF16),
        compiler_params=_params("parallel", "parallel", "arbitrary"),
        name="neighbourhood_attention",
    )(proj, proj, proj, bias)


HALO = SUBLANES_BF16


def _split3(x):
    hi = x.astype(BF16)
    r1 = x - hi.astype(F32)
    mid = r1.astype(BF16)
    lo = (r1 - mid.astype(F32)).astype(BF16)
    return hi, mid, lo


def _dn_prep_kernel(x_ref, xp_ref, xx_ref, cw_ref, gt_ref, alog_ref, dtb_ref,
                    q_ref, k_ref, v_ref, gc_ref, beta_ref, xs_ref):
    i = pl.program_id(1)
    tb = x_ref.shape[1]
    pad = A_CONV_W // 2
    xs_ref[0:HALO, :] = jnp.where(i == 0, 0.0, xp_ref[0].astype(F32))
    xs_ref[HALO:HALO + tb, :] = x_ref[0].astype(F32)
    xs_ref[HALO + tb:, :] = jnp.where(i == pl.num_programs(1) - 1, 0.0, xx_ref[0].astype(F32))
    cw = cw_ref[...]
    y = xs_ref[pl.ds(HALO - pad, tb), :] * cw[0:1]
    for t in range(1, A_CONV_W):
        y = y + xs_ref[pl.ds(HALO - pad + t, tb), :] * cw[t:t + 1]
    y = y * _sigmoid(y)
    for h in range(HEADS):
        qh = y[:, h * HEAD_DIM:(h + 1) * HEAD_DIM]
        kh = y[:, (HEADS + h) * HEAD_DIM:(HEADS + h + 1) * HEAD_DIM]
        qn = qh * (lax.rsqrt(jnp.sum(qh * qh, axis=-1, keepdims=True) + 1e-6) * (HEAD_DIM ** -0.5))
        kn = kh * lax.rsqrt(jnp.sum(kh * kh, axis=-1, keepdims=True) + 1e-6)
        q_ref[0, h] = qn.astype(q_ref.dtype)
        k_ref[0, h] = kn.astype(k_ref.dtype)
        v_ref[0, h] = y[:, (2 * HEADS + h) * HEAD_DIM:(2 * HEADS + h + 1) * HEAD_DIM].astype(v_ref.dtype)

    nd = 2 * HEADS
    gates = gt_ref[0]
    z = gates[:, 0:nd] + dtb_ref[...]
    softplus = jnp.maximum(z, 0.0) + jnp.log(1.0 + jnp.exp(-jnp.abs(z)))
    g = -jnp.exp(alog_ref[...]) * softplus
    beta_ref[0] = _sigmoid(gates[:, nd:2 * nd])
    r = lax.broadcasted_iota(jnp.int32, (tb, tb), 0)
    c = lax.broadcasted_iota(jnp.int32, (tb, tb), 1)
    lower = jnp.where(r >= c, 1.0, 0.0).astype(BF16)
    upper = jnp.where(r <= c, 1.0, 0.0).astype(BF16)
    parts = _split3(g)
    fwd = _dot(lower, parts[0]) + _dot(lower, parts[1]) + _dot(lower, parts[2])
    bwd = _dot(upper, parts[0]) + _dot(upper, parts[1]) + _dot(upper, parts[2])
    col = lax.broadcasted_iota(jnp.int32, (tb, nd), 1)
    gc_ref[0] = jnp.where(col < HEADS, fwd, bwd)


def deltanet_prep(proj, gates, conv_w, a_log, dt_bias, tb=DN_BLOCK):
    b, s, _ = proj.shape
    nb = s // tb
    rh = tb // HALO
    nd = 2 * HEADS
    hs = jax.ShapeDtypeStruct((b, HEADS, s, HEAD_DIM), BF16)
    head_spec = pl.BlockSpec((1, HEADS, tb, HEAD_DIM), lambda bi, i: (bi, 0, i, 0))
    vec_spec = pl.BlockSpec((1, tb, nd), lambda bi, i: (bi, i, 0))
    return pl.pallas_call(
        _dn_prep_kernel,
        grid=(b, nb),
        in_specs=[
            pl.BlockSpec((1, tb, A_QKV_W), lambda bi, i: (bi, i, 0)),
            pl.BlockSpec((1, HALO, A_QKV_W), lambda bi, i: (bi, jnp.maximum(i * rh - 1, 0), 0)),
            pl.BlockSpec((1, HALO, A_QKV_W), lambda bi, i: (bi, jnp.minimum((i + 1) * rh, s // HALO - 1), 0)),
            pl.BlockSpec((A_CONV_W, A_QKV_W), lambda bi, i: (0, 0)),
            pl.BlockSpec((1, tb, LANES), lambda bi, i: (bi, i, 0)),
            pl.BlockSpec((1, nd), lambda bi, i: (0, 0)),
            pl.BlockSpec((1, nd), lambda bi, i: (0, 0)),
        ],
        out_specs=[head_spec, head_spec, head_spec, vec_spec, vec_spec],
        out_shape=[hs, hs, hs, jax.ShapeDtypeStruct((b, s, nd), F32), jax.ShapeDtypeStruct((b, s, nd), F32)],
        scratch_shapes=[pltpu.VMEM((tb + 2 * HALO, A_QKV_W), F32)],
        compiler_params=_params("parallel", "arbitrary"),
        name="deltanet_prep",
    )(proj, proj, proj, conv_w, gates, a_log.reshape(1, nd), dt_bias.reshape(1, nd))


def _split2(x):
    hi = x.astype(BF16)
    return hi, (x - hi.astype(F32)).astype(BF16)


def _unit_triangular_solve(ms, rhss, same_block):
    n = ms[0].shape[0]
    grp = DN_GROUP
    ng = n // grp
    r = lax.broadcasted_iota(jnp.int32, (grp, grp), 0)
    c = lax.broadcasted_iota(jnp.int32, (grp, grp), 1)
    base_mask = (r // DN_BASE) == (c // DN_BASE)
    row = lax.broadcasted_iota(jnp.int32, (DN_BASE, grp), 0)
    lane = lax.broadcasted_iota(jnp.int32, (DN_BASE, grp), 1)
    left = lane < DN_BASE

    def pair_lhs(hi, lo):
        hi_f, lo_f = hi.astype(F32), lo.astype(F32)
        swapped = pltpu.roll(hi_f, DN_BASE, 1)
        top = jnp.concatenate([jnp.where(left, hi_f, swapped), jnp.where(left, lo_f, 0.0)], axis=1)
        bottom = jnp.concatenate([jnp.where(left, swapped, hi_f), jnp.where(left, 0.0, lo_f)], axis=1)
        return jnp.concatenate([top, bottom], axis=0).astype(BF16)

    def pair_product(lhs, rhs_hi, rhs_lo):
        res = _dot(lhs, jnp.concatenate([rhs_hi, rhs_lo, rhs_hi, rhs_hi], axis=0))
        return jnp.where(left, res[:DN_BASE], res[DN_BASE:])

    pieces = [jnp.where(base_mask, m[g * grp:(g + 1) * grp, g * grp:(g + 1) * grp], 0.0)
              for m in ms for g in range(ng)]
    pws = [piece[:DN_BASE] + piece[DN_BASE:] for piece in pieces]
    ts = [jnp.where(lane % DN_BASE == row, 1.0, 0.0) + pw for pw in pws]
    splits = [_split2(pw) for pw in pws]
    lhss = [pair_lhs(hi, lo) for hi, lo in splits]
    size = 2
    while size < DN_BASE:
        pws = [pair_product(lhs, hi, lo) for lhs, (hi, lo) in zip(lhss, splits)]
        splits = [_split2(pw) for pw in pws]
        lhss = [pair_lhs(hi, lo) for hi, lo in splits]
        ts = [t + pair_product(lhs, *_split2(t)) for lhs, t in zip(lhss, ts)]
        size *= 2
    t_bases = [jnp.concatenate([jnp.where(left, t, 0.0), jnp.where(left, 0.0, t)], axis=0).astype(BF16)
               for t in ts]
    offs = {}
    blk = 2 * DN_BASE
    while blk <= n:
        offs[blk] = [jnp.where(same_block(blk), jnp.where(same_block(blk // 2), 0.0, m), 0.0).astype(BF16)
                     for m in ms]
        blk *= 2

    def apply(blk, ys):
        if blk == DN_BASE:
            ybs = [y.astype(BF16) for y in ys]
            return [jnp.concatenate([_dot(t_bases[i * ng + g], yb[g * grp:(g + 1) * grp]) for g in range(ng)],
                                    axis=0) for i, yb in enumerate(ybs)]
        zs = apply(blk // 2, ys)
        corr = apply(blk // 2, [_dot(off, z.astype(BF16)) for off, z in zip(offs[blk], zs)])
        return [z + cr for z, cr in zip(zs, corr)]

    return apply(n, rhss)


def _dn_block(chains):
    n = chains[0][0].shape[0]
    r = lax.broadcasted_iota(jnp.int32, (n, n), 0)
    c = lax.broadcasted_iota(jnp.int32, (n, n), 1)

    def same_block(size):
        return (r // size) == (c // size)

    ms, rhss, pre = [], [], []
    for q, k, v, gcol, grow, beta, state_ref, forward in chains:
        d = (r - c) if forward else (c - r)
        kf = k.astype(F32)
        kb = kf * beta
        decay = jnp.exp(jnp.where(d >= 0, gcol - grow, NEG_BIG))
        ms.append(jnp.where(d > 0, -(_dot_nt(kb.astype(BF16), k) * decay), 0.0))
        intra = (_dot_nt(q, k) * decay).astype(BF16)
        eg = jnp.exp(gcol)
        rhss.append(jnp.concatenate([v.astype(F32) * beta, kb * eg], axis=1))
        g_last = grow[:, n - 1:n] if forward else grow[:, 0:1]
        pre.append((intra, (q.astype(F32) * eg).astype(BF16), (kf * jnp.exp(g_last - gcol)).astype(BF16),
                    jnp.exp(g_last)))
    sols = _unit_triangular_solve(ms, rhss, same_block)
    outs = []
    for chain, sol, (intra, q_dec, k_dec, blk_decay) in zip(chains, sols, pre):
        state_ref = chain[6]
        state = state_ref[...]
        sb = state.astype(BF16)
        v_new = sol[:, :HEAD_DIM] - _dot(sol[:, HEAD_DIM:].astype(BF16), sb)
        vb = v_new.astype(BF16)
        outs.append(_dot(q_dec, sb) + _dot(intra, vb))
        state_ref[...] = state * blk_decay + _dot_tn(k_dec, vb)
    return outs


def _dn_kernel(qf_ref, kf_ref, vf_ref, qb_ref, kb_ref, vb_ref, gcf_ref, gcb_ref, btf_ref, btb_ref,
               grf_ref, grb_ref, of_ref, ob_ref, state_ref):
    hb = qf_ref.shape[1]

    @pl.when(pl.program_id(2) == 0)
    def _():
        state_ref[...] = jnp.zeros_like(state_ref)

    lane = lax.broadcasted_iota(jnp.int32, gcf_ref.shape[1:], 1)

    def pick(ref, idx):
        return jnp.sum(jnp.where(lane == idx, ref[0], 0.0), axis=-1, keepdims=True)

    chains = []
    for hl in range(hb):
        h = pl.program_id(1) * hb + hl
        chains.append((qf_ref[0, hl], kf_ref[0, hl], vf_ref[0, hl], pick(gcf_ref, h),
                       grf_ref[0, pl.ds(h, 1), :], pick(btf_ref, h), state_ref.at[0, hl], True))
        chains.append((qb_ref[0, hl], kb_ref[0, hl], vb_ref[0, hl], pick(gcb_ref, HEADS + h),
                       grb_ref[0, pl.ds(HEADS + h, 1), :], pick(btb_ref, HEADS + h), state_ref.at[1, hl], False))
    outs = _dn_block(chains)
    for hl in range(hb):
        of_ref[0, hl] = outs[2 * hl]
        ob_ref[0, hl] = outs[2 * hl + 1]


def deltanet_scan(q, k, v, gc, beta, gc_rows, tb=DN_BLOCK, hb=DN_HEADS_PER_STEP):
    b, _, s, _ = q.shape
    nb = s // tb
    nd = 2 * HEADS
    fwd = pl.BlockSpec((1, hb, tb, HEAD_DIM), lambda bi, h, c: (bi, h, c, 0))
    bwd = pl.BlockSpec((1, hb, tb, HEAD_DIM), lambda bi, h, c: (bi, h, nb - 1 - c, 0))
    vec_f = pl.BlockSpec((1, tb, nd), lambda bi, h, c: (bi, c, 0))
    vec_b = pl.BlockSpec((1, tb, nd), lambda bi, h, c: (bi, nb - 1 - c, 0))
    row_f = pl.BlockSpec((1, nd, tb), lambda bi, h, c: (bi, 0, c))
    row_b = pl.BlockSpec((1, nd, tb), lambda bi, h, c: (bi, 0, nb - 1 - c))
    os_ = jax.ShapeDtypeStruct((b, HEADS, s, HEAD_DIM), F32)
    return pl.pallas_call(
        _dn_kernel,
        grid=(b, HEADS // hb, nb),
        in_specs=[fwd, fwd, fwd, bwd, bwd, bwd, vec_f, vec_b, vec_f, vec_b, row_f, row_b],
        out_specs=[fwd, bwd],
        out_shape=[os_, os_],
        scratch_shapes=[pltpu.VMEM((2, hb, HEAD_DIM, HEAD_DIM), F32)],
        compiler_params=_params("parallel", "parallel", "arbitrary"),
        name="deltanet_scan",
    )(q, k, v, q, k, v, gc, gc, beta, beta, gc_rows, gc_rows)


def _dn_out_kernel(of_ref, ob_ref, z_ref, nw_ref, o_ref):
    nw = nw_ref[...]
    for h in range(HEADS):
        o = _rms(of_ref[0, h] + ob_ref[0, h], nw)
        z = z_ref[0, :, h * HEAD_DIM:(h + 1) * HEAD_DIM].astype(F32)
        o_ref[0, :, h * HEAD_DIM:(h + 1) * HEAD_DIM] = (o * (z * _sigmoid(z))).astype(o_ref.dtype)


def deltanet_out(o_f, o_b, proj, z_col0, out_norm, ts=ROW_TILE):
    b, _, s, _ = o_f.shape
    head_spec = pl.BlockSpec((1, HEADS, ts, HEAD_DIM), lambda bi, i: (bi, 0, i, 0))
    zb = z_col0 // A_V_W
    return pl.pallas_call(
        _dn_out_kernel,
        grid=(b, s // ts),
        in_specs=[head_spec, head_spec,
                  pl.BlockSpec((1, ts, A_V_W), lambda bi, i: (bi, i, zb)),
                  pl.BlockSpec((1, HEAD_DIM), lambda bi, i: (0, 0))],
        out_specs=pl.BlockSpec((1, ts, A_V_W), lambda bi, i: (bi, i, 0)),
        out_shape=jax.ShapeDtypeStruct((b, s, A_V_W), BF16),
        compiler_params=_params("parallel", "parallel"),
        name="deltanet_out",
    )(o_f, o_b, proj, out_norm.reshape(1, HEAD_DIM))


def _rope_tables(s):
    half = ROPE_DIMS // 2
    inv = ROPE_THETA ** (-jnp.arange(0, ROPE_DIMS, 2, dtype=F32) / ROPE_DIMS)
    ang = jnp.arange(s, dtype=F32)[:, None] * inv[None, :]
    cos, sin = jnp.cos(ang), jnp.sin(ang)
    rest = HEAD_DIM - ROPE_DIMS
    c = jnp.concatenate([cos, cos, jnp.ones((s, rest), F32)], axis=1)
    s1 = jnp.concatenate([jnp.zeros((s, half), F32), sin, jnp.zeros((s, rest), F32)], axis=1)
    s2 = jnp.concatenate([-sin, jnp.zeros((s, half + rest), F32)], axis=1)
    return c, s1, s2


def _trunk(x_a, x_b, p_a, p_b, ab_w_in, ab_conv_w, ab_a_log, ab_dt_bias, ab_out_norm, ab_rpb, ab_w_out,
           c_w_in, c_lambda, c_subln, c_w_out, norms, ffn_w_in, ffn_conv_w, ffn_conv_b,
           ffn_w_out, ple_w_proj, ple_w_gate):
    b_a, s, d = x_a.shape
    b = b_a + x_b.shape[0]
    t, t_a = b * s, b_a * s
    depth = norms.shape[0]
    depth_p = p_a.shape[0]
    h = jnp.concatenate([x_a, x_b], axis=0).reshape(t, d)
    rope = _rope_tables(s)
    o1 = A_QKV_W
    o2 = o1 + A_V_W
    o3 = o2 + 4 * HEADS
    ab_main = jnp.concatenate([ab_w_in[:, :, :o2], ab_w_in[:, :, o3:]], axis=2).astype(BF16)
    ab_gate = jnp.pad(ab_w_in[:, :, o2:o3], ((0, 0), (0, 0), (0, LANES - 4 * HEADS))).astype(BF16)
    c_in = c_w_in.astype(BF16)
    ffn_in = ffn_w_in.astype(BF16)
    ab_out, c_out, ffn_out = ab_w_out.astype(BF16), c_w_out.astype(BF16), ffn_w_out.astype(BF16)
    ple_gate, ple_proj = ple_w_gate.astype(BF16), ple_w_proj.astype(BF16)
    for layer in range(depth):
        j = layer // 2
        if layer % 2 == 0:
            proj, gates = norm_matmul(h, norms[layer, 0], ab_main, j, BF16, side_w=ab_gate)
            proj, gates = proj.reshape(b, s, -1), gates.reshape(b, s, LANES)
            qa, ka, va, gc, beta = deltanet_prep(proj, gates, ab_conv_w[j], ab_a_log[j], ab_dt_bias[j])
            o_f, o_b = deltanet_scan(qa, ka, va, gc, beta, jnp.transpose(gc, (0, 2, 1)))
            o_a = deltanet_out(o_f, o_b, proj, o1, ab_out_norm[j])
            o_nb = neighbourhood_attention(proj, o2, _na_bias(ab_rpb[j]))
            h, xn = proj_norm_res([o_a.reshape(t, -1), o_nb.reshape(t, -1)], ab_out, j, norms[layer, 1],
                                  norms[layer, 2], h)
        else:
            proj = norm_matmul(h, norms[layer, 0], c_in, j, BF16, rope=rope,
                               rope_tiles=2 * C_QK_W // COL_TILE, seq_len=s).reshape(b, s, -1)
            o_c = diff_attention(proj, c_lambda[j], c_subln[j], layer)
            h, xn = proj_norm_res([o_c.reshape(t, -1)], c_out, j, norms[layer, 1], norms[layer, 2], h)
        f = conv_ffn(xn, ffn_in, ffn_out, layer, ffn_conv_w[layer], ffn_conv_b[layer], norms[layer, 3], s)
        ple_args = (f, p_a.reshape(depth_p, t_a, -1), p_b.reshape(depth_p, t - t_a, -1), ple_gate, ple_proj, layer)
        if layer + 1 < depth:
            h = ple(h, *ple_args)
    y_a = ple(h, *ple_args, row0=0, rows=t_a)
    y_b = ple(h, *ple_args, row0=t_a, rows=t - t_a)
    return y_a.reshape(b_a, s, d), y_b.reshape(b - b_a, s, d)


def kernel(x_prompt, x_sample, p_prompt, p_sample, ab_w_in, ab_conv_w, ab_a_log, ab_dt_bias, ab_out_norm,
           ab_rpb, ab_w_out, c_w_in, c_lambda, c_subln, c_w_out, norms, ffn_w_in, ffn_conv_w, ffn_conv_b,
           ffn_w_out, ple_w_proj, ple_w_gate):
    return _trunk(x_prompt, x_sample, p_prompt, p_sample, ab_w_in, ab_conv_w, ab_a_log, ab_dt_bias,
                  ab_out_norm, ab_rpb, ab_w_out, c_w_in, c_lambda, c_subln, c_w_out, norms, ffn_w_in,
                  ffn_conv_w, ffn_conv_b, ffn_w_out, ple_w_proj, ple_w_gate)
```
